```python
import math
import jax, jax.numpy as jnp
from jax import lax
import numpy as np

D_MODEL = 1024
BATCH = 2
SEQ = 8192
DEPTH = 2
DEC_BATCH = 128
DEC_SEQ = 1
PAST_LEN = 8192
PAGE_SIZE = 128

N_HEADS = 16
N_KV_HEADS = 4
GROUP = N_HEADS // N_KV_HEADS
HEAD_DIM = D_MODEL // N_HEADS
Q_W = N_HEADS * HEAD_DIM
KV_W = N_KV_HEADS * HEAD_DIM
WINDOW = 128
BLOCK = 128
N_BUCKETS = 32
MAX_DISTANCE = 128
D_CONV = D_MODEL
CONV_W = 3
D_FF = 2816
RMS_EPS = 1e-6
NEG = -1e30
W_BUF = min(WINDOW, PAST_LEN)
IN_W = 3 * D_CONV + Q_W + 2 * KV_W + 2 * D_MODEL
IN_SPLITS = (D_CONV, 2 * D_CONV, 3 * D_CONV, 3 * D_CONV + Q_W, 3 * D_CONV + Q_W + KV_W, 3 * D_CONV + Q_W + 2 * KV_W, 3 * D_CONV + Q_W + 2 * KV_W + D_MODEL)

kernel_name = "hybrid_conv_swa_sink_macaron_step"


def rmsnorm(x, g):
    xf = x.astype(jnp.float32)
    r = lax.rsqrt(jnp.mean(xf * xf, axis=-1, keepdims=True) + RMS_EPS)
    return (xf * r).astype(x.dtype) * g


def swiglu(h, w_gu, w_down):
    gate, up = jnp.split(h @ w_gu, 2, axis=-1)
    return (jax.nn.silu(gate) * up) @ w_down


def t5_bucket(rel):
    n = jnp.maximum(rel, 0)
    max_exact = N_BUCKETS // 2
    nf = jnp.maximum(n, 1).astype(jnp.float32)
    large = max_exact + (jnp.log(nf / max_exact) / math.log(MAX_DISTANCE / max_exact) * (N_BUCKETS - max_exact)).astype(jnp.int32)
    large = jnp.minimum(large, N_BUCKETS - 1)
    return jnp.where(n < max_exact, n, large)


def rel_bias_logits(rel, rel_bias):
    b = rel_bias.astype(jnp.float32)[t5_bucket(rel)]
    return jnp.moveaxis(b, -1, 0).reshape(N_KV_HEADS, GROUP, *rel.shape)


def sink_attend(q, k, v, bias, mask, sinks):
    s = jnp.einsum('...qhgd,...khd->...hgqk', q.astype(jnp.float32), k.astype(jnp.float32)) * (HEAD_DIM ** -0.5) + bias
    s = jnp.where(mask, s, NEG)
    sink = sinks.astype(jnp.float32).reshape(N_KV_HEADS, GROUP, 1, 1)
    m = jnp.maximum(jnp.max(s, axis=-1, keepdims=True), sink)
    p = jnp.exp(s - m)
    denom = jnp.sum(p, axis=-1, keepdims=True) + jnp.exp(sink - m)
    o = jnp.einsum('...hgqk,...khd->...qhgd', p / denom, v.astype(jnp.float32))
    return o.astype(v.dtype)


def banded_window_attention(q, k, v, sinks, rel_bias):
    b_, s_ = q.shape[0], q.shape[1]
    nb = s_ // BLOCK
    qb = q.reshape(b_, nb, BLOCK, N_KV_HEADS, GROUP, HEAD_DIM)

    def band(t):
        tb = t.reshape(b_, nb, BLOCK, N_KV_HEADS, HEAD_DIM)
        prev = jnp.concatenate([jnp.zeros_like(tb[:, :1]), tb[:, :-1]], axis=1)
        return jnp.concatenate([prev, tb], axis=2)

    qi = jnp.arange(BLOCK, dtype=jnp.int32)
    kj = jnp.arange(2 * BLOCK, dtype=jnp.int32)
    rel = qi[:, None] + BLOCK - kj[None, :]
    bias = rel_bias_logits(rel, rel_bias)
    key_abs = jnp.arange(nb, dtype=jnp.int32)[:, None] * BLOCK - BLOCK + kj[None, :]
    mask = ((rel >= 0) & (rel < WINDOW))[None] & (key_abs >= 0)[:, None, :]
    o = sink_attend(qb, band(k), band(v), bias, mask[:, None, None], sinks)
    w = min(WINDOW, s_)
    return o.reshape(b_, s_, N_KV_HEADS, GROUP, HEAD_DIM), k[:, -w:], v[:, -w:]


def decode_window_attention(q, k, v, k_buf, v_buf, sinks, rel_bias):
    t_ = q.shape[1]
    wb = k_buf.shape[1]
    kc = jnp.concatenate([k_buf.astype(k.dtype), k], axis=1)
    vc = jnp.concatenate([v_buf.astype(v.dtype), v], axis=1)
    qpos = PAST_LEN + jnp.arange(t_, dtype=jnp.int32)
    kpos = jnp.concatenate([PAST_LEN - wb + jnp.arange(wb, dtype=jnp.int32), qpos])
    rel = qpos[:, None] - kpos[None, :]
    bias = rel_bias_logits(rel, rel_bias)
    mask = (rel >= 0) & (rel < WINDOW)
    o = sink_attend(q, kc, vc, bias, mask, sinks)
    return o, kc[:, -wb:], vc[:, -wb:]


def token_mixer(h, conv_state, k_buf, v_buf, w_in, conv_w, w_conv_out, w_attn_out, w_out, sinks, rel_bias):
    prompt = conv_state is None
    nb_, t_ = h.shape[0], h.shape[1]
    cb, cc, cx, q, k, v, ga, gb = jnp.split(h @ w_in, IN_SPLITS, axis=-1)
    u = cc * cx
    if prompt:
        upad = jnp.concatenate([jnp.zeros((nb_, CONV_W - 1, D_CONV), u.dtype), u], axis=1)
    else:
        upad = jnp.concatenate([conv_state.astype(u.dtype), u], axis=1)
    yc = conv_w[0] * upad[:, 0:t_]
    for i in range(1, CONV_W):
        yc = yc + conv_w[i] * upad[:, i:i + t_]
    a_out = (cb * yc) @ w_conv_out
    new_conv = upad[:, -(CONV_W - 1):]
    q = q.reshape(nb_, t_, N_KV_HEADS, GROUP, HEAD_DIM)
    k = k.reshape(nb_, t_, N_KV_HEADS, HEAD_DIM)
    v = v.reshape(nb_, t_, N_KV_HEADS, HEAD_DIM)
    if prompt:
        o, new_k, new_v = banded_window_attention(q, k, v, sinks, rel_bias)
    else:
        o, new_k, new_v = decode_window_attention(q, k, v, k_buf, v_buf, sinks, rel_bias)
    att = o.reshape(nb_, t_, Q_W) @ w_attn_out
    merged = jax.nn.sigmoid(ga) * a_out + jax.nn.sigmoid(gb) * att
    return merged @ w_out, new_conv, new_k, new_v


def decoder_layer(x, conv_state, k_buf, v_buf, g, w_ff1_gu, w_ff1_down, w_in, conv_w, sinks, w_conv_out, w_attn_out, w_out, w_ff2_gu, w_ff2_down, rel_bias):
    x = x + 0.5 * rmsnorm(swiglu(rmsnorm(x, g[0]), w_ff1_gu, w_ff1_down), g[1])
    m, new_conv, new_k, new_v = token_mixer(rmsnorm(x, g[2]), conv_state, k_buf, v_buf, w_in, conv_w, w_conv_out, w_attn_out, w_out, sinks, rel_bias)
    x = x + rmsnorm(m, g[3])
    x = x + 0.5 * rmsnorm(swiglu(rmsnorm(x, g[4]), w_ff2_gu, w_ff2_down), g[5])
    return x, new_conv, new_k, new_v


def setup_inputs(seed: int = 0) -> dict:
    key = jax.random.key(seed)
    ks = jax.random.split(key, 20)
    f32 = jnp.float32
    nrm = lambda k, shape, scale: jax.random.normal(k, shape, f32) * scale
    return {
        "x_prompt": nrm(ks[0], (BATCH, SEQ, D_MODEL), 1.0),
        "x_sample": nrm(ks[1], (DEC_BATCH, DEC_SEQ, D_MODEL), 1.0),
        "state_conv": nrm(ks[2], (DEPTH, DEC_BATCH, CONV_W - 1, D_CONV), 1.0),
        "cache_k_win": nrm(ks[3], (DEPTH, DEC_BATCH, W_BUF, N_KV_HEADS, HEAD_DIM), 1.0),
        "cache_v_win": nrm(ks[4], (DEPTH, DEC_BATCH, W_BUF, N_KV_HEADS, HEAD_DIM), 1.0),
        "rel_bias": nrm(ks[5], (N_BUCKETS, N_HEADS), 0.5),
        "norm_g": 1.0 + nrm(ks[6], (DEPTH, 6, D_MODEL), 0.05),
        "w_ff1_gu": nrm(ks[7], (DEPTH, D_MODEL, 2 * D_FF), D_MODEL ** -0.5),
        "w_ff1_down": nrm(ks[8], (DEPTH, D_FF, D_MODEL), D_FF ** -0.5),
        "w_in": nrm(ks[9], (DEPTH, D_MODEL, IN_W), D_MODEL ** -0.5),
        "conv_w": nrm(ks[10], (DEPTH, CONV_W, D_CONV), CONV_W ** -0.5),
        "sinks": nrm(ks[11], (DEPTH, N_HEADS), 0.5),
        "w_conv_out": nrm(ks[12], (DEPTH, D_CONV, D_MODEL), D_CONV ** -0.5),
        "w_attn_out": nrm(ks[13], (DEPTH, Q_W, D_MODEL), Q_W ** -0.5),
        "w_out": nrm(ks[14], (DEPTH, D_MODEL, D_MODEL), D_MODEL ** -0.5),
        "w_ff2_gu": nrm(ks[15], (DEPTH, D_MODEL, 2 * D_FF), D_MODEL ** -0.5),
        "w_ff2_down": nrm(ks[16], (DEPTH, D_FF, D_MODEL), D_FF ** -0.5),
    }


def reference(x_prompt, x_sample, state_conv, cache_k_win, cache_v_win, rel_bias, norm_g, w_ff1_gu, w_ff1_down, w_in, conv_w, sinks, w_conv_out, w_attn_out, w_out, w_ff2_gu, w_ff2_down):
    yp, ys = x_prompt, x_sample
    pc, pk, pv, sc, sk, sv = [], [], [], [], [], []
    for l in range(DEPTH):
        yp, c1, k1, v1 = decoder_layer(yp, None, None, None, norm_g[l], w_ff1_gu[l], w_ff1_down[l], w_in[l], conv_w[l], sinks[l], w_conv_out[l], w_attn_out[l], w_out[l], w_ff2_gu[l], w_ff2_down[l], rel_bias)
        ys, c2, k2, v2 = decoder_layer(ys, state_conv[l], cache_k_win[l], cache_v_win[l], norm_g[l], w_ff1_gu[l], w_ff1_down[l], w_in[l], conv_w[l], sinks[l], w_conv_out[l], w_attn_out[l], w_out[l], w_ff2_gu[l], w_ff2_down[l], rel_bias)
        pc.append(c1); pk.append(k1); pv.append(v1)
        sc.append(c2); sk.append(k2); sv.append(v2)
    return (yp, ys, jnp.stack(pc), jnp.stack(pk), jnp.stack(pv), jnp.stack(sc), jnp.stack(sk), jnp.stack(sv))
```

```python
import functools

import numpy as np
import jax
import jax.numpy as jnp
from jax import lax
from jax.experimental import pallas as pl
from jax.experimental.pallas import tpu as pltpu

F32 = jnp.float32
BF16 = jnp.bfloat16

D_MODEL = 1024
D_FF = 2816
N_HEADS = 16
N_KV = 4
GROUP = N_HEADS // N_KV
HEAD_DIM = 64
KV_W = N_KV * HEAD_DIM
WINDOW = 128
BLOCK = 128
N_BUCKETS = 32
MAX_DISTANCE = 128
RMS_EPS = 1e-6
NEG = -1e30
DEPTH = 2

C_CB, C_CC, C_CX, C_Q, C_K, C_V, C_GA, C_GB, C_END = (
    0, 1024, 2048, 3072, 4096, 4352, 4608, 5632, 6656)

VMEM_LIMIT_BYTES = 56 * 1024 * 1024

FFN_ROWS = 256
FFN_CHUNK = 256
MIX_ROWS = 256
DEC_SEQS = 8


def _t5_bucket_np(rel):
    n = np.maximum(rel, 0)
    max_exact = N_BUCKETS // 2
    nf = np.maximum(n, 1).astype(np.float32)
    large = max_exact + (
        np.log(nf / max_exact) / np.log(MAX_DISTANCE / max_exact)
        * (N_BUCKETS - max_exact)).astype(np.int32)
    large = np.minimum(large, N_BUCKETS - 1)
    return np.where(n < max_exact, n, large).astype(np.int32)


def _band_tables():
    qi = np.arange(BLOCK)[:, None]
    kj = np.arange(2 * BLOCK)[None, :]
    rel = qi + BLOCK - kj
    bucket = _t5_bucket_np(rel)
    band = (rel >= 0) & (rel < WINDOW)
    mask = np.stack([band, band & (kj >= BLOCK)]).astype(np.float32)
    return bucket, mask


_BUCKET_NP, _MASK_NP = _band_tables()


def _rms(x, g):
    r = lax.rsqrt(jnp.mean(x * x, axis=-1, keepdims=True) + RMS_EPS)
    return (x * r) * g


def _dot(a, b):
    return jnp.dot(a, b, preferred_element_type=F32)


def _dot_t(a, b):
    return lax.dot_general(a, b, (((1,), (1,)), ((), ())),
                           preferred_element_type=F32)


def _resident(shape):
    return pl.BlockSpec(shape, lambda *_: (0,) * len(shape),
                        pipeline_mode=pl.Buffered(1))


def _bias_kernel(rb_ref, bucket_ref, out_ref):
    g = pl.program_id(0)
    hk = pl.program_id(1)
    h = hk * GROUP + g
    bucket = bucket_ref[...]

    def body(b, acc):
        return jnp.where(bucket == b, rb_ref[b, h], acc)

    out_ref[...] = lax.fori_loop(0, N_BUCKETS, body,
                                 jnp.zeros((BLOCK, 2 * BLOCK), F32))


def _bias_table(rel_bias):
    return pl.pallas_call(
        _bias_kernel,
        grid=(GROUP, N_KV),
        in_specs=[
            pl.BlockSpec(memory_space=pltpu.SMEM),
            pl.BlockSpec((BLOCK, 2 * BLOCK), lambda g, hk: (0, 0)),
        ],
        out_specs=pl.BlockSpec((BLOCK, 2 * BLOCK), lambda g, hk: (g, hk)),
        out_shape=jax.ShapeDtypeStruct((GROUP * BLOCK, N_KV * 2 * BLOCK), F32),
        name="bias_table",
    )(rel_bias, jnp.asarray(_BUCKET_NP))


def _ffn_kernel(x_ref, gin_ref, gout_ref, wgu_ref, wd_ref, o_ref, act_ref):
    x = x_ref[...]
    h = _rms(x, gin_ref[...]).astype(BF16)
    for c in range(D_FF // FFN_CHUNK):
        lo = c * FFN_CHUNK
        gate = _dot(h, wgu_ref[:, lo:lo + FFN_CHUNK])
        up = _dot(h, wgu_ref[:, D_FF + lo:D_FF + lo + FFN_CHUNK])
        act_ref[:, lo:lo + FFN_CHUNK] = (
            gate * jax.nn.sigmoid(gate) * up).astype(BF16)
    y = _dot(act_ref[...], wd_ref[...])
    o_ref[...] = x + 0.5 * _rms(y, gout_ref[...])


def _ffn(x, g_in, g_out, w_gu, w_down):
    m = x.shape[0]
    tm = min(FFN_ROWS, m)
    return pl.pallas_call(
        _ffn_kernel,
        grid=(m // tm,),
        in_specs=[
            pl.BlockSpec((tm, D_MODEL), lambda i: (i, 0)),
            _resident((1, D_MODEL)),
            _resident((1, D_MODEL)),
            _resident((D_MODEL, 2 * D_FF)),
            _resident((D_FF, D_MODEL)),
        ],
        out_specs=pl.BlockSpec((tm, D_MODEL), lambda i: (i, 0)),
        out_shape=jax.ShapeDtypeStruct((m, D_MODEL), F32),
        scratch_shapes=[pltpu.VMEM((tm, D_FF), BF16)],
        compiler_params=pltpu.CompilerParams(
            dimension_semantics=("arbitrary",),
            vmem_limit_bytes=VMEM_LIMIT_BYTES),
        name="ffn_half",
    )(x, g_in.reshape(1, D_MODEL), g_out.reshape(1, D_MODEL), w_gu, w_down)


def _lane_segment_mask(rows):
    lane = lax.broadcasted_iota(jnp.int32, (rows, KV_W), 1)
    return [(lane >= hk * HEAD_DIM) & (lane < (hk + 1) * HEAD_DIM)
            for hk in range(N_KV)]


def _mixer_prompt_kernel(x_ref, g2_ref, g3_ref, win_ref, cw_ref, wco_ref,
                         wao_ref, wo_ref, bt_ref, mt_ref, sinks_ref,
                         y_ref, cst_ref, kout_ref, vout_ref,
                         h_s, q_s, kbuf, vbuf, upad, ag_s, o_s, p_s,
                         *, layer, rows):
    i = pl.program_id(1)
    last = pl.num_programs(1) - 1
    nq = rows // BLOCK

    @pl.when(i == 0)
    def _():
        upad[0:8, :] = jnp.zeros((8, D_MODEL), F32)
        kbuf[0:BLOCK, :] = jnp.zeros((BLOCK, KV_W), BF16)
        vbuf[0:BLOCK, :] = jnp.zeros((BLOCK, KV_W), BF16)

    @pl.when(i > 0)
    def _():
        upad[0:8, :] = upad[rows:rows + 8, :]
        kbuf[0:BLOCK, :] = kbuf[rows:rows + BLOCK, :]
        vbuf[0:BLOCK, :] = vbuf[rows:rows + BLOCK, :]

    x = x_ref[...]
    h = _rms(x, g2_ref[...]).astype(BF16)
    h_s[...] = h

    u = _dot(h, win_ref[:, C_CC:C_CX]) * _dot(h, win_ref[:, C_CX:C_Q])
    upad[8:rows + 8, :] = u
    yc = (cw_ref[0:1, :] * upad[6:rows + 6, :]
          + cw_ref[1:2, :] * upad[7:rows + 7, :]
          + cw_ref[2:3, :] * u)
    a_in = (_dot(h, win_ref[:, C_CB:C_CC]) * yc).astype(BF16)
    a_out = _dot(a_in, wco_ref[...])
    ag_s[...] = jax.nn.sigmoid(_dot(h_s[...], win_ref[:, C_GA:C_GB])) * a_out

    q_s[...] = (_dot(h_s[...], win_ref[:, C_Q:C_K])
                * (HEAD_DIM ** -0.5)).astype(BF16)
    k = _dot(h_s[...], win_ref[:, C_K:C_V])
    v = _dot(h_s[...], win_ref[:, C_V:C_GA])
    kbuf[BLOCK:BLOCK + rows, :] = k.astype(BF16)
    vbuf[BLOCK:BLOCK + rows, :] = v.astype(BF16)

    @pl.when(i == last)
    def _():
        cst_ref[...] = u[rows - 2:rows, :]
        kout_ref[...] = k[rows - BLOCK:rows, :]
        vout_ref[...] = v[rows - BLOCK:rows, :]

    seg = _lane_segment_mask(2 * BLOCK)
    seg_q = _lane_segment_mask(BLOCK)
    zero_kv = jnp.zeros((2 * BLOCK, KV_W), BF16)

    def attend(j, carry):
        r0 = pl.multiple_of(j * BLOCK, BLOCK)
        qj = q_s[pl.ds(r0, BLOCK), :]
        kw = kbuf[pl.ds(r0, 2 * BLOCK), :]
        vw = vbuf[pl.ds(r0, 2 * BLOCK), :]
        qs = jnp.concatenate(
            [qj[:, g * KV_W:(g + 1) * KV_W] for g in range(GROUP)], axis=0)
        kb = jnp.concatenate(
            [jnp.where(seg[hk], kw, zero_kv) for hk in range(N_KV)], axis=0)
        vb = jnp.concatenate(
            [jnp.where(seg[hk], vw, zero_kv) for hk in range(N_KV)], axis=0)
        s_all = _dot_t(qs, kb)
        first = jnp.logical_and(i == 0, j == 0).astype(jnp.int32)
        valid = mt_ref[first] > 0.0
        inv = []
        for g in range(GROUP):
            inv_g = []
            for hk in range(N_KV):
                rs = slice(g * BLOCK, (g + 1) * BLOCK)
                cs = slice(hk * 2 * BLOCK, (hk + 1) * 2 * BLOCK)
                s = jnp.where(valid, s_all[rs, cs] + bt_ref[rs, cs], NEG)
                sink = sinks_ref[layer, hk * GROUP + g]
                m = jnp.maximum(jnp.max(s, axis=-1, keepdims=True), sink)
                p = jnp.exp(s - m)
                den = jnp.sum(p, axis=-1, keepdims=True) + jnp.exp(sink - m)
                p_s[rs, cs] = p.astype(BF16)
                inv_g.append(1.0 / den)
            inv.append(inv_g)
        o_all = _dot(p_s[...], vb)
        for g in range(GROUP):
            scale = jnp.where(
                seg_q[0], inv[g][0],
                jnp.where(seg_q[1], inv[g][1],
                          jnp.where(seg_q[2], inv[g][2], inv[g][3])))
            o_s[pl.ds(r0, BLOCK), g * KV_W:(g + 1) * KV_W] = (
                o_all[g * BLOCK:(g + 1) * BLOCK, :] * scale).astype(BF16)
        return carry

    lax.fori_loop(0, nq, attend, 0)

    att = _dot(o_s[...], wao_ref[...])
    gate_b = jax.nn.sigmoid(_dot(h_s[...], win_ref[:, C_GB:C_END]))
    merged = (ag_s[...] + gate_b * att).astype(BF16)
    y_ref[...] = x_ref[...] + _rms(_dot(merged, wo_ref[...]), g3_ref[...])


def _mixer_prompt(x, batch, g2, g3, w_in, conv_w, w_co, w_ao, w_o, bt, mt,
                  sinks, layer):
    m = x.shape[0]
    seq = m // batch
    rows = MIX_ROWS
    steps = seq // rows
    kern = functools.partial(_mixer_prompt_kernel, layer=layer, rows=rows)
    return pl.pallas_call(
        kern,
        grid=(batch, steps),
        in_specs=[
            pl.BlockSpec((rows, D_MODEL), lambda b, i: (b * steps + i, 0)),
            _resident((1, D_MODEL)),
            _resident((1, D_MODEL)),
            _resident((D_MODEL, C_END)),
            _resident((3, D_MODEL)),
            _resident((D_MODEL, D_MODEL)),
            _resident((D_MODEL, D_MODEL)),
            _resident((D_MODEL, D_MODEL)),
            _resident((GROUP * BLOCK, N_KV * 2 * BLOCK)),
            _resident((2, BLOCK, 2 * BLOCK)),
            pl.BlockSpec(memory_space=pltpu.SMEM),
        ],
        out_specs=[
            pl.BlockSpec((rows, D_MODEL), lambda b, i: (b * steps + i, 0)),
            pl.BlockSpec((None, 2, D_MODEL), lambda b, i: (b, 0, 0)),
            pl.BlockSpec((None, BLOCK, KV_W), lambda b, i: (b, 0, 0)),
            pl.BlockSpec((None, BLOCK, KV_W), lambda b, i: (b, 0, 0)),
        ],
        out_shape=[
            jax.ShapeDtypeStruct((m, D_MODEL), F32),
            jax.ShapeDtypeStruct((batch, 2, D_MODEL), F32),
            jax.ShapeDtypeStruct((batch, BLOCK, KV_W), F32),
            jax.ShapeDtypeStruct((batch, BLOCK, KV_W), F32),
        ],
        scratch_shapes=[
            pltpu.VMEM((rows, D_MODEL), BF16),
            pltpu.VMEM((rows, D_MODEL), BF16),
            pltpu.VMEM((rows + BLOCK, KV_W), BF16),
            pltpu.VMEM((rows + BLOCK, KV_W), BF16),
            pltpu.VMEM((rows + 8, D_MODEL), F32),
            pltpu.VMEM((rows, D_MODEL), F32),
            pltpu.VMEM((rows, D_MODEL), BF16),
            pltpu.VMEM((GROUP * BLOCK, N_KV * 2 * BLOCK), BF16),
        ],
        compiler_params=pltpu.CompilerParams(
            dimension_semantics=("arbitrary", "arbitrary"),
            vmem_limit_bytes=VMEM_LIMIT_BYTES),
        name="mixer_prompt",
    )(x, g2.reshape(1, D_MODEL), g3.reshape(1, D_MODEL), w_in, conv_w, w_co,
      w_ao, w_o, bt, mt, sinks)


def _mixer_decode_kernel(x_ref, st_ref, kc_ref, vc_ref, g2_ref, g3_ref,
                         win_ref, cw_ref, wco_ref, wao_ref, wo_ref, bdec_ref,
                         sink_ref,
                         y_ref, nst_ref, kout_ref, vout_ref,
                         h_s, q_s, knew_s, vnew_s, ag_s, o_s):
    i = pl.program_id(0)
    last = pl.num_programs(0) - 1

    @pl.when(i == 0)
    def _():
        x = x_ref[...]
        h = _rms(x, g2_ref[...]).astype(BF16)
        h_s[...] = h
        u = _dot(h, win_ref[:, C_CC:C_CX]) * _dot(h, win_ref[:, C_CX:C_Q])
        st0 = st_ref[:, 0:D_MODEL]
        st1 = st_ref[:, D_MODEL:2 * D_MODEL]
        yc = cw_ref[0:1, :] * st0 + cw_ref[1:2, :] * st1 + cw_ref[2:3, :] * u
        nst_ref[:, 0:D_MODEL] = st1
        nst_ref[:, D_MODEL:2 * D_MODEL] = u
        a_in = (_dot(h, win_ref[:, C_CB:C_CC]) * yc).astype(BF16)
        a_out = _dot(a_in, wco_ref[...])
        ag_s[...] = jax.nn.sigmoid(_dot(h, win_ref[:, C_GA:C_GB])) * a_out
        q_s[...] = _dot(h, win_ref[:, C_Q:C_K]) * (HEAD_DIM ** -0.5)
        knew_s[...] = _dot(h, win_ref[:, C_K:C_V])
        vnew_s[...] = _dot(h, win_ref[:, C_V:C_GA])

    row = lax.broadcasted_iota(jnp.int32, (N_HEADS, KV_W), 0)
    lane = lax.broadcasted_iota(jnp.int32, (N_HEADS, KV_W), 1)
    row_g = row // N_KV
    seg = (lane // HEAD_DIM) == (row % N_KV)
    bias = bdec_ref[...]
    sink = sink_ref[...]

    for b in range(DEC_SEQS):
        r = i * DEC_SEQS + b
        kout_ref[b, 0:WINDOW - 1, :] = kc_ref[b, 1:WINDOW, :]
        kout_ref[b, WINDOW - 1:WINDOW, :] = knew_s[pl.ds(r, 1), :]
        vout_ref[b, 0:WINDOW - 1, :] = vc_ref[b, 1:WINDOW, :]
        vout_ref[b, WINDOW - 1:WINDOW, :] = vnew_s[pl.ds(r, 1), :]
        kp = kout_ref[b].astype(BF16)
        vp = vout_ref[b].astype(BF16)
        q_row = q_s[pl.ds(r, 1), :]
        pieces = [jnp.broadcast_to(q_row[:, g * KV_W:(g + 1) * KV_W],
                                   (N_HEADS, KV_W)) for g in range(GROUP)]
        q_rows = jnp.where(row_g == 0, pieces[0],
                           jnp.where(row_g == 1, pieces[1],
                                     jnp.where(row_g == 2, pieces[2],
                                               pieces[3])))
        qm = jnp.where(seg, q_rows, 0.0).astype(BF16)
        s = _dot_t(qm, kp) + bias
        m = jnp.maximum(jnp.max(s, axis=-1, keepdims=True), sink)
        p = jnp.exp(s - m)
        den = jnp.sum(p, axis=-1, keepdims=True) + jnp.exp(sink - m)
        o = _dot(p.astype(BF16), vp) / den
        om = jnp.where(seg, o, 0.0)
        for g in range(GROUP):
            o_s[pl.ds(r, 1), g * KV_W:(g + 1) * KV_W] = jnp.sum(
                om[g * N_KV:(g + 1) * N_KV, :], axis=0, keepdims=True)

    @pl.when(i == last)
    def _():
        att = _dot(o_s[...].astype(BF16), wao_ref[...])
        gate_b = jax.nn.sigmoid(_dot(h_s[...], win_ref[:, C_GB:C_END]))
        merged = (ag_s[...] + gate_b * att).astype(BF16)
        y_ref[...] = x_ref[...] + _rms(_dot(merged, wo_ref[...]), g3_ref[...])


def _mixer_decode(x, st, kc, vc, g2, g3, w_in, conv_w, w_co, w_ao, w_o, bdec,
                  sink_col):
    n = x.shape[0]
    sb = DEC_SEQS
    cache_spec = pl.BlockSpec((sb, WINDOW, KV_W), lambda i: (i, 0, 0))
    return pl.pallas_call(
        _mixer_decode_kernel,
        grid=(n // sb,),
        in_specs=[
            _resident((n, D_MODEL)),
            _resident((n, 2 * D_MODEL)),
            cache_spec,
            cache_spec,
            _resident((1, D_MODEL)),
            _resident((1, D_MODEL)),
            _resident((D_MODEL, C_END)),
            _resident((3, D_MODEL)),
            _resident((D_MODEL, D_MODEL)),
            _resident((D_MODEL, D_MODEL)),
            _resident((D_MODEL, D_MODEL)),
            _resident((N_HEADS, WINDOW)),
            _resident((N_HEADS, 1)),
        ],
        out_specs=[
            pl.BlockSpec((n, D_MODEL), lambda i: (0, 0)),
            pl.BlockSpec((n, 2 * D_MODEL), lambda i: (0, 0)),
            cache_spec,
            cache_spec,
        ],
        out_shape=[
            jax.ShapeDtypeStruct((n, D_MODEL), F32),
            jax.ShapeDtypeStruct((n, 2 * D_MODEL), F32),
            jax.ShapeDtypeStruct((n, WINDOW, KV_W), F32),
            jax.ShapeDtypeStruct((n, WINDOW, KV_W), F32),
        ],
        scratch_shapes=[
            pltpu.VMEM((n, D_MODEL), BF16),
            pltpu.VMEM((n, D_MODEL), F32),
            pltpu.VMEM((n, KV_W), F32),
            pltpu.VMEM((n, KV_W), F32),
            pltpu.VMEM((n, D_MODEL), F32),
            pltpu.VMEM((n, D_MODEL), F32),
        ],
        compiler_params=pltpu.CompilerParams(
            dimension_semantics=("arbitrary",),
            vmem_limit_bytes=VMEM_LIMIT_BYTES),
        name="mixer_decode",
    )(x, st, kc, vc, g2.reshape(1, D_MODEL), g3.reshape(1, D_MODEL), w_in,
      conv_w, w_co, w_ao, w_o, bdec, sink_col)


def _heads_g_major(w, axis):
    shape = w.shape
    split = shape[:axis] + (N_KV, GROUP, HEAD_DIM) + shape[axis + 1:]
    perm = list(range(len(split)))
    perm[axis], perm[axis + 1] = perm[axis + 1], perm[axis]
    return w.reshape(split).transpose(perm).reshape(shape)


def kernel(x_prompt, x_sample, state_conv, cache_k_win, cache_v_win, rel_bias,
           norm_g, w_ff1_gu, w_ff1_down, w_in, conv_w, sinks, w_conv_out,
           w_attn_out, w_out, w_ff2_gu, w_ff2_down):
    batch, seq, _ = x_prompt.shape
    n_dec = x_sample.shape[0]

    bt = _bias_table(rel_bias)
    mt = jnp.asarray(_MASK_NP)
    bdec = bt.reshape(GROUP, BLOCK, N_KV, 2 * BLOCK)[:, BLOCK - 1, :, BLOCK:]
    bdec = bdec.reshape(N_HEADS, WINDOW)

    xp = x_prompt.reshape(batch * seq, D_MODEL)
    xs = x_sample.reshape(n_dec, D_MODEL)
    pc, pk, pv, sc, sk, sv = [], [], [], [], [], []
    for l in range(DEPTH):
        g = norm_g[l]
        w1gu = w_ff1_gu[l].astype(BF16)
        w1d = w_ff1_down[l].astype(BF16)
        w2gu = w_ff2_gu[l].astype(BF16)
        w2d = w_ff2_down[l].astype(BF16)
        win = jnp.concatenate(
            [w_in[l][:, :C_Q], _heads_g_major(w_in[l][:, C_Q:C_K], 1),
             w_in[l][:, C_K:]], axis=1).astype(BF16)
        wco = w_conv_out[l].astype(BF16)
        wao = _heads_g_major(w_attn_out[l], 0).astype(BF16)
        wo = w_out[l].astype(BF16)
        sink_col = sinks[l].reshape(N_KV, GROUP).T.reshape(N_HEADS, 1)

        xp = _ffn(xp, g[0], g[1], w1gu, w1d)
        xs = _ffn(xs, g[0], g[1], w1gu, w1d)

        xp, c1, k1, v1 = _mixer_prompt(xp, batch, g[2], g[3], win, conv_w[l],
                                       wco, wao, wo, bt, mt, sinks, l)
        xs, c2, k2, v2 = _mixer_decode(
            xs, state_conv[l].reshape(n_dec, 2 * D_MODEL),
            cache_k_win[l].reshape(n_dec, WINDOW, KV_W),
            cache_v_win[l].reshape(n_dec, WINDOW, KV_W),
            g[2], g[3], win, conv_w[l], wco, wao, wo, bdec, sink_col)

        xp = _ffn(xp, g[4], g[5], w2gu, w2d)
        xs = _ffn(xs, g[4], g[5], w2gu, w2d)

        pc.append(c1)
        pk.append(k1.reshape(batch, WINDOW, N_KV, HEAD_DIM))
        pv.append(v1.reshape(batch, WINDOW, N_KV, HEAD_DIM))
        sc.append(c2.reshape(n_dec, 2, D_MODEL))
        sk.append(k2.reshape(n_dec, WINDOW, N_KV, HEAD_DIM))
        sv.append(v2.reshape(n_dec, WINDOW, N_KV, HEAD_DIM))

    return (xp.reshape(batch, seq, D_MODEL), xs.reshape(n_dec, 1, D_MODEL),
            jnp.stack(pc), jnp.stack(pk), jnp.stack(pv),
            jnp.stack(sc), jnp.stack(sk), jnp.stack(sv))
```

```python
import functools

import numpy as np
import jax
import jax.numpy as jnp
from jax import lax
from jax.experimental import pallas as pl
from jax.experimental.pallas import tpu as pltpu

F32 = jnp.float32
BF16 = jnp.bfloat16

D_MODEL = 1024
D_FF = 2816
N_HEADS = 16
N_KV = 4
GROUP = N_HEADS // N_KV
HEAD_DIM = 64
KV_W = N_KV * HEAD_DIM
WINDOW = 128
BLOCK = 128
N_BUCKETS = 32
MAX_DISTANCE = 128
RMS_EPS = 1e-6
NEG = -1e30
DEPTH = 2
N_NORMS = 6

C_CB, C_CC, C_CX, C_Q, C_K, C_V, C_GA, C_GB, C_END = (
    0, 1024, 2048, 3072, 4096, 4352, 4608, 5632, 6656)

VMEM_LIMIT_BYTES = 56 * 1024 * 1024

FFN_ROWS = 256
FFN_CHUNK = 256
MIX_ROWS = 256
DEC_SEQS = 8


def _t5_bucket_np(rel):
    n = np.maximum(rel, 0)
    max_exact = N_BUCKETS // 2
    nf = np.maximum(n, 1).astype(np.float32)
    large = max_exact + (
        np.log(nf / max_exact) / np.log(MAX_DISTANCE / max_exact)
        * (N_BUCKETS - max_exact)).astype(np.int32)
    large = np.minimum(large, N_BUCKETS - 1)
    return np.where(n < max_exact, n, large).astype(np.int32)


def _band_tables():
    qi = np.arange(BLOCK)[:, None]
    kj = np.arange(2 * BLOCK)[None, :]
    rel = qi + BLOCK - kj
    bucket = _t5_bucket_np(rel)
    band = (rel >= 0) & (rel < WINDOW)
    mask = np.stack([band, band & (kj >= BLOCK)]).astype(np.float32)
    return bucket, mask


_BUCKET_NP, _MASK_NP = _band_tables()


def _rms(x, g):
    r = lax.rsqrt(jnp.mean(x * x, axis=-1, keepdims=True) + RMS_EPS)
    return (x * r) * g


def _dot(a, b):
    return jnp.dot(a, b, preferred_element_type=F32)


def _dot_t(a, b):
    return lax.dot_general(a, b, (((1,), (1,)), ((), ())),
                           preferred_element_type=F32)


def _resident(shape):
    return pl.BlockSpec(shape, lambda *_: (0,) * len(shape),
                        pipeline_mode=pl.Buffered(1))


def _layer_resident(shape, layer):
    return pl.BlockSpec((None,) + tuple(shape),
                        lambda *_: (layer,) + (0,) * len(shape),
                        pipeline_mode=pl.Buffered(1))


def _bias_kernel(rb_ref, bucket_ref, mask_ref, out_ref):
    g = pl.program_id(1)
    hk = pl.program_id(2)
    h = hk * GROUP + g
    bucket = bucket_ref[...]

    def body(b, acc):
        return jnp.where(bucket == b, rb_ref[b, h], acc)

    bias = lax.fori_loop(0, N_BUCKETS, body,
                         jnp.zeros((BLOCK, 2 * BLOCK), F32))
    out_ref[...] = jnp.where(mask_ref[...] > 0.0, bias, NEG)


def _bias_table(rel_bias):
    return pl.pallas_call(
        _bias_kernel,
        grid=(2, GROUP, N_KV),
        in_specs=[
            pl.BlockSpec(memory_space=pltpu.SMEM),
            pl.BlockSpec((BLOCK, 2 * BLOCK), lambda v, g, hk: (0, 0)),
            pl.BlockSpec((None, BLOCK, 2 * BLOCK), lambda v, g, hk: (v, 0, 0)),
        ],
        out_specs=pl.BlockSpec((None, BLOCK, 2 * BLOCK),
                               lambda v, g, hk: (v, g, hk)),
        out_shape=jax.ShapeDtypeStruct(
            (2, GROUP * BLOCK, N_KV * 2 * BLOCK), F32),
        name="bias_table",
    )(rel_bias, jnp.asarray(_BUCKET_NP), jnp.asarray(_MASK_NP))


def _ffn_kernel(x_ref, g_ref, wgu_ref, wd_ref, o_ref, act_ref, *, gi):
    x = x_ref[...]
    h = _rms(x, g_ref[gi:gi + 1, :]).astype(BF16)
    for c in range(D_FF // FFN_CHUNK):
        lo = c * FFN_CHUNK
        gate = _dot(h, wgu_ref[:, lo:lo + FFN_CHUNK])
        up = _dot(h, wgu_ref[:, D_FF + lo:D_FF + lo + FFN_CHUNK])
        act_ref[:, lo:lo + FFN_CHUNK] = (
            gate * jax.nn.sigmoid(gate) * up).astype(BF16)
    y = _dot(act_ref[...], wd_ref[...])
    o_ref[...] = x + 0.5 * _rms(y, g_ref[gi + 1:gi + 2, :])


def _ffn(x, norm_g, w_gu, w_down, layer, gi):
    m = x.shape[0]
    tm = min(FFN_ROWS, m)
    return pl.pallas_call(
        functools.partial(_ffn_kernel, gi=gi),
        grid=(m // tm,),
        in_specs=[
            pl.BlockSpec((tm, D_MODEL), lambda i: (i, 0)),
            _layer_resident((N_NORMS, D_MODEL), layer),
            _layer_resident((D_MODEL, 2 * D_FF), layer),
            _layer_resident((D_FF, D_MODEL), layer),
        ],
        out_specs=pl.BlockSpec((tm, D_MODEL), lambda i: (i, 0)),
        out_shape=jax.ShapeDtypeStruct((m, D_MODEL), F32),
        scratch_shapes=[pltpu.VMEM((tm, D_FF), BF16)],
        compiler_params=pltpu.CompilerParams(
            dimension_semantics=("arbitrary",),
            vmem_limit_bytes=VMEM_LIMIT_BYTES),
        name="ffn_half",
    )(x, norm_g, w_gu, w_down)


def _lane_segment_mask(rows):
    lane = lax.broadcasted_iota(jnp.int32, (rows, KV_W), 1)
    return [(lane >= hk * HEAD_DIM) & (lane < (hk + 1) * HEAD_DIM)
            for hk in range(N_KV)]


def _mixer_prompt_kernel(x_ref, g_ref, win_ref, wq_ref, cw_ref, wco_ref,
                         wao_ref, wo_ref, bt_ref, sinks_ref,
                         y_ref, cst_ref, kout_ref, vout_ref,
                         q_s, kbuf, vbuf, upad, o_s, p_s,
                         *, layer, rows):
    i = pl.program_id(1)
    last = pl.num_programs(1) - 1
    nq = rows // BLOCK

    @pl.when(i == 0)
    def _():
        upad[0:8, :] = jnp.zeros((8, D_MODEL), F32)
        kbuf[0:BLOCK, :] = jnp.zeros((BLOCK, KV_W), BF16)
        vbuf[0:BLOCK, :] = jnp.zeros((BLOCK, KV_W), BF16)

    @pl.when(i > 0)
    def _():
        upad[0:8, :] = upad[rows:rows + 8, :]
        kbuf[0:BLOCK, :] = kbuf[rows:rows + BLOCK, :]
        vbuf[0:BLOCK, :] = vbuf[rows:rows + BLOCK, :]

    x = x_ref[...]
    h = _rms(x, g_ref[2:3, :]).astype(BF16)

    q_s[...] = (_dot(h, wq_ref[...]) * (HEAD_DIM ** -0.5)).astype(BF16)
    k = _dot(h, win_ref[:, C_K:C_V])
    v = _dot(h, win_ref[:, C_V:C_GA])
    kbuf[BLOCK:BLOCK + rows, :] = k.astype(BF16)
    vbuf[BLOCK:BLOCK + rows, :] = v.astype(BF16)

    @pl.when(i == last)
    def _():
        kout_ref[...] = k[rows - BLOCK:rows, :]
        vout_ref[...] = v[rows - BLOCK:rows, :]

    seg = _lane_segment_mask(2 * BLOCK)
    seg_q = _lane_segment_mask(BLOCK)
    zero_kv = jnp.zeros((2 * BLOCK, KV_W), BF16)
    first = (i == 0).astype(jnp.int32)

    for j in range(nq):
        r0 = j * BLOCK
        variant = first if j == 0 else 0
        qj = q_s[r0:r0 + BLOCK, :]
        kw = kbuf[r0:r0 + 2 * BLOCK, :]
        vw = vbuf[r0:r0 + 2 * BLOCK, :]
        qs = jnp.concatenate(
            [qj[:, g * KV_W:(g + 1) * KV_W] for g in range(GROUP)], axis=0)
        kb = jnp.concatenate(
            [jnp.where(seg[hk], kw, zero_kv) for hk in range(N_KV)], axis=0)
        vb = jnp.concatenate(
            [jnp.where(seg[hk], vw, zero_kv) for hk in range(N_KV)], axis=0)
        s_all = _dot_t(qs, kb)
        inv = []
        for g in range(GROUP):
            inv_g = []
            for hk in range(N_KV):
                rs = slice(g * BLOCK, (g + 1) * BLOCK)
                cs = slice(hk * 2 * BLOCK, (hk + 1) * 2 * BLOCK)
                s = s_all[rs, cs] + bt_ref[variant, rs, cs]
                sink = sinks_ref[layer, hk * GROUP + g]
                m = jnp.maximum(jnp.max(s, axis=-1, keepdims=True), sink)
                p = jnp.exp(s - m)
                den = jnp.sum(p, axis=-1, keepdims=True) + jnp.exp(sink - m)
                p_s[j, rs, cs] = p.astype(BF16)
                inv_g.append(1.0 / den)
            inv.append(inv_g)
        o_all = _dot(p_s[j], vb)
        for g in range(GROUP):
            scale = jnp.where(
                seg_q[0], inv[g][0],
                jnp.where(seg_q[1], inv[g][1],
                          jnp.where(seg_q[2], inv[g][2], inv[g][3])))
            o_s[r0:r0 + BLOCK, g * KV_W:(g + 1) * KV_W] = (
                o_all[g * BLOCK:(g + 1) * BLOCK, :] * scale).astype(BF16)

    u = _dot(h, win_ref[:, C_CC:C_CX]) * _dot(h, win_ref[:, C_CX:C_Q])
    upad[8:rows + 8, :] = u

    @pl.when(i == last)
    def _():
        cst_ref[...] = upad[rows + 6:rows + 8, :]

    yc = (cw_ref[0:1, :] * upad[6:rows + 6, :]
          + cw_ref[1:2, :] * upad[7:rows + 7, :]
          + cw_ref[2:3, :] * u)
    a_in = (_dot(h, win_ref[:, C_CB:C_CC]) * yc).astype(BF16)
    a_out = _dot(a_in, wco_ref[...])
    gated_a = jax.nn.sigmoid(_dot(h, win_ref[:, C_GA:C_GB])) * a_out

    att = _dot(o_s[...], wao_ref[...])
    gate_b = jax.nn.sigmoid(_dot(h, win_ref[:, C_GB:C_END]))
    merged = (gated_a + gate_b * att).astype(BF16)
    y_ref[...] = x + _rms(_dot(merged, wo_ref[...]), g_ref[3:4, :])


def _mixer_prompt(x, batch, norm_g, w_in, w_q, conv_w, w_co, w_ao, w_o, bt,
                  sinks, layer):
    m = x.shape[0]
    seq = m // batch
    rows = MIX_ROWS
    steps = seq // rows
    kern = functools.partial(_mixer_prompt_kernel, layer=layer, rows=rows)
    return pl.pallas_call(
        kern,
        grid=(batch, steps),
        in_specs=[
            pl.BlockSpec((rows, D_MODEL), lambda b, i: (b * steps + i, 0)),
            _layer_resident((N_NORMS, D_MODEL), layer),
            _layer_resident((D_MODEL, C_END), layer),
            _layer_resident((D_MODEL, D_MODEL), layer),
            _layer_resident((3, D_MODEL), layer),
            _layer_resident((D_MODEL, D_MODEL), layer),
            _layer_resident((D_MODEL, D_MODEL), layer),
            _layer_resident((D_MODEL, D_MODEL), layer),
            _resident((2, GROUP * BLOCK, N_KV * 2 * BLOCK)),
            pl.BlockSpec(memory_space=pltpu.SMEM),
        ],
        out_specs=[
            pl.BlockSpec((rows, D_MODEL), lambda b, i: (b * steps + i, 0)),
            pl.BlockSpec((None, 2, D_MODEL), lambda b, i: (b, 0, 0)),
            pl.BlockSpec((None, BLOCK, KV_W), lambda b, i: (b, 0, 0)),
            pl.BlockSpec((None, BLOCK, KV_W), lambda b, i: (b, 0, 0)),
        ],
        out_shape=[
            jax.ShapeDtypeStruct((m, D_MODEL), F32),
            jax.ShapeDtypeStruct((batch, 2, D_MODEL), F32),
            jax.ShapeDtypeStruct((batch, BLOCK, KV_W), F32),
            jax.ShapeDtypeStruct((batch, BLOCK, KV_W), F32),
        ],
        scratch_shapes=[
            pltpu.VMEM((rows, D_MODEL), BF16),
            pltpu.VMEM((rows + BLOCK, KV_W), BF16),
            pltpu.VMEM((rows + BLOCK, KV_W), BF16),
            pltpu.VMEM((rows + 8, D_MODEL), F32),
            pltpu.VMEM((rows, D_MODEL), BF16),
            pltpu.VMEM((rows // BLOCK, GROUP * BLOCK, N_KV * 2 * BLOCK),
                       BF16),
        ],
        compiler_params=pltpu.CompilerParams(
            dimension_semantics=("arbitrary", "arbitrary"),
            vmem_limit_bytes=VMEM_LIMIT_BYTES),
        name="mixer_prompt",
    )(x, norm_g, w_in, w_q, conv_w, w_co, w_ao, w_o, bt, sinks)


def _mixer_decode_kernel(x_ref, st_ref, kc_ref, vc_ref, g_ref,
                         win_ref, wq_ref, cw_ref, wco_ref, wao_ref, wo_ref,
                         bdec_ref, sink_ref,
                         y_ref, nst_ref, kout_ref, vout_ref,
                         h_s, q_s, knew_s, vnew_s, ag_s, o_s):
    i = pl.program_id(0)
    last = pl.num_programs(0) - 1

    @pl.when(i == 0)
    def _():
        x = x_ref[...]
        h = _rms(x, g_ref[2:3, :]).astype(BF16)
        h_s[...] = h
        u = _dot(h, win_ref[:, C_CC:C_CX]) * _dot(h, win_ref[:, C_CX:C_Q])
        st0 = st_ref[:, 0:D_MODEL]
        st1 = st_ref[:, D_MODEL:2 * D_MODEL]
        yc = cw_ref[0:1, :] * st0 + cw_ref[1:2, :] * st1 + cw_ref[2:3, :] * u
        nst_ref[:, 0:D_MODEL] = st1
        nst_ref[:, D_MODEL:2 * D_MODEL] = u
        a_in = (_dot(h, win_ref[:, C_CB:C_CC]) * yc).astype(BF16)
        a_out = _dot(a_in, wco_ref[...])
        ag_s[...] = jax.nn.sigmoid(_dot(h, win_ref[:, C_GA:C_GB])) * a_out
        q_s[...] = _dot(h, wq_ref[...]) * (HEAD_DIM ** -0.5)
        knew_s[...] = _dot(h, win_ref[:, C_K:C_V])
        vnew_s[...] = _dot(h, win_ref[:, C_V:C_GA])

    row = lax.broadcasted_iota(jnp.int32, (N_HEADS, KV_W), 0)
    lane = lax.broadcasted_iota(jnp.int32, (N_HEADS, KV_W), 1)
    row_g = row // N_KV
    seg = (lane // HEAD_DIM) == (row % N_KV)
    bias = bdec_ref[...]
    sink = sink_ref[...]

    for b in range(DEC_SEQS):
        r = i * DEC_SEQS + b
        kout_ref[b, 0:WINDOW - 1, :] = kc_ref[b, 1:WINDOW, :]
        kout_ref[b, WINDOW - 1:WINDOW, :] = knew_s[pl.ds(r, 1), :]
        vout_ref[b, 0:WINDOW - 1, :] = vc_ref[b, 1:WINDOW, :]
        vout_ref[b, WINDOW - 1:WINDOW, :] = vnew_s[pl.ds(r, 1), :]
        kp = kout_ref[b].astype(BF16)
        vp = vout_ref[b].astype(BF16)
        q_row = q_s[pl.ds(r, 1), :]
        pieces = [jnp.broadcast_to(q_row[:, g * KV_W:(g + 1) * KV_W],
                                   (N_HEADS, KV_W)) for g in range(GROUP)]
        q_rows = jnp.where(row_g == 0, pieces[0],
                           jnp.where(row_g == 1, pieces[1],
                                     jnp.where(row_g == 2, pieces[2],
                                               pieces[3])))
        qm = jnp.where(seg, q_rows, 0.0).astype(BF16)
        s = _dot_t(qm, kp) + bias
        m = jnp.maximum(jnp.max(s, axis=-1, keepdims=True), sink)
        p = jnp.exp(s - m)
        den = jnp.sum(p, axis=-1, keepdims=True) + jnp.exp(sink - m)
        o = _dot(p.astype(BF16), vp) / den
        om = jnp.where(seg, o, 0.0)
        for g in range(GROUP):
            o_s[pl.ds(r, 1), g * KV_W:(g + 1) * KV_W] = jnp.sum(
                om[g * N_KV:(g + 1) * N_KV, :], axis=0, keepdims=True)

    @pl.when(i == last)
    def _():
        att = _dot(o_s[...].astype(BF16), wao_ref[...])
        gate_b = jax.nn.sigmoid(_dot(h_s[...], win_ref[:, C_GB:C_END]))
        merged = (ag_s[...] + gate_b * att).astype(BF16)
        y_ref[...] = x_ref[...] + _rms(_dot(merged, wo_ref[...]),
                                       g_ref[3:4, :])


def _mixer_decode(x, st, kc, vc, norm_g, w_in, w_q, conv_w, w_co, w_ao, w_o,
                  bdec, sink_col, layer):
    n = x.shape[0]
    sb = DEC_SEQS
    cache_spec = pl.BlockSpec((sb, WINDOW, KV_W), lambda i: (i, 0, 0))
    return pl.pallas_call(
        _mixer_decode_kernel,
        grid=(n // sb,),
        in_specs=[
            _resident((n, D_MODEL)),
            _resident((n, 2 * D_MODEL)),
            cache_spec,
            cache_spec,
            _layer_resident((N_NORMS, D_MODEL), layer),
            _layer_resident((D_MODEL, C_END), layer),
            _layer_resident((D_MODEL, D_MODEL), layer),
            _layer_resident((3, D_MODEL), layer),
            _layer_resident((D_MODEL, D_MODEL), layer),
            _layer_resident((D_MODEL, D_MODEL), layer),
            _layer_resident((D_MODEL, D_MODEL), layer),
            _resident((N_HEADS, WINDOW)),
            _resident((N_HEADS, 1)),
        ],
        out_specs=[
            pl.BlockSpec((n, D_MODEL), lambda i: (0, 0)),
            pl.BlockSpec((n, 2 * D_MODEL), lambda i: (0, 0)),
            cache_spec,
            cache_spec,
        ],
        out_shape=[
            jax.ShapeDtypeStruct((n, D_MODEL), F32),
            jax.ShapeDtypeStruct((n, 2 * D_MODEL), F32),
            jax.ShapeDtypeStruct((n, WINDOW, KV_W), F32),
            jax.ShapeDtypeStruct((n, WINDOW, KV_W), F32),
        ],
        scratch_shapes=[
            pltpu.VMEM((n, D_MODEL), BF16),
            pltpu.VMEM((n, D_MODEL), F32),
            pltpu.VMEM((n, KV_W), F32),
            pltpu.VMEM((n, KV_W), F32),
            pltpu.VMEM((n, D_MODEL), F32),
            pltpu.VMEM((n, D_MODEL), F32),
        ],
        compiler_params=pltpu.CompilerParams(
            dimension_semantics=("arbitrary",),
            vmem_limit_bytes=VMEM_LIMIT_BYTES),
        name="mixer_decode",
    )(x, st, kc, vc, norm_g, w_in, w_q, conv_w, w_co, w_ao, w_o, bdec,
      sink_col)


def _heads_g_major(w, axis):
    shape = w.shape
    split = shape[:axis] + (N_KV, GROUP, HEAD_DIM) + shape[axis + 1:]
    perm = list(range(len(split)))
    perm[axis], perm[axis + 1] = perm[axis + 1], perm[axis]
    return w.reshape(split).transpose(perm).reshape(shape)


def kernel(x_prompt, x_sample, state_conv, cache_k_win, cache_v_win, rel_bias,
           norm_g, w_ff1_gu, w_ff1_down, w_in, conv_w, sinks, w_conv_out,
           w_attn_out, w_out, w_ff2_gu, w_ff2_down):
    batch, seq, _ = x_prompt.shape
    n_dec = x_sample.shape[0]

    bt = _bias_table(rel_bias)
    bdec = bt[0].reshape(GROUP, BLOCK, N_KV, 2 * BLOCK)[:, BLOCK - 1, :, BLOCK:]
    bdec = bdec.reshape(N_HEADS, WINDOW)

    w1gu = w_ff1_gu.astype(BF16)
    w1d = w_ff1_down.astype(BF16)
    w2gu = w_ff2_gu.astype(BF16)
    w2d = w_ff2_down.astype(BF16)
    win = w_in.astype(BF16)
    wq = _heads_g_major(w_in[:, :, C_Q:C_K], 2).astype(BF16)
    wco = w_conv_out.astype(BF16)
    wao = _heads_g_major(w_attn_out, 1).astype(BF16)
    wo = w_out.astype(BF16)

    xp = x_prompt.reshape(batch * seq, D_MODEL)
    xs = x_sample.reshape(n_dec, D_MODEL)
    pc, pk, pv, sc, sk, sv = [], [], [], [], [], []
    for l in range(DEPTH):
        sink_col = sinks[l].reshape(N_KV, GROUP).T.reshape(N_HEADS, 1)

        xp = _ffn(xp, norm_g, w1gu, w1d, l, 0)
        xs = _ffn(xs, norm_g, w1gu, w1d, l, 0)

        xp, c1, k1, v1 = _mixer_prompt(xp, batch, norm_g, win, wq, conv_w,
                                       wco, wao, wo, bt, sinks, l)
        xs, c2, k2, v2 = _mixer_decode(
            xs, state_conv[l].reshape(n_dec, 2 * D_MODEL),
            cache_k_win[l].reshape(n_dec, WINDOW, KV_W),
            cache_v_win[l].reshape(n_dec, WINDOW, KV_W),
            norm_g, win, wq, conv_w, wco, wao, wo, bdec, sink_col, l)

        xp = _ffn(xp, norm_g, w2gu, w2d, l, 4)
        xs = _ffn(xs, norm_g, w2gu, w2d, l, 4)

        pc.append(c1)
        pk.append(k1.reshape(batch, WINDOW, N_KV, HEAD_DIM))
        pv.append(v1.reshape(batch, WINDOW, N_KV, HEAD_DIM))
        sc.append(c2.reshape(n_dec, 2, D_MODEL))
        sk.append(k2.reshape(n_dec, WINDOW, N_KV, HEAD_DIM))
        sv.append(v2.reshape(n_dec, WINDOW, N_KV, HEAD_DIM))

    return (xp.reshape(batch, seq, D_MODEL), xs.reshape(n_dec, 1, D_MODEL),
            jnp.stack(pc), jnp.stack(pk), jnp.stack(pv),
            jnp.stack(sc), jnp.stack(sk), jnp.stack(sv))
```

```python
import functools

import numpy as np
import jax
import jax.numpy as jnp
from jax import lax
from jax.experimental import pallas as pl
from jax.experimental.pallas import tpu as pltpu

F32 = jnp.float32
BF16 = jnp.bfloat16

D_MODEL = 1024
D_FF = 2816
N_HEADS = 16
N_KV = 4
GROUP = N_HEADS // N_KV
HEAD_DIM = 64
KV_W = N_KV * HEAD_DIM
WINDOW = 128
BLOCK = 128
N_BUCKETS = 32
MAX_DISTANCE = 128
RMS_EPS = 1e-6
NEG = -1e30
DEPTH = 2
N_NORMS = 6

C_CB, C_CC, C_CX, C_Q, C_K, C_V, C_GA, C_GB, C_END = (
    0, 1024, 2048, 3072, 4096, 4352, 4608, 5632, 6656)

VMEM_LIMIT_BYTES = 56 * 1024 * 1024

FFN_ROWS = 256
FFN_CHUNK = 256
MIX_ROWS = 256
DEC_SEQS = 8


def _t5_bucket_np(rel):
    n = np.maximum(rel, 0)
    max_exact = N_BUCKETS // 2
    nf = np.maximum(n, 1).astype(np.float32)
    large = max_exact + (
        np.log(nf / max_exact) / np.log(MAX_DISTANCE / max_exact)
        * (N_BUCKETS - max_exact)).astype(np.int32)
    large = np.minimum(large, N_BUCKETS - 1)
    return np.where(n < max_exact, n, large).astype(np.int32)


def _band_tables():
    qi = np.arange(BLOCK)[:, None]
    kj = np.arange(2 * BLOCK)[None, :]
    rel = qi + BLOCK - kj
    bucket = _t5_bucket_np(rel)
    band = (rel >= 0) & (rel < WINDOW)
    mask = np.stack([band, band & (kj >= BLOCK)]).astype(np.float32)
    return bucket, mask


_BUCKET_NP, _MASK_NP = _band_tables()


def _rms(x, g):
    r = lax.rsqrt(jnp.mean(x * x, axis=-1, keepdims=True) + RMS_EPS)
    return (x * r) * g


def _dot(a, b):
    return jnp.dot(a, b, preferred_element_type=F32)


def _dot_t(a, b):
    return lax.dot_general(a, b, (((1,), (1,)), ((), ())),
                           preferred_element_type=F32)


def _resident(shape):
    return pl.BlockSpec(shape, lambda *_: (0,) * len(shape),
                        pipeline_mode=pl.Buffered(1))


def _layer_resident(shape, layer):
    return pl.BlockSpec((None,) + tuple(shape),
                        lambda *_: (layer,) + (0,) * len(shape),
                        pipeline_mode=pl.Buffered(1))


def _bias_kernel(rb_ref, bucket_ref, mask_ref, out_ref):
    g = pl.program_id(1)
    hk = pl.program_id(2)
    h = hk * GROUP + g
    bucket = bucket_ref[...]

    def body(b, acc):
        return jnp.where(bucket == b, rb_ref[b, h], acc)

    bias = lax.fori_loop(0, N_BUCKETS, body,
                         jnp.zeros((BLOCK, 2 * BLOCK), F32))
    out_ref[...] = jnp.where(mask_ref[...] > 0.0, bias, NEG)


def _bias_table(rel_bias):
    return pl.pallas_call(
        _bias_kernel,
        grid=(2, GROUP, N_KV),
        in_specs=[
            pl.BlockSpec(memory_space=pltpu.SMEM),
            pl.BlockSpec((BLOCK, 2 * BLOCK), lambda v, g, hk: (0, 0)),
            pl.BlockSpec((None, BLOCK, 2 * BLOCK), lambda v, g, hk: (v, 0, 0)),
        ],
        out_specs=pl.BlockSpec((None, BLOCK, 2 * BLOCK),
                               lambda v, g, hk: (v, g, hk)),
        out_shape=jax.ShapeDtypeStruct(
            (2, GROUP * BLOCK, N_KV * 2 * BLOCK), F32),
        name="bias_table",
    )(rel_bias, jnp.asarray(_BUCKET_NP), jnp.asarray(_MASK_NP))


def _ffn_kernel(x_ref, g_ref, wgu_ref, wd_ref, o_ref, act_ref, *, gi):
    x = x_ref[...]
    h = _rms(x, g_ref[gi:gi + 1, :]).astype(BF16)
    for c in range(D_FF // FFN_CHUNK):
        lo = c * FFN_CHUNK
        gate = _dot(h, wgu_ref[:, lo:lo + FFN_CHUNK])
        up = _dot(h, wgu_ref[:, D_FF + lo:D_FF + lo + FFN_CHUNK])
        act_ref[:, lo:lo + FFN_CHUNK] = (
            gate * jax.nn.sigmoid(gate) * up).astype(BF16)
    y = _dot(act_ref[...], wd_ref[...])
    o_ref[...] = x + 0.5 * _rms(y, g_ref[gi + 1:gi + 2, :])


def _ffn(x, norm_g, w_gu, w_down, layer, gi):
    m = x.shape[0]
    tm = min(FFN_ROWS, m)
    return pl.pallas_call(
        functools.partial(_ffn_kernel, gi=gi),
        grid=(m // tm,),
        in_specs=[
            pl.BlockSpec((tm, D_MODEL), lambda i: (i, 0)),
            _layer_resident((N_NORMS, D_MODEL), layer),
            _layer_resident((D_MODEL, 2 * D_FF), layer),
            _layer_resident((D_FF, D_MODEL), layer),
        ],
        out_specs=pl.BlockSpec((tm, D_MODEL), lambda i: (i, 0)),
        out_shape=jax.ShapeDtypeStruct((m, D_MODEL), F32),
        scratch_shapes=[pltpu.VMEM((tm, D_FF), BF16)],
        compiler_params=pltpu.CompilerParams(
            dimension_semantics=("arbitrary",),
            vmem_limit_bytes=VMEM_LIMIT_BYTES),
        name="ffn_half",
    )(x, norm_g, w_gu, w_down)


def _lane_segment_mask(rows):
    lane = lax.broadcasted_iota(jnp.int32, (rows, KV_W), 1)
    return [(lane >= hk * HEAD_DIM) & (lane < (hk + 1) * HEAD_DIM)
            for hk in range(N_KV)]


def _mixer_prompt_kernel(x_ref, g_ref, win_ref, wq_ref, cw_ref, wco_ref,
                         wao_ref, wo_ref, bt_ref, sinks_ref,
                         y_ref, cst_ref, kout_ref, vout_ref,
                         q_s, kbuf, vbuf, upad, o_s, p_s,
                         *, layer, rows):
    i = pl.program_id(1)
    last = pl.num_programs(1) - 1
    nq = rows // BLOCK

    @pl.when(i == 0)
    def _():
        upad[0:8, :] = jnp.zeros((8, D_MODEL), F32)
        kbuf[0:BLOCK, :] = jnp.zeros((BLOCK, KV_W), BF16)
        vbuf[0:BLOCK, :] = jnp.zeros((BLOCK, KV_W), BF16)

    @pl.when(i > 0)
    def _():
        upad[0:8, :] = upad[rows:rows + 8, :]
        kbuf[0:BLOCK, :] = kbuf[rows:rows + BLOCK, :]
        vbuf[0:BLOCK, :] = vbuf[rows:rows + BLOCK, :]

    x = x_ref[...]
    h = _rms(x, g_ref[2:3, :]).astype(BF16)

    q_s[...] = (_dot(h, wq_ref[...]) * (HEAD_DIM ** -0.5)).astype(BF16)
    k = _dot(h, win_ref[:, C_K:C_V])
    v = _dot(h, win_ref[:, C_V:C_GA])
    kbuf[BLOCK:BLOCK + rows, :] = k.astype(BF16)
    vbuf[BLOCK:BLOCK + rows, :] = v.astype(BF16)

    @pl.when(i == last)
    def _():
        kout_ref[...] = k[rows - BLOCK:rows, :]
        vout_ref[...] = v[rows - BLOCK:rows, :]

    seg = _lane_segment_mask(2 * BLOCK)
    seg_q = _lane_segment_mask(BLOCK)
    zero_kv = jnp.zeros((2 * BLOCK, KV_W), BF16)
    first = (i == 0).astype(jnp.int32)

    for j in range(nq):
        r0 = j * BLOCK
        variant = first if j == 0 else 0
        qj = q_s[r0:r0 + BLOCK, :]
        kw = kbuf[r0:r0 + 2 * BLOCK, :]
        vw = vbuf[r0:r0 + 2 * BLOCK, :]
        qs = jnp.concatenate(
            [qj[:, g * KV_W:(g + 1) * KV_W] for g in range(GROUP)], axis=0)
        kb = jnp.concatenate(
            [jnp.where(seg[hk], kw, zero_kv) for hk in range(N_KV)], axis=0)
        vb = jnp.concatenate(
            [jnp.where(seg[hk], vw, zero_kv) for hk in range(N_KV)], axis=0)
        s_all = _dot_t(qs, kb)
        inv = []
        for g in range(GROUP):
            inv_g = []
            for hk in range(N_KV):
                rs = slice(g * BLOCK, (g + 1) * BLOCK)
                cs = slice(hk * 2 * BLOCK, (hk + 1) * 2 * BLOCK)
                s = s_all[rs, cs] + bt_ref[variant, rs, cs]
                sink = sinks_ref[layer, hk * GROUP + g]
                m = jnp.maximum(jnp.max(s, axis=-1, keepdims=True), sink)
                p = jnp.exp(s - m)
                den = jnp.sum(p, axis=-1, keepdims=True) + jnp.exp(sink - m)
                p_s[j, rs, cs] = p.astype(BF16)
                inv_g.append(1.0 / den)
            inv.append(inv_g)
        o_all = _dot(p_s[j], vb)
        for g in range(GROUP):
            scale = jnp.where(
                seg_q[0], inv[g][0],
                jnp.where(seg_q[1], inv[g][1],
                          jnp.where(seg_q[2], inv[g][2], inv[g][3])))
            o_s[r0:r0 + BLOCK, g * KV_W:(g + 1) * KV_W] = (
                o_all[g * BLOCK:(g + 1) * BLOCK, :] * scale).astype(BF16)

    u = _dot(h, win_ref[:, C_CC:C_CX]) * _dot(h, win_ref[:, C_CX:C_Q])
    upad[8:rows + 8, :] = u

    @pl.when(i == last)
    def _():
        cst_ref[...] = upad[rows + 6:rows + 8, :]

    yc = (cw_ref[0:1, :] * upad[6:rows + 6, :]
          + cw_ref[1:2, :] * upad[7:rows + 7, :]
          + cw_ref[2:3, :] * u)
    a_in = (_dot(h, win_ref[:, C_CB:C_CC]) * yc).astype(BF16)
    a_out = _dot(a_in, wco_ref[...])
    gated_a = jax.nn.sigmoid(_dot(h, win_ref[:, C_GA:C_GB])) * a_out

    att = _dot(o_s[...], wao_ref[...])
    gate_b = jax.nn.sigmoid(_dot(h, win_ref[:, C_GB:C_END]))
    merged = (gated_a + gate_b * att).astype(BF16)
    y_ref[...] = x + _rms(_dot(merged, wo_ref[...]), g_ref[3:4, :])


def _mixer_prompt(x, batch, norm_g, w_in, w_q, conv_w, w_co, w_ao, w_o, bt,
                  sinks, layer):
    m = x.shape[0]
    seq = m // batch
    rows = MIX_ROWS
    steps = seq // rows
    kern = functools.partial(_mixer_prompt_kernel, layer=layer, rows=rows)
    return pl.pallas_call(
        kern,
        grid=(batch, steps),
        in_specs=[
            pl.BlockSpec((rows, D_MODEL), lambda b, i: (b * steps + i, 0)),
            _layer_resident((N_NORMS, D_MODEL), layer),
            _layer_resident((D_MODEL, C_END), layer),
            _layer_resident((D_MODEL, D_MODEL), layer),
            _layer_resident((3, D_MODEL), layer),
            _layer_resident((D_MODEL, D_MODEL), layer),
            _layer_resident((D_MODEL, D_MODEL), layer),
            _layer_resident((D_MODEL, D_MODEL), layer),
            _resident((2, GROUP * BLOCK, N_KV * 2 * BLOCK)),
            pl.BlockSpec(memory_space=pltpu.SMEM),
        ],
        out_specs=[
            pl.BlockSpec((rows, D_MODEL), lambda b, i: (b * steps + i, 0)),
            pl.BlockSpec((None, 2, D_MODEL), lambda b, i: (b, 0, 0)),
            pl.BlockSpec((None, BLOCK, KV_W), lambda b, i: (b, 0, 0)),
            pl.BlockSpec((None, BLOCK, KV_W), lambda b, i: (b, 0, 0)),
        ],
        out_shape=[
            jax.ShapeDtypeStruct((m, D_MODEL), F32),
            jax.ShapeDtypeStruct((batch, 2, D_MODEL), F32),
            jax.ShapeDtypeStruct((batch, BLOCK, KV_W), F32),
            jax.ShapeDtypeStruct((batch, BLOCK, KV_W), F32),
        ],
        scratch_shapes=[
            pltpu.VMEM((rows, D_MODEL), BF16),
            pltpu.VMEM((rows + BLOCK, KV_W), BF16),
            pltpu.VMEM((rows + BLOCK, KV_W), BF16),
            pltpu.VMEM((rows + 8, D_MODEL), F32),
            pltpu.VMEM((rows, D_MODEL), BF16),
            pltpu.VMEM((rows // BLOCK, GROUP * BLOCK, N_KV * 2 * BLOCK),
                       BF16),
        ],
        compiler_params=pltpu.CompilerParams(
            dimension_semantics=("arbitrary", "arbitrary"),
            vmem_limit_bytes=VMEM_LIMIT_BYTES),
        name="mixer_prompt",
    )(x, norm_g, w_in, w_q, conv_w, w_co, w_ao, w_o, bt, sinks)


def _mixer_decode_kernel(*refs, n_alias):
    (x_ref, st_ref, kc_ref, vc_ref, g_ref, win_ref, wq_ref, cw_ref, wco_ref,
     wao_ref, wo_ref, bdec_ref, sink_ref) = refs[:13]
    (y_ref, nst_ref, kout_ref, vout_ref, h_s, q_s, knew_s, vnew_s, knewt_s,
     vnewt_s, ag_s, o_s) = refs[13 + n_alias:]
    i = pl.program_id(0)
    last = pl.num_programs(0) - 1

    @pl.when(i == 0)
    def _():
        x = x_ref[...]
        h = _rms(x, g_ref[2:3, :]).astype(BF16)
        h_s[...] = h
        u = _dot(h, win_ref[:, C_CC:C_CX]) * _dot(h, win_ref[:, C_CX:C_Q])
        st0 = st_ref[:, 0:D_MODEL]
        st1 = st_ref[:, D_MODEL:2 * D_MODEL]
        yc = cw_ref[0:1, :] * st0 + cw_ref[1:2, :] * st1 + cw_ref[2:3, :] * u
        nst_ref[:, 0:D_MODEL] = st1
        nst_ref[:, D_MODEL:2 * D_MODEL] = u
        a_in = (_dot(h, win_ref[:, C_CB:C_CC]) * yc).astype(BF16)
        a_out = _dot(a_in, wco_ref[...])
        ag_s[...] = jax.nn.sigmoid(_dot(h, win_ref[:, C_GA:C_GB])) * a_out
        q_s[...] = _dot(h, wq_ref[...]) * (HEAD_DIM ** -0.5)
        knew = _dot(h, win_ref[:, C_K:C_V])
        vnew = _dot(h, win_ref[:, C_V:C_GA])
        knew_s[...] = knew
        vnew_s[...] = vnew
        knewt_s[...] = knew.T
        vnewt_s[...] = vnew.T

    row = lax.broadcasted_iota(jnp.int32, (N_HEADS, KV_W), 0)
    lane = lax.broadcasted_iota(jnp.int32, (N_HEADS, KV_W), 1)
    row_g = row // N_KV
    seg = (lane // HEAD_DIM) == (row % N_KV)
    key0 = lax.broadcasted_iota(jnp.int32, (N_HEADS, WINDOW), 1) == 0
    newest = lax.broadcasted_iota(jnp.int32, (KV_W, WINDOW), 1) == WINDOW - 1
    bias = bdec_ref[...]
    sink = sink_ref[...]

    for b in range(DEC_SEQS):
        r = i * DEC_SEQS + b
        kt = kc_ref[b]
        vt = vc_ref[b]
        to_last = (WINDOW - 1) - r
        kout_ref[b] = jnp.where(
            newest, pltpu.roll(knewt_s[...], to_last, axis=1),
            pltpu.roll(kt, WINDOW - 1, axis=1))
        vout_ref[b] = jnp.where(
            newest, pltpu.roll(vnewt_s[...], to_last, axis=1),
            pltpu.roll(vt, WINDOW - 1, axis=1))
        q_row = q_s[pl.ds(r, 1), :]
        pieces = [jnp.broadcast_to(q_row[:, g * KV_W:(g + 1) * KV_W],
                                   (N_HEADS, KV_W)) for g in range(GROUP)]
        q_rows = jnp.where(row_g == 0, pieces[0],
                           jnp.where(row_g == 1, pieces[1],
                                     jnp.where(row_g == 2, pieces[2],
                                               pieces[3])))
        qm = jnp.where(seg, q_rows, 0.0).astype(BF16)
        k_row = knew_s[pl.ds(r, 1), :].astype(BF16).astype(F32)
        v_row = vnew_s[pl.ds(r, 1), :].astype(BF16).astype(F32)
        s_new = jnp.sum(qm.astype(F32) * k_row, axis=-1, keepdims=True)
        s = jnp.where(key0, s_new, _dot(qm, kt.astype(BF16))) + bias
        m = jnp.maximum(jnp.max(s, axis=-1, keepdims=True), sink)
        p = jnp.exp(s - m)
        den = jnp.sum(p, axis=-1, keepdims=True) + jnp.exp(sink - m)
        p16 = p.astype(BF16)
        p_new = p16[:, 0:1].astype(F32)
        p_old = jnp.where(key0, jnp.zeros_like(p16), p16)
        o = (_dot_t(p_old, vt.astype(BF16)) + p_new * v_row) / den
        om = jnp.where(seg, o, 0.0)
        for g in range(GROUP):
            o_s[pl.ds(r, 1), g * KV_W:(g + 1) * KV_W] = jnp.sum(
                om[g * N_KV:(g + 1) * N_KV, :], axis=0, keepdims=True)

    @pl.when(i == last)
    def _():
        att = _dot(o_s[...].astype(BF16), wao_ref[...])
        gate_b = jax.nn.sigmoid(_dot(h_s[...], win_ref[:, C_GB:C_END]))
        merged = (ag_s[...] + gate_b * att).astype(BF16)
        y_ref[...] = x_ref[...] + _rms(_dot(merged, wo_ref[...]),
                                       g_ref[3:4, :])


def _mixer_decode(x, st, kc, vc, norm_g, w_in, w_q, conv_w, w_co, w_ao, w_o,
                  bdec, sink_col, layer, prev_caches):
    n = x.shape[0]
    sb = DEC_SEQS
    cache_spec = pl.BlockSpec((None, sb, KV_W, WINDOW),
                              lambda i: (layer, i, 0, 0))
    n_alias = len(prev_caches)
    n_in = 13
    return pl.pallas_call(
        functools.partial(_mixer_decode_kernel, n_alias=n_alias),
        grid=(n // sb,),
        in_specs=[
            _resident((n, D_MODEL)),
            _resident((n, 2 * D_MODEL)),
            cache_spec,
            cache_spec,
            _layer_resident((N_NORMS, D_MODEL), layer),
            _layer_resident((D_MODEL, C_END), layer),
            _layer_resident((D_MODEL, D_MODEL), layer),
            _layer_resident((3, D_MODEL), layer),
            _layer_resident((D_MODEL, D_MODEL), layer),
            _layer_resident((D_MODEL, D_MODEL), layer),
            _layer_resident((D_MODEL, D_MODEL), layer),
            _resident((N_HEADS, WINDOW)),
            _resident((N_HEADS, 1)),
        ] + [pl.BlockSpec(memory_space=pl.ANY)] * n_alias,
        out_specs=[
            pl.BlockSpec((n, D_MODEL), lambda i: (0, 0)),
            pl.BlockSpec((n, 2 * D_MODEL), lambda i: (0, 0)),
            cache_spec,
            cache_spec,
        ],
        out_shape=[
            jax.ShapeDtypeStruct((n, D_MODEL), F32),
            jax.ShapeDtypeStruct((n, 2 * D_MODEL), F32),
            jax.ShapeDtypeStruct(kc.shape, F32),
            jax.ShapeDtypeStruct(vc.shape, F32),
        ],
        input_output_aliases={n_in + a: 2 + a for a in range(n_alias)},
        scratch_shapes=[
            pltpu.VMEM((n, D_MODEL), BF16),
            pltpu.VMEM((n, D_MODEL), F32),
            pltpu.VMEM((n, KV_W), F32),
            pltpu.VMEM((n, KV_W), F32),
            pltpu.VMEM((KV_W, n), F32),
            pltpu.VMEM((KV_W, n), F32),
            pltpu.VMEM((n, D_MODEL), F32),
            pltpu.VMEM((n, D_MODEL), F32),
        ],
        compiler_params=pltpu.CompilerParams(
            dimension_semantics=("arbitrary",),
            vmem_limit_bytes=VMEM_LIMIT_BYTES),
        name="mixer_decode",
    )(x, st, kc, vc, norm_g, w_in, w_q, conv_w, w_co, w_ao, w_o, bdec,
      sink_col, *prev_caches)


def _heads_g_major(w, axis):
    shape = w.shape
    split = shape[:axis] + (N_KV, GROUP, HEAD_DIM) + shape[axis + 1:]
    perm = list(range(len(split)))
    perm[axis], perm[axis + 1] = perm[axis + 1], perm[axis]
    return w.reshape(split).transpose(perm).reshape(shape)


def kernel(x_prompt, x_sample, state_conv, cache_k_win, cache_v_win, rel_bias,
           norm_g, w_ff1_gu, w_ff1_down, w_in, conv_w, sinks, w_conv_out,
           w_attn_out, w_out, w_ff2_gu, w_ff2_down):
    batch, seq, _ = x_prompt.shape
    n_dec = x_sample.shape[0]

    bt = _bias_table(rel_bias)
    bdec = bt[0].reshape(GROUP, BLOCK, N_KV, 2 * BLOCK)[:, BLOCK - 1, :, BLOCK:]
    bdec = bdec.reshape(N_HEADS, WINDOW)
    bdec = jnp.concatenate([bdec[:, WINDOW - 1:], bdec[:, :WINDOW - 1]], axis=1)

    def keys_minor(c):
        return jnp.transpose(c, (0, 1, 3, 4, 2)).reshape(
            DEPTH, n_dec, KV_W, WINDOW)

    def keys_major(c):
        return jnp.transpose(
            c.reshape(DEPTH, n_dec, N_KV, HEAD_DIM, WINDOW), (0, 1, 4, 2, 3))

    kct = keys_minor(cache_k_win)
    vct = keys_minor(cache_v_win)

    w1gu = w_ff1_gu.astype(BF16)
    w1d = w_ff1_down.astype(BF16)
    w2gu = w_ff2_gu.astype(BF16)
    w2d = w_ff2_down.astype(BF16)
    win = w_in.astype(BF16)
    wq = _heads_g_major(w_in[:, :, C_Q:C_K], 2).astype(BF16)
    wco = w_conv_out.astype(BF16)
    wao = _heads_g_major(w_attn_out, 1).astype(BF16)
    wo = w_out.astype(BF16)

    xp = x_prompt.reshape(batch * seq, D_MODEL)
    xs = x_sample.reshape(n_dec, D_MODEL)
    pc, pk, pv, sc = [], [], [], []
    new_caches = ()
    for l in range(DEPTH):
        sink_col = sinks[l].reshape(N_KV, GROUP).T.reshape(N_HEADS, 1)

        xp = _ffn(xp, norm_g, w1gu, w1d, l, 0)
        xs = _ffn(xs, norm_g, w1gu, w1d, l, 0)

        xp, c1, k1, v1 = _mixer_prompt(xp, batch, norm_g, win, wq, conv_w,
                                       wco, wao, wo, bt, sinks, l)
        xs, c2, k2, v2 = _mixer_decode(
            xs, state_conv[l].reshape(n_dec, 2 * D_MODEL), kct, vct,
            norm_g, win, wq, conv_w, wco, wao, wo, bdec, sink_col, l,
            new_caches)
        new_caches = (k2, v2)

        xp = _ffn(xp, norm_g, w2gu, w2d, l, 4)
        xs = _ffn(xs, norm_g, w2gu, w2d, l, 4)

        pc.append(c1)
        pk.append(k1.reshape(batch, WINDOW, N_KV, HEAD_DIM))
        pv.append(v1.reshape(batch, WINDOW, N_KV, HEAD_DIM))
        sc.append(c2.reshape(n_dec, 2, D_MODEL))

    return (xp.reshape(batch, seq, D_MODEL), xs.reshape(n_dec, 1, D_MODEL),
            jnp.stack(pc), jnp.stack(pk), jnp.stack(pv),
            jnp.stack(sc), keys_major(new_caches[0]),
            keys_major(new_caches[1]))
```

```python
import functools

import numpy as np
import jax
import jax.numpy as jnp
from jax import lax
from jax.experimental import pallas as pl
from jax.experimental.pallas import tpu as pltpu

F32 = jnp.float32
BF16 = jnp.bfloat16

D_MODEL = 1024
D_FF = 2816
N_HEADS = 16
N_KV = 4
GROUP = N_HEADS // N_KV
HEAD_DIM = 64
KV_W = N_KV * HEAD_DIM
WINDOW = 128
BLOCK = 128
N_BUCKETS = 32
MAX_DISTANCE = 128
RMS_EPS = 1e-6
NEG = -1e30
DEPTH = 2
N_NORMS = 6

C_CB, C_CC, C_CX, C_Q, C_K, C_V, C_GA, C_GB, C_END = (
    0, 1024, 2048, 3072, 4096, 4352, 4608, 5632, 6656)

VMEM_LIMIT_BYTES = 56 * 1024 * 1024

FFN_ROWS = 256
FFN_CHUNK = 256
FFN_STAGE_COLS = 512
FFN_STAGE_ROWS = 256
MIX_ROWS = 256
DEC_SEQS = 8


def _t5_bucket_np(rel):
    n = np.maximum(rel, 0)
    max_exact = N_BUCKETS // 2
    nf = np.maximum(n, 1).astype(np.float32)
    large = max_exact + (
        np.log(nf / max_exact) / np.log(MAX_DISTANCE / max_exact)
        * (N_BUCKETS - max_exact)).astype(np.int32)
    large = np.minimum(large, N_BUCKETS - 1)
    return np.where(n < max_exact, n, large).astype(np.int32)


def _band_tables():
    qi = np.arange(BLOCK)[:, None]
    kj = np.arange(2 * BLOCK)[None, :]
    rel = qi + BLOCK - kj
    bucket = _t5_bucket_np(rel)
    band = (rel >= 0) & (rel < WINDOW)
    mask = np.stack([band, band & (kj >= BLOCK)]).astype(np.float32)
    return bucket, mask


_BUCKET_NP, _MASK_NP = _band_tables()


def _rms(x, g):
    r = lax.rsqrt(jnp.mean(x * x, axis=-1, keepdims=True) + RMS_EPS)
    return (x * r) * g


def _dot(a, b):
    return jnp.dot(a, b, preferred_element_type=F32)


def _dot_t(a, b):
    return lax.dot_general(a, b, (((1,), (1,)), ((), ())),
                           preferred_element_type=F32)


def _resident(shape):
    return pl.BlockSpec(shape, lambda *_: (0,) * len(shape),
                        pipeline_mode=pl.Buffered(1))


def _layer_resident(shape, layer):
    return pl.BlockSpec((None,) + tuple(shape),
                        lambda *_: (layer,) + (0,) * len(shape),
                        pipeline_mode=pl.Buffered(1))


def _bias_kernel(rb_ref, bucket_ref, mask_ref, out_ref):
    g = pl.program_id(1)
    hk = pl.program_id(2)
    h = hk * GROUP + g
    bucket = bucket_ref[...]

    def body(b, acc):
        return jnp.where(bucket == b, rb_ref[b, h], acc)

    bias = lax.fori_loop(0, N_BUCKETS, body,
                         jnp.zeros((BLOCK, 2 * BLOCK), F32))
    out_ref[...] = jnp.where(mask_ref[...] > 0.0, bias, NEG)


def _bias_table(rel_bias):
    return pl.pallas_call(
        _bias_kernel,
        grid=(2, GROUP, N_KV),
        in_specs=[
            pl.BlockSpec(memory_space=pltpu.SMEM),
            pl.BlockSpec((BLOCK, 2 * BLOCK), lambda v, g, hk: (0, 0)),
            pl.BlockSpec((None, BLOCK, 2 * BLOCK), lambda v, g, hk: (v, 0, 0)),
        ],
        out_specs=pl.BlockSpec((None, BLOCK, 2 * BLOCK),
                               lambda v, g, hk: (v, g, hk)),
        out_shape=jax.ShapeDtypeStruct(
            (2, GROUP * BLOCK, N_KV * 2 * BLOCK), F32),
        name="bias_table",
    )(rel_bias, jnp.asarray(_BUCKET_NP), jnp.asarray(_MASK_NP))


def _ffn_load_weights(wgu_hbm, wd_hbm, wgu_s, wd_s, gu_stage, d_stage,
                      gu_sem, d_sem, layer):
    n_gu = (2 * D_FF) // FFN_STAGE_COLS
    n_d = D_FF // FFN_STAGE_ROWS
    tiles = FFN_STAGE_COLS // FFN_CHUNK

    def gu_copy(k, slot):
        return pltpu.make_async_copy(
            wgu_hbm.at[layer, :, pl.ds(k * FFN_STAGE_COLS, FFN_STAGE_COLS)],
            gu_stage.at[slot], gu_sem.at[slot])

    def d_copy(k, slot):
        return pltpu.make_async_copy(
            wd_hbm.at[layer, pl.ds(k * FFN_STAGE_ROWS, FFN_STAGE_ROWS), :],
            d_stage.at[slot], d_sem.at[slot])

    gu_copy(0, 0).start()
    d_copy(0, 0).start()

    def gu_body(k, carry):
        slot = k % 2

        @pl.when(k + 1 < n_gu)
        def _():
            gu_copy(k + 1, 1 - slot).start()

        gu_copy(k, slot).wait()
        for t in range(tiles):
            wgu_s[k * tiles + t] = gu_stage[
                slot, :, t * FFN_CHUNK:(t + 1) * FFN_CHUNK].astype(BF16)
        return carry

    lax.fori_loop(0, n_gu, gu_body, 0)

    def d_body(k, carry):
        slot = k % 2

        @pl.when(k + 1 < n_d)
        def _():
            d_copy(k + 1, 1 - slot).start()

        d_copy(k, slot).wait()
        r0 = pl.multiple_of(k * FFN_STAGE_ROWS, FFN_STAGE_ROWS)
        wd_s[pl.ds(r0, FFN_STAGE_ROWS), :] = d_stage[slot].astype(BF16)
        return carry

    lax.fori_loop(0, n_d, d_body, 0)


def _ffn_rows(x, g_ref, wgu_s, wd_s, act_ref, gi):
    rows = x.shape[0]
    n_chunks = D_FF // FFN_CHUNK
    h = _rms(x, g_ref[gi:gi + 1, :]).astype(BF16)
    for c in range(n_chunks):
        lo = c * FFN_CHUNK
        gate = _dot(h, wgu_s[c])
        up = _dot(h, wgu_s[n_chunks + c])
        act_ref[0:rows, lo:lo + FFN_CHUNK] = (
            gate * jax.nn.sigmoid(gate) * up).astype(BF16)
    y = _dot(act_ref[0:rows, :], wd_s[...])
    return x + 0.5 * _rms(y, g_ref[gi + 1:gi + 2, :])


def _ffn_kernel(xp_ref, xs_ref, g_ref, wgu_hbm, wd_hbm, op_ref, os_ref,
                wgu_s, wd_s, gu_stage, d_stage, gu_sem, d_sem, act_ref,
                *, gi, layer, prompt_steps):
    i = pl.program_id(0)

    @pl.when(i == 0)
    def _():
        _ffn_load_weights(wgu_hbm, wd_hbm, wgu_s, wd_s, gu_stage, d_stage,
                          gu_sem, d_sem, layer)

    @pl.when(i < prompt_steps)
    def _():
        op_ref[...] = _ffn_rows(xp_ref[...], g_ref, wgu_s, wd_s, act_ref, gi)

    @pl.when(i == prompt_steps)
    def _():
        os_ref[...] = _ffn_rows(xs_ref[...], g_ref, wgu_s, wd_s, act_ref, gi)


def _ffn(xp, xs, norm_g, w_gu, w_down, layer, gi):
    m = xp.shape[0]
    n = xs.shape[0]
    tm = FFN_ROWS
    steps = m // tm
    n_chunks = D_FF // FFN_CHUNK
    row_block = lambda i: (jnp.minimum(i, steps - 1), 0)
    return pl.pallas_call(
        functools.partial(_ffn_kernel, gi=gi, layer=layer,
                          prompt_steps=steps),
        grid=(steps + 1,),
        in_specs=[
            pl.BlockSpec((tm, D_MODEL), row_block),
            _resident((n, D_MODEL)),
            _layer_resident((N_NORMS, D_MODEL), layer),
            pl.BlockSpec(memory_space=pl.ANY),
            pl.BlockSpec(memory_space=pl.ANY),
        ],
        out_specs=[
            pl.BlockSpec((tm, D_MODEL), row_block),
            pl.BlockSpec((n, D_MODEL), lambda i: (0, 0)),
        ],
        out_shape=[
            jax.ShapeDtypeStruct((m, D_MODEL), F32),
            jax.ShapeDtypeStruct((n, D_MODEL), F32),
        ],
        scratch_shapes=[
            pltpu.VMEM((2 * n_chunks, D_MODEL, FFN_CHUNK), BF16),
            pltpu.VMEM((D_FF, D_MODEL), BF16),
            pltpu.VMEM((2, D_MODEL, FFN_STAGE_COLS), F32),
            pltpu.VMEM((2, FFN_STAGE_ROWS, D_MODEL), F32),
            pltpu.SemaphoreType.DMA((2,)),
            pltpu.SemaphoreType.DMA((2,)),
            pltpu.VMEM((tm, D_FF), BF16),
        ],
        compiler_params=pltpu.CompilerParams(
            dimension_semantics=("arbitrary",),
            vmem_limit_bytes=VMEM_LIMIT_BYTES),
        name="ffn_half",
    )(xp, xs, norm_g, w_gu, w_down)


def _lane_segment_mask(rows):
    lane = lax.broadcasted_iota(jnp.int32, (rows, KV_W), 1)
    return [(lane >= hk * HEAD_DIM) & (lane < (hk + 1) * HEAD_DIM)
            for hk in range(N_KV)]


def _mixer_prompt_kernel(x_ref, g_ref, win_ref, wq_ref, cw_ref, wco_ref,
                         wao_ref, wo_ref, bt_ref, sinks_ref,
                         y_ref, cst_ref, kout_ref, vout_ref,
                         q_s, kbuf, vbuf, upad, o_s, p_s,
                         *, layer, rows):
    i = pl.program_id(1)
    last = pl.num_programs(1) - 1
    nq = rows // BLOCK

    @pl.when(i == 0)
    def _():
        upad[0:8, :] = jnp.zeros((8, D_MODEL), F32)
        kbuf[0:BLOCK, :] = jnp.zeros((BLOCK, KV_W), BF16)
        vbuf[0:BLOCK, :] = jnp.zeros((BLOCK, KV_W), BF16)

    @pl.when(i > 0)
    def _():
        upad[0:8, :] = upad[rows:rows + 8, :]
        kbuf[0:BLOCK, :] = kbuf[rows:rows + BLOCK, :]
        vbuf[0:BLOCK, :] = vbuf[rows:rows + BLOCK, :]

    x = x_ref[...]
    h = _rms(x, g_ref[2:3, :]).astype(BF16)

    q_s[...] = (_dot(h, wq_ref[...]) * (HEAD_DIM ** -0.5)).astype(BF16)
    k = _dot(h, win_ref[:, C_K:C_V])
    v = _dot(h, win_ref[:, C_V:C_GA])
    kbuf[BLOCK:BLOCK + rows, :] = k.astype(BF16)
    vbuf[BLOCK:BLOCK + rows, :] = v.astype(BF16)

    @pl.when(i == last)
    def _():
        kout_ref[...] = k[rows - BLOCK:rows, :]
        vout_ref[...] = v[rows - BLOCK:rows, :]

    seg = _lane_segment_mask(2 * BLOCK)
    seg_q = _lane_segment_mask(BLOCK)
    zero_kv = jnp.zeros((2 * BLOCK, KV_W), BF16)
    first = (i == 0).astype(jnp.int32)

    for j in range(nq):
        r0 = j * BLOCK
        variant = first if j == 0 else 0
        qj = q_s[r0:r0 + BLOCK, :]
        kw = kbuf[r0:r0 + 2 * BLOCK, :]
        vw = vbuf[r0:r0 + 2 * BLOCK, :]
        qs = jnp.concatenate(
            [qj[:, g * KV_W:(g + 1) * KV_W] for g in range(GROUP)], axis=0)
        kb = jnp.concatenate(
            [jnp.where(seg[hk], kw, zero_kv) for hk in range(N_KV)], axis=0)
        vb = jnp.concatenate(
            [jnp.where(seg[hk], vw, zero_kv) for hk in range(N_KV)], axis=0)
        s_all = _dot_t(qs, kb)
        inv = []
        for g in range(GROUP):
            inv_g = []
            for hk in range(N_KV):
                rs = slice(g * BLOCK, (g + 1) * BLOCK)
                cs = slice(hk * 2 * BLOCK, (hk + 1) * 2 * BLOCK)
                s = s_all[rs, cs] + bt_ref[variant, rs, cs]
                sink = sinks_ref[layer, hk * GROUP + g]
                m = jnp.maximum(jnp.max(s, axis=-1, keepdims=True), sink)
                p = jnp.exp(s - m)
                den = jnp.sum(p, axis=-1, keepdims=True) + jnp.exp(sink - m)
                p_s[j, rs, cs] = p.astype(BF16)
                inv_g.append(1.0 / den)
            inv.append(inv_g)
        o_all = _dot(p_s[j], vb)
        for g in range(GROUP):
            scale = jnp.where(
                seg_q[0], inv[g][0],
                jnp.where(seg_q[1], inv[g][1],
                          jnp.where(seg_q[2], inv[g][2], inv[g][3])))
            o_s[r0:r0 + BLOCK, g * KV_W:(g + 1) * KV_W] = (
                o_all[g * BLOCK:(g + 1) * BLOCK, :] * scale).astype(BF16)

    u = _dot(h, win_ref[:, C_CC:C_CX]) * _dot(h, win_ref[:, C_CX:C_Q])
    upad[8:rows + 8, :] = u

    @pl.when(i == last)
    def _():
        cst_ref[...] = upad[rows + 6:rows + 8, :]

    yc = (cw_ref[0:1, :] * upad[6:rows + 6, :]
          + cw_ref[1:2, :] * upad[7:rows + 7, :]
          + cw_ref[2:3, :] * u)
    a_in = (_dot(h, win_ref[:, C_CB:C_CC]) * yc).astype(BF16)
    a_out = _dot(a_in, wco_ref[...])
    gated_a = jax.nn.sigmoid(_dot(h, win_ref[:, C_GA:C_GB])) * a_out

    att = _dot(o_s[...], wao_ref[...])
    gate_b = jax.nn.sigmoid(_dot(h, win_ref[:, C_GB:C_END]))
    merged = (gated_a + gate_b * att).astype(BF16)
    y_ref[...] = x + _rms(_dot(merged, wo_ref[...]), g_ref[3:4, :])


def _mixer_prompt(x, batch, norm_g, w_in, w_q, conv_w, w_co, w_ao, w_o, bt,
                  sinks, layer):
    m = x.shape[0]
    seq = m // batch
    rows = MIX_ROWS
    steps = seq // rows
    kern = functools.partial(_mixer_prompt_kernel, layer=layer, rows=rows)
    return pl.pallas_call(
        kern,
        grid=(batch, steps),
        in_specs=[
            pl.BlockSpec((rows, D_MODEL), lambda b, i: (b * steps + i, 0)),
            _layer_resident((N_NORMS, D_MODEL), layer),
            _layer_resident((D_MODEL, C_END), layer),
            _layer_resident((D_MODEL, D_MODEL), layer),
            _layer_resident((3, D_MODEL), layer),
            _layer_resident((D_MODEL, D_MODEL), layer),
            _layer_resident((D_MODEL, D_MODEL), layer),
            _layer_resident((D_MODEL, D_MODEL), layer),
            _resident((2, GROUP * BLOCK, N_KV * 2 * BLOCK)),
            pl.BlockSpec(memory_space=pltpu.SMEM),
        ],
        out_specs=[
            pl.BlockSpec((rows, D_MODEL), lambda b, i: (b * steps + i, 0)),
            pl.BlockSpec((None, 2, D_MODEL), lambda b, i: (b, 0, 0)),
            pl.BlockSpec((None, BLOCK, KV_W), lambda b, i: (b, 0, 0)),
            pl.BlockSpec((None, BLOCK, KV_W), lambda b, i: (b, 0, 0)),
        ],
        out_shape=[
            jax.ShapeDtypeStruct((m, D_MODEL), F32),
            jax.ShapeDtypeStruct((batch, 2, D_MODEL), F32),
            jax.ShapeDtypeStruct((batch, BLOCK, KV_W), F32),
            jax.ShapeDtypeStruct((batch, BLOCK, KV_W), F32),
        ],
        scratch_shapes=[
            pltpu.VMEM((rows, D_MODEL), BF16),
            pltpu.VMEM((rows + BLOCK, KV_W), BF16),
            pltpu.VMEM((rows + BLOCK, KV_W), BF16),
            pltpu.VMEM((rows + 8, D_MODEL), F32),
            pltpu.VMEM((rows, D_MODEL), BF16),
            pltpu.VMEM((rows // BLOCK, GROUP * BLOCK, N_KV * 2 * BLOCK),
                       BF16),
        ],
        compiler_params=pltpu.CompilerParams(
            dimension_semantics=("arbitrary", "arbitrary"),
            vmem_limit_bytes=VMEM_LIMIT_BYTES),
        name="mixer_prompt",
    )(x, norm_g, w_in, w_q, conv_w, w_co, w_ao, w_o, bt, sinks)


def _mixer_decode_kernel(*refs, n_alias):
    (x_ref, st_ref, kc_ref, vc_ref, g_ref, win_ref, wq_ref, cw_ref, wco_ref,
     wao_ref, wo_ref, bdec_ref, sink_ref) = refs[:13]
    (y_ref, nst_ref, kout_ref, vout_ref, h_s, q_s, knew_s, vnew_s, knewt_s,
     vnewt_s, ag_s, o_s) = refs[13 + n_alias:]
    i = pl.program_id(0)
    last = pl.num_programs(0) - 1

    @pl.when(i == 0)
    def _():
        x = x_ref[...]
        h = _rms(x, g_ref[2:3, :]).astype(BF16)
        h_s[...] = h
        u = _dot(h, win_ref[:, C_CC:C_CX]) * _dot(h, win_ref[:, C_CX:C_Q])
        st0 = st_ref[:, 0:D_MODEL]
        st1 = st_ref[:, D_MODEL:2 * D_MODEL]
        yc = cw_ref[0:1, :] * st0 + cw_ref[1:2, :] * st1 + cw_ref[2:3, :] * u
        nst_ref[:, 0:D_MODEL] = st1
        nst_ref[:, D_MODEL:2 * D_MODEL] = u
        a_in = (_dot(h, win_ref[:, C_CB:C_CC]) * yc).astype(BF16)
        a_out = _dot(a_in, wco_ref[...])
        ag_s[...] = jax.nn.sigmoid(_dot(h, win_ref[:, C_GA:C_GB])) * a_out
        q_s[...] = _dot(h, wq_ref[...]) * (HEAD_DIM ** -0.5)
        knew = _dot(h, win_ref[:, C_K:C_V])
        vnew = _dot(h, win_ref[:, C_V:C_GA])
        knew_s[...] = knew
        vnew_s[...] = vnew
        knewt_s[...] = knew.T
        vnewt_s[...] = vnew.T

    row = lax.broadcasted_iota(jnp.int32, (N_HEADS, KV_W), 0)
    lane = lax.broadcasted_iota(jnp.int32, (N_HEADS, KV_W), 1)
    row_g = row // N_KV
    seg = (lane // HEAD_DIM) == (row % N_KV)
    key0 = lax.broadcasted_iota(jnp.int32, (N_HEADS, WINDOW), 1) == 0
    newest = lax.broadcasted_iota(jnp.int32, (KV_W, WINDOW), 1) == WINDOW - 1
    bias = bdec_ref[...]
    sink = sink_ref[...]

    for b in range(DEC_SEQS):
        r = i * DEC_SEQS + b
        kt = kc_ref[b]
        vt = vc_ref[b]
        to_last = (WINDOW - 1) - r
        kout_ref[b] = jnp.where(
            newest, pltpu.roll(knewt_s[...], to_last, axis=1),
            pltpu.roll(kt, WINDOW - 1, axis=1))
        vout_ref[b] = jnp.where(
            newest, pltpu.roll(vnewt_s[...], to_last, axis=1),
            pltpu.roll(vt, WINDOW - 1, axis=1))
        q_row = q_s[pl.ds(r, 1), :]
        pieces = [jnp.broadcast_to(q_row[:, g * KV_W:(g + 1) * KV_W],
                                   (N_HEADS, KV_W)) for g in range(GROUP)]
        q_rows = jnp.where(row_g == 0, pieces[0],
                           jnp.where(row_g == 1, pieces[1],
                                     jnp.where(row_g == 2, pieces[2],
                                               pieces[3])))
        qm = jnp.where(seg, q_rows, 0.0).astype(BF16)
        k_row = knew_s[pl.ds(r, 1), :].astype(BF16).astype(F32)
        v_row = vnew_s[pl.ds(r, 1), :].astype(BF16).astype(F32)
        s_new = jnp.sum(qm.astype(F32) * k_row, axis=-1, keepdims=True)
        s = jnp.where(key0, s_new, _dot(qm, kt.astype(BF16))) + bias
        m = jnp.maximum(jnp.max(s, axis=-1, keepdims=True), sink)
        p = jnp.exp(s - m)
        den = jnp.sum(p, axis=-1, keepdims=True) + jnp.exp(sink - m)
        p16 = p.astype(BF16)
        p_new = p16[:, 0:1].astype(F32)
        p_old = jnp.where(key0, jnp.zeros_like(p16), p16)
        o = (_dot_t(p_old, vt.astype(BF16)) + p_new * v_row) / den
        om = jnp.where(seg, o, 0.0)
        for g in range(GROUP):
            o_s[pl.ds(r, 1), g * KV_W:(g + 1) * KV_W] = jnp.sum(
                om[g * N_KV:(g + 1) * N_KV, :], axis=0, keepdims=True)

    @pl.when(i == last)
    def _():
        att = _dot(o_s[...].astype(BF16), wao_ref[...])
        gate_b = jax.nn.sigmoid(_dot(h_s[...], win_ref[:, C_GB:C_END]))
        merged = (ag_s[...] + gate_b * att).astype(BF16)
        y_ref[...] = x_ref[...] + _rms(_dot(merged, wo_ref[...]),
                                       g_ref[3:4, :])


def _mixer_decode(x, st, kc, vc, norm_g, w_in, w_q, conv_w, w_co, w_ao, w_o,
                  bdec, sink_col, layer, prev_caches):
    n = x.shape[0]
    sb = DEC_SEQS
    cache_spec = pl.BlockSpec((None, sb, KV_W, WINDOW),
                              lambda i: (layer, i, 0, 0))
    n_alias = len(prev_caches)
    n_in = 13
    return pl.pallas_call(
        functools.partial(_mixer_decode_kernel, n_alias=n_alias),
        grid=(n // sb,),
        in_specs=[
            _resident((n, D_MODEL)),
            _resident((n, 2 * D_MODEL)),
            cache_spec,
            cache_spec,
            _layer_resident((N_NORMS, D_MODEL), layer),
            _layer_resident((D_MODEL, C_END), layer),
            _layer_resident((D_MODEL, D_MODEL), layer),
            _layer_resident((3, D_MODEL), layer),
            _layer_resident((D_MODEL, D_MODEL), layer),
            _layer_resident((D_MODEL, D_MODEL), layer),
            _layer_resident((D_MODEL, D_MODEL), layer),
            _resident((N_HEADS, WINDOW)),
            _resident((N_HEADS, 1)),
        ] + [pl.BlockSpec(memory_space=pl.ANY)] * n_alias,
        out_specs=[
            pl.BlockSpec((n, D_MODEL), lambda i: (0, 0)),
            pl.BlockSpec((n, 2 * D_MODEL), lambda i: (0, 0)),
            cache_spec,
            cache_spec,
        ],
        out_shape=[
            jax.ShapeDtypeStruct((n, D_MODEL), F32),
            jax.ShapeDtypeStruct((n, 2 * D_MODEL), F32),
            jax.ShapeDtypeStruct(kc.shape, F32),
            jax.ShapeDtypeStruct(vc.shape, F32),
        ],
        input_output_aliases={n_in + a: 2 + a for a in range(n_alias)},
        scratch_shapes=[
            pltpu.VMEM((n, D_MODEL), BF16),
            pltpu.VMEM((n, D_MODEL), F32),
            pltpu.VMEM((n, KV_W), F32),
            pltpu.VMEM((n, KV_W), F32),
            pltpu.VMEM((KV_W, n), F32),
            pltpu.VMEM((KV_W, n), F32),
            pltpu.VMEM((n, D_MODEL), F32),
            pltpu.VMEM((n, D_MODEL), F32),
        ],
        compiler_params=pltpu.CompilerParams(
            dimension_semantics=("arbitrary",),
            vmem_limit_bytes=VMEM_LIMIT_BYTES),
        name="mixer_decode",
    )(x, st, kc, vc, norm_g, w_in, w_q, conv_w, w_co, w_ao, w_o, bdec,
      sink_col, *prev_caches)


def _heads_g_major(w, axis):
    shape = w.shape
    split = shape[:axis] + (N_KV, GROUP, HEAD_DIM) + shape[axis + 1:]
    perm = list(range(len(split)))
    perm[axis], perm[axis + 1] = perm[axis + 1], perm[axis]
    return w.reshape(split).transpose(perm).reshape(shape)


def kernel(x_prompt, x_sample, state_conv, cache_k_win, cache_v_win, rel_bias,
           norm_g, w_ff1_gu, w_ff1_down, w_in, conv_w, sinks, w_conv_out,
           w_attn_out, w_out, w_ff2_gu, w_ff2_down):
    batch, seq, _ = x_prompt.shape
    n_dec = x_sample.shape[0]

    bt = _bias_table(rel_bias)
    bdec = bt[0].reshape(GROUP, BLOCK, N_KV, 2 * BLOCK)[:, BLOCK - 1, :, BLOCK:]
    bdec = bdec.reshape(N_HEADS, WINDOW)
    bdec = jnp.concatenate([bdec[:, WINDOW - 1:], bdec[:, :WINDOW - 1]], axis=1)

    def keys_minor(c):
        return jnp.transpose(c, (0, 1, 3, 4, 2)).reshape(
            DEPTH, n_dec, KV_W, WINDOW)

    def keys_major(c):
        return jnp.transpose(
            c.reshape(DEPTH, n_dec, N_KV, HEAD_DIM, WINDOW), (0, 1, 4, 2, 3))

    kct = keys_minor(cache_k_win)
    vct = keys_minor(cache_v_win)

    win =w_in.astype(BF16)
    wq = _heads_g_major(w_in[:, :, C_Q:C_K], 2).astype(BF16)
    wco = w_conv_out.astype(BF16)
    wao = _heads_g_major(w_attn_out, 1).astype(BF16)
    wo = w_out.astype(BF16)

    xp = x_prompt.reshape(batch * seq, D_MODEL)
    xs = x_sample.reshape(n_dec, D_MODEL)
    pc, pk, pv, sc = [], [], [], []
    new_caches = ()
    for l in range(DEPTH):
        sink_col = sinks[l].reshape(N_KV, GROUP).T.reshape(N_HEADS, 1)

        xp, xs = _ffn(xp, xs, norm_g, w_ff1_gu, w_ff1_down, l, 0)

        xp, c1, k1, v1 = _mixer_prompt(xp, batch, norm_g, win, wq, conv_w,
                                       wco, wao, wo, bt, sinks, l)
        xs, c2, k2, v2 = _mixer_decode(
            xs, state_conv[l].reshape(n_dec, 2 * D_MODEL), kct, vct,
            norm_g, win, wq, conv_w, wco, wao, wo, bdec, sink_col, l,
            new_caches)
        new_caches = (k2, v2)

        xp, xs = _ffn(xp, xs, norm_g, w_ff2_gu, w_ff2_down, l, 4)

        pc.append(c1)
        pk.append(k1.reshape(batch, WINDOW, N_KV, HEAD_DIM))
        pv.append(v1.reshape(batch, WINDOW, N_KV, HEAD_DIM))
        sc.append(c2.reshape(n_dec, 2, D_MODEL))

    return (xp.reshape(batch, seq, D_MODEL), xs.reshape(n_dec, 1, D_MODEL),
            jnp.stack(pc), jnp.stack(pk), jnp.stack(pv),
            jnp.stack(sc), keys_major(new_caches[0]),
            keys_major(new_caches[1]))
```

```python
import functools
from typing import Callable, NamedTuple

import numpy as np
import jax
import jax.numpy as jnp
from jax import lax
from jax.experimental import pallas as pl
from jax.experimental.pallas import tpu as pltpu

F32 = jnp.float32
BF16 = jnp.bfloat16

D_MODEL = 1024
D_FF = 2816
N_HEADS = 16
N_KV = 4
GROUP = N_HEADS // N_KV
HEAD_DIM = 64
KV_W = N_KV * HEAD_DIM
WINDOW = 128
BLOCK = 128
N_BUCKETS = 32
MAX_DISTANCE = 128
RMS_EPS = 1e-6
NEG = -1e30
DEPTH = 2
N_NORMS = 6

C_CB, C_CC, C_CX, C_Q, C_K, C_V, C_GA, C_GB, C_END = (
    0, 1024, 2048, 3072, 4096, 4352, 4608, 5632, 6656)

VMEM_LIMIT_BYTES = 56 * 1024 * 1024

FFN_ROWS = 256
FFN_CHUNK = 256
MIX_ROWS = 256
DEC_SEQS = 8


def _t5_bucket_np(rel):
    n = np.maximum(rel, 0)
    max_exact = N_BUCKETS // 2
    nf = np.maximum(n, 1).astype(np.float32)
    large = max_exact + (
        np.log(nf / max_exact) / np.log(MAX_DISTANCE / max_exact)
        * (N_BUCKETS - max_exact)).astype(np.int32)
    large = np.minimum(large, N_BUCKETS - 1)
    return np.where(n < max_exact, n, large).astype(np.int32)


def _band_tables():
    qi = np.arange(BLOCK)[:, None]
    kj = np.arange(2 * BLOCK)[None, :]
    rel = qi + BLOCK - kj
    bucket = _t5_bucket_np(rel)
    band = (rel >= 0) & (rel < WINDOW)
    mask = np.stack([band, band & (kj >= BLOCK)]).astype(np.float32)
    return bucket, mask


_BUCKET_NP, _MASK_NP = _band_tables()


def _rms(x, g):
    r = lax.rsqrt(jnp.mean(x * x, axis=-1, keepdims=True) + RMS_EPS)
    return (x * r) * g


def _dot(a, b):
    return jnp.dot(a, b, preferred_element_type=F32)


def _dot_t(a, b):
    return lax.dot_general(a, b, (((1,), (1,)), ((), ())),
                           preferred_element_type=F32)


def _resident(shape):
    return pl.BlockSpec(shape, lambda *_: (0,) * len(shape),
                        pipeline_mode=pl.Buffered(1))


def _layer_resident(shape, layer):
    return pl.BlockSpec((None,) + tuple(shape),
                        lambda *_: (layer,) + (0,) * len(shape),
                        pipeline_mode=pl.Buffered(1))


class _CastJob(NamedTuple):
    src: jax.Array
    layer: int
    rows: int
    src_block: Callable = lambda d: d

    @property
    def n_blocks(self):
        return self.src.shape[1] // self.rows


def _cast_specs(jobs, step_of):
    in_specs, out_specs, out_shapes = [], [], []
    for job in jobs:
        cols = job.src.shape[2]

        def dst(*ids, job=job):
            return jnp.minimum(step_of(*ids), job.n_blocks - 1)

        in_specs.append(pl.BlockSpec(
            (None, job.rows, cols),
            lambda *ids, job=job, dst=dst: (job.layer,
                                            job.src_block(dst(*ids)), 0)))
        out_specs.append(pl.BlockSpec(
            (job.rows, cols), lambda *ids, dst=dst: (dst(*ids), 0)))
        out_shapes.append(jax.ShapeDtypeStruct(job.src.shape[1:], BF16))
    return in_specs, out_specs, out_shapes


def _run_casts(step, n_blocks, in_refs, out_refs):
    for nb, src, dst in zip(n_blocks, in_refs, out_refs):
        @pl.when(step < nb)
        def _(src=src, dst=dst):
            dst[...] = src[...].astype(BF16)


def _ffn_cast_jobs(w_gu, w_down, layer):
    return [_CastJob(w_gu, layer, 32), _CastJob(w_down, layer, 128)]


def _mixer_cast_jobs(w_in, w_co, w_ao, w_o, layer):
    to_g_major = lambda d: (d % GROUP) * N_KV + d // GROUP
    return [_CastJob(w_in, layer, 32), _CastJob(w_co, layer, 64),
            _CastJob(w_ao, layer, HEAD_DIM, to_g_major),
            _CastJob(w_o, layer, 64)]


def _bias_kernel(*refs, cast_blocks):
    nj = len(cast_blocks)
    rb_ref, bucket_ref, mask_ref = refs[:3]
    cast_in = refs[3:3 + nj]
    out_ref = refs[3 + nj]
    cast_out = refs[4 + nj:]
    g = pl.program_id(1)
    hk = pl.program_id(2)
    step = (pl.program_id(0) * GROUP + g) * N_KV + hk
    _run_casts(step, cast_blocks, cast_in, cast_out)
    h = hk * GROUP + g
    bucket = bucket_ref[...]

    def body(b, acc):
        return jnp.where(bucket == b, rb_ref[b, h], acc)

    bias = lax.fori_loop(0, N_BUCKETS, body,
                         jnp.zeros((BLOCK, 2 * BLOCK), F32))
    out_ref[...] = jnp.where(mask_ref[...] > 0.0, bias, NEG)


def _bias_table(rel_bias, jobs):
    c_in, c_out, c_shapes = _cast_specs(
        jobs, lambda v, g, hk: (v * GROUP + g) * N_KV + hk)
    outs = pl.pallas_call(
        functools.partial(_bias_kernel,
                          cast_blocks=tuple(j.n_blocks for j in jobs)),
        grid=(2, GROUP, N_KV),
        in_specs=[
            pl.BlockSpec(memory_space=pltpu.SMEM),
            pl.BlockSpec((BLOCK, 2 * BLOCK), lambda v, g, hk: (0, 0)),
            pl.BlockSpec((None, BLOCK, 2 * BLOCK), lambda v, g, hk: (v, 0, 0)),
        ] + c_in,
        out_specs=[pl.BlockSpec((None, BLOCK, 2 * BLOCK),
                                lambda v, g, hk: (v, g, hk))] + c_out,
        out_shape=[jax.ShapeDtypeStruct(
            (2, GROUP * BLOCK, N_KV * 2 * BLOCK), F32)] + c_shapes,
        compiler_params=pltpu.CompilerParams(
            dimension_semantics=("arbitrary",) * 3),
        name="bias_table",
    )(rel_bias, jnp.asarray(_BUCKET_NP), jnp.asarray(_MASK_NP),
      *[j.src for j in jobs])
    return outs[0], outs[1:]


def _ffn_rows(x, g_ref, wgu_ref, wd_ref, act_ref, gi):
    rows = x.shape[0]
    h = _rms(x, g_ref[gi:gi + 1, :]).astype(BF16)
    for c in range(D_FF // FFN_CHUNK):
        lo = c * FFN_CHUNK
        gate = _dot(h, wgu_ref[:, lo:lo + FFN_CHUNK])
        up = _dot(h, wgu_ref[:, D_FF + lo:D_FF + lo + FFN_CHUNK])
        act_ref[0:rows, lo:lo + FFN_CHUNK] = (
            gate * jax.nn.sigmoid(gate) * up).astype(BF16)
    y = _dot(act_ref[0:rows, :], wd_ref[...])
    return x + 0.5 * _rms(y, g_ref[gi + 1:gi + 2, :])


def _ffn_kernel(*refs, gi, prompt_steps, cast_blocks):
    nj = len(cast_blocks)
    xp_ref, xs_ref, g_ref, wgu_ref, wd_ref = refs[:5]
    cast_in = refs[5:5 + nj]
    op_ref, os_ref = refs[5 + nj:7 + nj]
    cast_out = refs[7 + nj:7 + 2 * nj]
    act_ref = refs[7 + 2 * nj]
    i = pl.program_id(0)
    _run_casts(i, cast_blocks, cast_in, cast_out)

    @pl.when(i < prompt_steps)
    def _():
        op_ref[...] = _ffn_rows(xp_ref[...], g_ref, wgu_ref, wd_ref, act_ref,
                                gi)

    @pl.when(i == prompt_steps)
    def _():
        os_ref[...] = _ffn_rows(xs_ref[...], g_ref, wgu_ref, wd_ref, act_ref,
                                gi)


def _ffn(xp, xs, norm_g, w_gu, w_down, layer, gi, jobs):
    m = xp.shape[0]
    n = xs.shape[0]
    tm = FFN_ROWS
    steps = m // tm
    row_block = lambda i: (jnp.minimum(i, steps - 1), 0)
    c_in, c_out, c_shapes = _cast_specs(jobs, lambda i: i)
    outs = pl.pallas_call(
        functools.partial(_ffn_kernel, gi=gi, prompt_steps=steps,
                          cast_blocks=tuple(j.n_blocks for j in jobs)),
        grid=(steps + 1,),
        in_specs=[
            pl.BlockSpec((tm, D_MODEL), row_block),
            _resident((n, D_MODEL)),
            _layer_resident((N_NORMS, D_MODEL), layer),
            _resident((D_MODEL, 2 * D_FF)),
            _resident((D_FF, D_MODEL)),
        ] + c_in,
        out_specs=[
            pl.BlockSpec((tm, D_MODEL), row_block),
            pl.BlockSpec((n, D_MODEL), lambda i: (0, 0)),
        ] + c_out,
        out_shape=[
            jax.ShapeDtypeStruct((m, D_MODEL), F32),
            jax.ShapeDtypeStruct((n, D_MODEL), F32),
        ] + c_shapes,
        scratch_shapes=[pltpu.VMEM((tm, D_FF), BF16)],
        compiler_params=pltpu.CompilerParams(
            dimension_semantics=("arbitrary",),
            vmem_limit_bytes=VMEM_LIMIT_BYTES),
        name="ffn_half",
    )(xp, xs, norm_g, w_gu, w_down, *[j.src for j in jobs])
    return outs[0], outs[1], outs[2:]


def _lane_segment_mask(rows):
    lane = lax.broadcasted_iota(jnp.int32, (rows, KV_W), 1)
    return [(lane >= hk * HEAD_DIM) & (lane < (hk + 1) * HEAD_DIM)
            for hk in range(N_KV)]


def _mixer_prompt_kernel(*refs, layer, rows, cast_blocks):
    nj = len(cast_blocks)
    (x_ref, g_ref, win_ref, wq_ref, cw_ref, wco_ref, wao_ref, wo_ref, bt_ref,
     sinks_ref) = refs[:10]
    cast_in = refs[10:10 + nj]
    y_ref, cst_ref, kout_ref, vout_ref = refs[10 + nj:14 + nj]
    cast_out = refs[14 + nj:14 + 2 * nj]
    q_s, kbuf, vbuf, upad, o_s, p_s = refs[14 + 2 * nj:]
    i = pl.program_id(1)
    steps = pl.num_programs(1)
    last = steps - 1
    nq = rows // BLOCK
    _run_casts(pl.program_id(0) * steps + i, cast_blocks, cast_in, cast_out)

    @pl.when(i == 0)
    def _():
        upad[0:8, :] = jnp.zeros((8, D_MODEL), F32)
        kbuf[0:BLOCK, :] = jnp.zeros((BLOCK, KV_W), BF16)
        vbuf[0:BLOCK, :] = jnp.zeros((BLOCK, KV_W), BF16)

    @pl.when(i > 0)
    def _():
        upad[0:8, :] = upad[rows:rows + 8, :]
        kbuf[0:BLOCK, :] = kbuf[rows:rows + BLOCK, :]
        vbuf[0:BLOCK, :] = vbuf[rows:rows + BLOCK, :]

    x = x_ref[...]
    h = _rms(x, g_ref[2:3, :]).astype(BF16)

    q_s[...] = (_dot(h, wq_ref[...]) * (HEAD_DIM ** -0.5)).astype(BF16)
    k = _dot(h, win_ref[:, C_K:C_V])
    v = _dot(h, win_ref[:, C_V:C_GA])
    kbuf[BLOCK:BLOCK + rows, :] = k.astype(BF16)
    vbuf[BLOCK:BLOCK + rows, :] = v.astype(BF16)

    @pl.when(i == last)
    def _():
        kout_ref[...] = k[rows - BLOCK:rows, :]
        vout_ref[...] = v[rows - BLOCK:rows, :]

    seg = _lane_segment_mask(2 * BLOCK)
    seg_q = _lane_segment_mask(BLOCK)
    zero_kv = jnp.zeros((2 * BLOCK, KV_W), BF16)
    first = (i == 0).astype(jnp.int32)

    for j in range(nq):
        r0 = j * BLOCK
        variant = first if j == 0 else 0
        qj = q_s[r0:r0 + BLOCK, :]
        kw = kbuf[r0:r0 + 2 * BLOCK, :]
        vw = vbuf[r0:r0 + 2 * BLOCK, :]
        qs = jnp.concatenate(
            [qj[:, g * KV_W:(g + 1) * KV_W] for g in range(GROUP)], axis=0)
        kb = jnp.concatenate(
            [jnp.where(seg[hk], kw, zero_kv) for hk in range(N_KV)], axis=0)
        vb = jnp.concatenate(
            [jnp.where(seg[hk], vw, zero_kv) for hk in range(N_KV)], axis=0)
        s_all = _dot_t(qs, kb)
        inv = []
        for g in range(GROUP):
            inv_g = []
            for hk in range(N_KV):
                rs = slice(g * BLOCK, (g + 1) * BLOCK)
                cs = slice(hk * 2 * BLOCK, (hk + 1) * 2 * BLOCK)
                s = s_all[rs, cs] + bt_ref[variant, rs, cs]
                sink = sinks_ref[layer, hk * GROUP + g]
                m = jnp.maximum(jnp.max(s, axis=-1, keepdims=True), sink)
                p = jnp.exp(s - m)
                den = jnp.sum(p, axis=-1, keepdims=True) + jnp.exp(sink - m)
                p_s[j, rs, cs] = p.astype(BF16)
                inv_g.append(1.0 / den)
            inv.append(inv_g)
        o_all = _dot(p_s[j], vb)
        for g in range(GROUP):
            scale = jnp.where(
                seg_q[0], inv[g][0],
                jnp.where(seg_q[1], inv[g][1],
                          jnp.where(seg_q[2], inv[g][2], inv[g][3])))
            o_s[r0:r0 + BLOCK, g * KV_W:(g + 1) * KV_W] = (
                o_all[g * BLOCK:(g + 1) * BLOCK, :] * scale).astype(BF16)

    u = _dot(h, win_ref[:, C_CC:C_CX]) * _dot(h, win_ref[:, C_CX:C_Q])
    upad[8:rows + 8, :] = u

    @pl.when(i == last)
    def _():
        cst_ref[...] = upad[rows + 6:rows + 8, :]

    yc = (cw_ref[0:1, :] * upad[6:rows + 6, :]
          + cw_ref[1:2, :] * upad[7:rows + 7, :]
          + cw_ref[2:3, :] * u)
    a_in = (_dot(h, win_ref[:, C_CB:C_CC]) * yc).astype(BF16)
    a_out = _dot(a_in, wco_ref[...])
    gated_a = jax.nn.sigmoid(_dot(h, win_ref[:, C_GA:C_GB])) * a_out

    att = _dot(o_s[...], wao_ref[...])
    gate_b = jax.nn.sigmoid(_dot(h, win_ref[:, C_GB:C_END]))
    merged = (gated_a + gate_b * att).astype(BF16)
    y_ref[...] = x + _rms(_dot(merged, wo_ref[...]), g_ref[3:4, :])


def _mixer_prompt(x, batch, norm_g, w_in, w_q, conv_w, w_co, w_ao, w_o, bt,
                  sinks, layer, jobs):
    m = x.shape[0]
    seq = m // batch
    rows = MIX_ROWS
    steps = seq // rows
    c_in, c_out, c_shapes = _cast_specs(jobs, lambda b, i: b * steps + i)
    kern = functools.partial(_mixer_prompt_kernel, layer=layer, rows=rows,
                             cast_blocks=tuple(j.n_blocks for j in jobs))
    outs = pl.pallas_call(
        kern,
        grid=(batch, steps),
        in_specs=[
            pl.BlockSpec((rows, D_MODEL), lambda b, i: (b * steps + i, 0)),
            _layer_resident((N_NORMS, D_MODEL), layer),
            _resident((D_MODEL, C_END)),
            _layer_resident((D_MODEL, D_MODEL), layer),
            _layer_resident((3, D_MODEL), layer),
            _resident((D_MODEL, D_MODEL)),
            _resident((D_MODEL, D_MODEL)),
            _resident((D_MODEL, D_MODEL)),
            _resident((2, GROUP * BLOCK, N_KV * 2 * BLOCK)),
            pl.BlockSpec(memory_space=pltpu.SMEM),
        ] + c_in,
        out_specs=[
            pl.BlockSpec((rows, D_MODEL), lambda b, i: (b * steps + i, 0)),
            pl.BlockSpec((None, 2, D_MODEL), lambda b, i: (b, 0, 0)),
            pl.BlockSpec((None, BLOCK, KV_W), lambda b, i: (b, 0, 0)),
            pl.BlockSpec((None, BLOCK, KV_W), lambda b, i: (b, 0, 0)),
        ] + c_out,
        out_shape=[
            jax.ShapeDtypeStruct((m, D_MODEL), F32),
            jax.ShapeDtypeStruct((batch, 2, D_MODEL), F32),
            jax.ShapeDtypeStruct((batch, BLOCK, KV_W), F32),
            jax.ShapeDtypeStruct((batch, BLOCK, KV_W), F32),
        ] + c_shapes,
        scratch_shapes=[
            pltpu.VMEM((rows, D_MODEL), BF16),
            pltpu.VMEM((rows + BLOCK, KV_W), BF16),
            pltpu.VMEM((rows + BLOCK, KV_W), BF16),
            pltpu.VMEM((rows + 8, D_MODEL), F32),
            pltpu.VMEM((rows, D_MODEL), BF16),
            pltpu.VMEM((rows // BLOCK, GROUP * BLOCK, N_KV * 2 * BLOCK),
                       BF16),
        ],
        compiler_params=pltpu.CompilerParams(
            dimension_semantics=("arbitrary", "arbitrary"),
            vmem_limit_bytes=VMEM_LIMIT_BYTES),
        name="mixer_prompt",
    )(x, norm_g, w_in, w_q, conv_w, w_co, w_ao, w_o, bt, sinks,
      *[j.src for j in jobs])
    return outs[0], outs[1], outs[2], outs[3], outs[4:]


def _mixer_decode_kernel(*refs, n_alias):
    (x_ref, st_ref, kc_ref, vc_ref, g_ref, win_ref, wq_ref, cw_ref, wco_ref,
     wao_ref, wo_ref, bdec_ref, sink_ref) = refs[:13]
    (y_ref, nst_ref, kout_ref, vout_ref, h_s, q_s, knew_s, vnew_s, knewt_s,
     vnewt_s, ag_s, o_s) = refs[13 + n_alias:]
    i = pl.program_id(0)
    last = pl.num_programs(0) - 1

    @pl.when(i == 0)
    def _():
        x = x_ref[...]
        h = _rms(x, g_ref[2:3, :]).astype(BF16)
        h_s[...] = h
        u = _dot(h, win_ref[:, C_CC:C_CX]) * _dot(h, win_ref[:, C_CX:C_Q])
        st0 = st_ref[:, 0:D_MODEL]
        st1 = st_ref[:, D_MODEL:2 * D_MODEL]
        yc = cw_ref[0:1, :] * st0 + cw_ref[1:2, :] * st1 + cw_ref[2:3, :] * u
        nst_ref[:, 0:D_MODEL] = st1
        nst_ref[:, D_MODEL:2 * D_MODEL] = u
        a_in = (_dot(h, win_ref[:, C_CB:C_CC]) * yc).astype(BF16)
        a_out = _dot(a_in, wco_ref[...])
        ag_s[...] = jax.nn.sigmoid(_dot(h, win_ref[:, C_GA:C_GB])) * a_out
        q_s[...] = _dot(h, wq_ref[...]) * (HEAD_DIM ** -0.5)
        knew = _dot(h, win_ref[:, C_K:C_V])
        vnew = _dot(h, win_ref[:, C_V:C_GA])
        knew_s[...] = knew
        vnew_s[...] = vnew
        knewt_s[...] = knew.T
        vnewt_s[...] = vnew.T

    row = lax.broadcasted_iota(jnp.int32, (N_HEADS, KV_W), 0)
    lane = lax.broadcasted_iota(jnp.int32, (N_HEADS, KV_W), 1)
    row_g = row // N_KV
    seg = (lane // HEAD_DIM) == (row % N_KV)
    key0 = lax.broadcasted_iota(jnp.int32, (N_HEADS, WINDOW), 1) == 0
    newest = lax.broadcasted_iota(jnp.int32, (KV_W, WINDOW), 1) == WINDOW - 1
    bias = bdec_ref[...]
    sink = sink_ref[...]

    for b in range(DEC_SEQS):
        r = i * DEC_SEQS + b
        kt = kc_ref[b]
        vt = vc_ref[b]
        to_last = (WINDOW - 1) - r
        kout_ref[b] = jnp.where(
            newest, pltpu.roll(knewt_s[...], to_last, axis=1),
            pltpu.roll(kt, WINDOW - 1, axis=1))
        vout_ref[b] = jnp.where(
            newest, pltpu.roll(vnewt_s[...], to_last, axis=1),
            pltpu.roll(vt, WINDOW - 1, axis=1))
        q_row = q_s[pl.ds(r, 1), :]
        pieces = [jnp.broadcast_to(q_row[:, g * KV_W:(g + 1) * KV_W],
                                   (N_HEADS, KV_W)) for g in range(GROUP)]
        q_rows = jnp.where(row_g == 0, pieces[0],
                           jnp.where(row_g == 1, pieces[1],
                                     jnp.where(row_g == 2, pieces[2],
                                               pieces[3])))
        qm = jnp.where(seg, q_rows, 0.0).astype(BF16)
        k_row = knew_s[pl.ds(r, 1), :].astype(BF16).astype(F32)
        v_row = vnew_s[pl.ds(r, 1), :].astype(BF16).astype(F32)
        s_new = jnp.sum(qm.astype(F32) * k_row, axis=-1, keepdims=True)
        s = jnp.where(key0, s_new, _dot(qm, kt.astype(BF16))) + bias
        m = jnp.maximum(jnp.max(s, axis=-1, keepdims=True), sink)
        p = jnp.exp(s - m)
        den = jnp.sum(p, axis=-1, keepdims=True) + jnp.exp(sink - m)
        p16 = p.astype(BF16)
        p_new = p16[:, 0:1].astype(F32)
        p_old = jnp.where(key0, jnp.zeros_like(p16), p16)
        o = (_dot_t(p_old, vt.astype(BF16)) + p_new * v_row) / den
        om = jnp.where(seg, o, 0.0)
        for g in range(GROUP):
            o_s[pl.ds(r, 1), g * KV_W:(g + 1) * KV_W] = jnp.sum(
                om[g * N_KV:(g + 1) * N_KV, :], axis=0, keepdims=True)

    @pl.when(i == last)
    def _():
        att = _dot(o_s[...].astype(BF16), wao_ref[...])
        gate_b = jax.nn.sigmoid(_dot(h_s[...], win_ref[:, C_GB:C_END]))
        merged = (ag_s[...] + gate_b * att).astype(BF16)
        y_ref[...] = x_ref[...] + _rms(_dot(merged, wo_ref[...]),
                                       g_ref[3:4, :])


def _mixer_decode(x, st, kc, vc, norm_g, w_in, w_q, conv_w, w_co, w_ao, w_o,
                  bdec, sink_col, layer, prev_caches):
    n = x.shape[0]
    sb = DEC_SEQS
    cache_spec = pl.BlockSpec((None, sb, KV_W, WINDOW),
                              lambda i: (layer, i, 0, 0))
    n_alias = len(prev_caches)
    n_in = 13
    return pl.pallas_call(
        functools.partial(_mixer_decode_kernel, n_alias=n_alias),
        grid=(n // sb,),
        in_specs=[
            _resident((n, D_MODEL)),
            _resident((n, 2 * D_MODEL)),
            cache_spec,
            cache_spec,
            _layer_resident((N_NORMS, D_MODEL), layer),
            _resident((D_MODEL, C_END)),
            _layer_resident((D_MODEL, D_MODEL), layer),
            _layer_resident((3, D_MODEL), layer),
            _resident((D_MODEL, D_MODEL)),
            _resident((D_MODEL, D_MODEL)),
            _resident((D_MODEL, D_MODEL)),
            _resident((N_HEADS, WINDOW)),
            _resident((N_HEADS, 1)),
        ] + [pl.BlockSpec(memory_space=pl.ANY)] * n_alias,
        out_specs=[
            pl.BlockSpec((n, D_MODEL), lambda i: (0, 0)),
            pl.BlockSpec((n, 2 * D_MODEL), lambda i: (0, 0)),
            cache_spec,
            cache_spec,
        ],
        out_shape=[
            jax.ShapeDtypeStruct((n, D_MODEL), F32),
            jax.ShapeDtypeStruct((n, 2 * D_MODEL), F32),
            jax.ShapeDtypeStruct(kc.shape, F32),
            jax.ShapeDtypeStruct(vc.shape, F32),
        ],
        input_output_aliases={n_in + a: 2 + a for a in range(n_alias)},
        scratch_shapes=[
            pltpu.VMEM((n, D_MODEL), BF16),
            pltpu.VMEM((n, D_MODEL), F32),
            pltpu.VMEM((n, KV_W), F32),
            pltpu.VMEM((n, KV_W), F32),
            pltpu.VMEM((KV_W, n), F32),
            pltpu.VMEM((KV_W, n), F32),
            pltpu.VMEM((n, D_MODEL), F32),
            pltpu.VMEM((n, D_MODEL), F32),
        ],
        compiler_params=pltpu.CompilerParams(
            dimension_semantics=("arbitrary",),
            vmem_limit_bytes=VMEM_LIMIT_BYTES),
        name="mixer_decode",
    )(x, st, kc, vc, norm_g, w_in, w_q, conv_w, w_co, w_ao, w_o, bdec,
      sink_col, *prev_caches)


def _heads_g_major(w, axis):
    shape = w.shape
    split = shape[:axis] + (N_KV, GROUP, HEAD_DIM) + shape[axis + 1:]
    perm = list(range(len(split)))
    perm[axis], perm[axis + 1] = perm[axis + 1], perm[axis]
    return w.reshape(split).transpose(perm).reshape(shape)


def kernel(x_prompt, x_sample, state_conv, cache_k_win, cache_v_win, rel_bias,
           norm_g, w_ff1_gu, w_ff1_down, w_in, conv_w, sinks, w_conv_out,
           w_attn_out, w_out, w_ff2_gu, w_ff2_down):
    batch, seq, _ = x_prompt.shape
    n_dec = x_sample.shape[0]

    bt, ffn_w = _bias_table(rel_bias,
                            _ffn_cast_jobs(w_ff1_gu, w_ff1_down, 0))
    bdec = bt[0].reshape(GROUP, BLOCK, N_KV, 2 * BLOCK)[:, BLOCK - 1, :, BLOCK:]
    bdec = bdec.reshape(N_HEADS, WINDOW)
    bdec = jnp.concatenate([bdec[:, WINDOW - 1:], bdec[:, :WINDOW - 1]], axis=1)

    def keys_minor(c):
        return jnp.transpose(c, (0, 1, 3, 4, 2)).reshape(
            DEPTH, n_dec, KV_W, WINDOW)

    def keys_major(c):
        return jnp.transpose(
            c.reshape(DEPTH, n_dec, N_KV, HEAD_DIM, WINDOW), (0, 1, 4, 2, 3))

    kct = keys_minor(cache_k_win)
    vct = keys_minor(cache_v_win)

    wq = _heads_g_major(w_in[:, :, C_Q:C_K], 2).astype(BF16)

    xp = x_prompt.reshape(batch * seq, D_MODEL)
    xs = x_sample.reshape(n_dec, D_MODEL)
    pc, pk, pv, sc = [], [], [], []
    new_caches = ()
    for l in range(DEPTH):
        sink_col = sinks[l].reshape(N_KV, GROUP).T.reshape(N_HEADS, 1)

        xp, xs, mix_w = _ffn(
            xp, xs, norm_g, ffn_w[0], ffn_w[1], l, 0,
            _mixer_cast_jobs(w_in, w_conv_out, w_attn_out, w_out, l))
        win, wco, wao, wo = mix_w

        xp, c1, k1, v1, ffn_w = _mixer_prompt(
            xp, batch, norm_g, win, wq, conv_w, wco, wao, wo, bt, sinks, l,
            _ffn_cast_jobs(w_ff2_gu, w_ff2_down, l))
        xs, c2, k2, v2 = _mixer_decode(
            xs, state_conv[l].reshape(n_dec, 2 * D_MODEL), kct, vct,
            norm_g, win, wq, conv_w, wco, wao, wo, bdec, sink_col, l,
            new_caches)
        new_caches = (k2, v2)

        next_jobs = (_ffn_cast_jobs(w_ff1_gu, w_ff1_down, l + 1)
                     if l + 1 < DEPTH else [])
        xp, xs, ffn_w = _ffn(xp, xs, norm_g, ffn_w[0], ffn_w[1], l, 4,
                             next_jobs)

        pc.append(c1)
        pk.append(k1.reshape(batch, WINDOW, N_KV, HEAD_DIM))
        pv.append(v1.reshape(batch, WINDOW, N_KV, HEAD_DIM))
        sc.append(c2.reshape(n_dec, 2, D_MODEL))

    return (xp.reshape(batch, seq, D_MODEL), xs.reshape(n_dec, 1, D_MODEL),
            jnp.stack(pc), jnp.stack(pk), jnp.stack(pv),
            jnp.stack(sc), keys_major(new_caches[0]),
            keys_major(new_caches[1]))
```

```python
import functools
from typing import Callable, NamedTuple

import numpy as np
import jax
import jax.numpy as jnp
from jax import lax
from jax.experimental import pallas as pl
from jax.experimental.pallas import tpu as pltpu

F32 = jnp.float32
BF16 = jnp.bfloat16

D_MODEL = 1024
D_FF = 2816
N_HEADS = 16
N_KV = 4
GROUP = N_HEADS // N_KV
HEAD_DIM = 64
KV_W = N_KV * HEAD_DIM
WINDOW = 128
BLOCK = 128
N_BUCKETS = 32
MAX_DISTANCE = 128
RMS_EPS = 1e-6
NEG = -1e30
DEPTH = 2
N_NORMS = 6

C_CB, C_CC, C_CX, C_Q, C_K, C_V, C_GA, C_GB, C_END = (
    0, 1024, 2048, 3072, 4096, 4352, 4608, 5632, 6656)

VMEM_LIMIT_BYTES = 56 * 1024 * 1024

FFN_ROWS = 512
FFN_CHUNK = 256
MIX_ROWS = 256
DEC_SEQS = 8


def _t5_bucket_np(rel):
    n = np.maximum(rel, 0)
    max_exact = N_BUCKETS // 2
    nf = np.maximum(n, 1).astype(np.float32)
    large = max_exact + (
        np.log(nf / max_exact) / np.log(MAX_DISTANCE / max_exact)
        * (N_BUCKETS - max_exact)).astype(np.int32)
    large = np.minimum(large, N_BUCKETS - 1)
    return np.where(n < max_exact, n, large).astype(np.int32)


def _band_tables():
    qi = np.arange(BLOCK)[:, None]
    kj = np.arange(2 * BLOCK)[None, :]
    rel = qi + BLOCK - kj
    bucket = _t5_bucket_np(rel)
    band = (rel >= 0) & (rel < WINDOW)
    mask = np.stack([band, band & (kj >= BLOCK)]).astype(np.float32)
    return bucket, mask


_BUCKET_NP, _MASK_NP = _band_tables()


def _rms(x, g):
    r = lax.rsqrt(jnp.mean(x * x, axis=-1, keepdims=True) + RMS_EPS)
    return (x * r) * g


def _dot(a, b):
    return jnp.dot(a, b, preferred_element_type=F32)


def _dot_t(a, b):
    return lax.dot_general(a, b, (((1,), (1,)), ((), ())),
                           preferred_element_type=F32)


def _resident(shape):
    return pl.BlockSpec(shape, lambda *_: (0,) * len(shape),
                        pipeline_mode=pl.Buffered(1))


def _layer_resident(shape, layer):
    return pl.BlockSpec((None,) + tuple(shape),
                        lambda *_: (layer,) + (0,) * len(shape),
                        pipeline_mode=pl.Buffered(1))


class _CastJob(NamedTuple):
    src: jax.Array
    layer: int
    rows: int
    src_block: Callable = lambda d: d

    @property
    def n_blocks(self):
        return self.src.shape[1] // self.rows


def _cast_specs(jobs, step_of):
    in_specs, out_specs, out_shapes = [], [], []
    for job in jobs:
        cols = job.src.shape[2]

        def dst(*ids, job=job):
            return jnp.minimum(step_of(*ids), job.n_blocks - 1)

        in_specs.append(pl.BlockSpec(
            (None, job.rows, cols),
            lambda *ids, job=job, dst=dst: (job.layer,
                                            job.src_block(dst(*ids)), 0)))
        out_specs.append(pl.BlockSpec(
            (job.rows, cols), lambda *ids, dst=dst: (dst(*ids), 0)))
        out_shapes.append(jax.ShapeDtypeStruct(job.src.shape[1:], BF16))
    return in_specs, out_specs, out_shapes


def _run_casts(step, n_blocks, in_refs, out_refs):
    for nb, src, dst in zip(n_blocks, in_refs, out_refs):
        @pl.when(step < nb)
        def _(src=src, dst=dst):
            dst[...] = src[...].astype(BF16)


def _ffn_cast_jobs(w_gu, w_down, layer):
    return [_CastJob(w_gu, layer, 32), _CastJob(w_down, layer, 128)]


def _mixer_cast_jobs(w_in, w_co, w_ao, w_o, layer):
    to_g_major = lambda d: (d % GROUP) * N_KV + d // GROUP
    return [_CastJob(w_in, layer, 32), _CastJob(w_co, layer, 64),
            _CastJob(w_ao, layer, HEAD_DIM, to_g_major),
            _CastJob(w_o, layer, 64)]


def _bias_kernel(*refs, cast_blocks):
    nj = len(cast_blocks)
    rb_ref, bucket_ref, mask_ref = refs[:3]
    cast_in = refs[3:3 + nj]
    out_ref = refs[3 + nj]
    cast_out = refs[4 + nj:]
    g = pl.program_id(1)
    hk = pl.program_id(2)
    step = (pl.program_id(0) * GROUP + g) * N_KV + hk
    _run_casts(step, cast_blocks, cast_in, cast_out)
    h = hk * GROUP + g
    bucket = bucket_ref[...]

    def body(b, acc):
        return jnp.where(bucket == b, rb_ref[b, h], acc)

    bias = lax.fori_loop(0, N_BUCKETS, body,
                         jnp.zeros((BLOCK, 2 * BLOCK), F32))
    out_ref[...] = jnp.where(mask_ref[...] > 0.0, bias, NEG)


def _bias_table(rel_bias, jobs):
    c_in, c_out, c_shapes = _cast_specs(
        jobs, lambda v, g, hk: (v * GROUP + g) * N_KV + hk)
    outs = pl.pallas_call(
        functools.partial(_bias_kernel,
                          cast_blocks=tuple(j.n_blocks for j in jobs)),
        grid=(2, GROUP, N_KV),
        in_specs=[
            pl.BlockSpec(memory_space=pltpu.SMEM),
            pl.BlockSpec((BLOCK, 2 * BLOCK), lambda v, g, hk: (0, 0)),
            pl.BlockSpec((None, BLOCK, 2 * BLOCK), lambda v, g, hk: (v, 0, 0)),
        ] + c_in,
        out_specs=[pl.BlockSpec((None, BLOCK, 2 * BLOCK),
                                lambda v, g, hk: (v, g, hk))] + c_out,
        out_shape=[jax.ShapeDtypeStruct(
            (2, GROUP * BLOCK, N_KV * 2 * BLOCK), F32)] + c_shapes,
        compiler_params=pltpu.CompilerParams(
            dimension_semantics=("arbitrary",) * 3),
        name="bias_table",
    )(rel_bias, jnp.asarray(_BUCKET_NP), jnp.asarray(_MASK_NP),
      *[j.src for j in jobs])
    return outs[0], outs[1:]


def _ffn_rows(x, g_ref, wgu_ref, wd_ref, act_ref, gi):
    rows = x.shape[0]
    h = _rms(x, g_ref[gi:gi + 1, :]).astype(BF16)
    for c in range(D_FF // FFN_CHUNK):
        lo = c * FFN_CHUNK
        gate = _dot(h, wgu_ref[:, lo:lo + FFN_CHUNK])
        up = _dot(h, wgu_ref[:, D_FF + lo:D_FF + lo + FFN_CHUNK])
        act_ref[0:rows, lo:lo + FFN_CHUNK] = (
            gate * jax.nn.sigmoid(gate) * up).astype(BF16)
    y = _dot(act_ref[0:rows, :], wd_ref[...])
    return x + 0.5 * _rms(y, g_ref[gi + 1:gi + 2, :])


def _ffn_kernel(*refs, gi, prompt_steps, cast_blocks):
    nj = len(cast_blocks)
    xp_ref, xs_ref, g_ref, wgu_ref, wd_ref = refs[:5]
    cast_in = refs[5:5 + nj]
    op_ref, os_ref = refs[5 + nj:7 + nj]
    cast_out = refs[7 + nj:7 + 2 * nj]
    act_ref = refs[7 + 2 * nj]
    i = pl.program_id(0)
    _run_casts(i, cast_blocks, cast_in, cast_out)

    @pl.when(i < prompt_steps)
    def _():
        op_ref[...] = _ffn_rows(xp_ref[...], g_ref, wgu_ref, wd_ref, act_ref,
                                gi)

    @pl.when(i == prompt_steps)
    def _():
        os_ref[...] = _ffn_rows(xs_ref[...], g_ref, wgu_ref, wd_ref, act_ref,
                                gi)


def _ffn(xp, xs, norm_g, w_gu, w_down, layer, gi, jobs):
    m = xp.shape[0]
    n = xs.shape[0]
    tm = FFN_ROWS
    steps = m // tm
    row_block = lambda i: (jnp.minimum(i, steps - 1), 0)
    c_in, c_out, c_shapes = _cast_specs(jobs, lambda i: i)
    outs = pl.pallas_call(
        functools.partial(_ffn_kernel, gi=gi, prompt_steps=steps,
                          cast_blocks=tuple(j.n_blocks for j in jobs)),
        grid=(steps + 1,),
        in_specs=[
            pl.BlockSpec((tm, D_MODEL), row_block),
            _resident((n, D_MODEL)),
            _layer_resident((N_NORMS, D_MODEL), layer),
            _resident((D_MODEL, 2 * D_FF)),
            _resident((D_FF, D_MODEL)),
        ] + c_in,
        out_specs=[
            pl.BlockSpec((tm, D_MODEL), row_block),
            pl.BlockSpec((n, D_MODEL), lambda i: (0, 0)),
        ] + c_out,
        out_shape=[
            jax.ShapeDtypeStruct((m, D_MODEL), F32),
            jax.ShapeDtypeStruct((n, D_MODEL), F32),
        ] + c_shapes,
        scratch_shapes=[pltpu.VMEM((tm, D_FF), BF16)],
        compiler_params=pltpu.CompilerParams(
            dimension_semantics=("arbitrary",),
            vmem_limit_bytes=VMEM_LIMIT_BYTES),
        name="ffn_half",
    )(xp, xs, norm_g, w_gu, w_down, *[j.src for j in jobs])
    return outs[0], outs[1], outs[2:]


def _lane_segment_mask(rows):
    lane = lax.broadcasted_iota(jnp.int32, (rows, KV_W), 1)
    return [(lane >= hk * HEAD_DIM) & (lane < (hk + 1) * HEAD_DIM)
            for hk in range(N_KV)]


def _mixer_prompt_kernel(*refs, layer, rows, cast_blocks):
    nj = len(cast_blocks)
    (x_ref, g_ref, win_ref, wq_ref, cw_ref, wco_ref, wao_ref, wo_ref, bt_ref,
     sinks_ref) = refs[:10]
    cast_in = refs[10:10 + nj]
    y_ref, cst_ref, kout_ref, vout_ref = refs[10 + nj:14 + nj]
    cast_out = refs[14 + nj:14 + 2 * nj]
    q_s, kbuf, vbuf, upad, o_s, p_s = refs[14 + 2 * nj:]
    i = pl.program_id(1)
    steps = pl.num_programs(1)
    last = steps - 1
    nq = rows // BLOCK
    _run_casts(pl.program_id(0) * steps + i, cast_blocks, cast_in, cast_out)

    @pl.when(i == 0)
    def _():
        upad[0:8, :] = jnp.zeros((8, D_MODEL), F32)
        kbuf[0:BLOCK, :] = jnp.zeros((BLOCK, KV_W), BF16)
        vbuf[0:BLOCK, :] = jnp.zeros((BLOCK, KV_W), BF16)

    @pl.when(i > 0)
    def _():
        upad[0:8, :] = upad[rows:rows + 8, :]
        kbuf[0:BLOCK, :] = kbuf[rows:rows + BLOCK, :]
        vbuf[0:BLOCK, :] = vbuf[rows:rows + BLOCK, :]

    x = x_ref[...]
    h = _rms(x, g_ref[2:3, :]).astype(BF16)

    q_s[...] = (_dot(h, wq_ref[...]) * (HEAD_DIM ** -0.5)).astype(BF16)
    k = _dot(h, win_ref[:, C_K:C_V])
    v = _dot(h, win_ref[:, C_V:C_GA])
    kbuf[BLOCK:BLOCK + rows, :] = k.astype(BF16)
    vbuf[BLOCK:BLOCK + rows, :] = v.astype(BF16)

    @pl.when(i == last)
    def _():
        kout_ref[...] = k[rows - BLOCK:rows, :]
        vout_ref[...] = v[rows - BLOCK:rows, :]

    seg = _lane_segment_mask(2 * BLOCK)
    seg_q = _lane_segment_mask(BLOCK)
    zero_kv = jnp.zeros((2 * BLOCK, KV_W), BF16)
    first = (i == 0).astype(jnp.int32)

    for j in range(nq):
        r0 = j * BLOCK
        variant = first if j == 0 else 0
        qj = q_s[r0:r0 + BLOCK, :]
        kw = kbuf[r0:r0 + 2 * BLOCK, :]
        vw = vbuf[r0:r0 + 2 * BLOCK, :]
        qs = jnp.concatenate(
            [qj[:, g * KV_W:(g + 1) * KV_W] for g in range(GROUP)], axis=0)
        kb = jnp.concatenate(
            [jnp.where(seg[hk], kw, zero_kv) for hk in range(N_KV)], axis=0)
        vb = jnp.concatenate(
            [jnp.where(seg[hk], vw, zero_kv) for hk in range(N_KV)], axis=0)
        s_all = _dot_t(qs, kb)
        inv = []
        for g in range(GROUP):
            inv_g = []
            for hk in range(N_KV):
                rs = slice(g * BLOCK, (g + 1) * BLOCK)
                cs = slice(hk * 2 * BLOCK, (hk + 1) * 2 * BLOCK)
                s = s_all[rs, cs] + bt_ref[variant, rs, cs]
                sink = sinks_ref[layer, hk * GROUP + g]
                m = jnp.maximum(jnp.max(s, axis=-1, keepdims=True), sink)
                p = jnp.exp(s - m)
                den = jnp.sum(p, axis=-1, keepdims=True) + jnp.exp(sink - m)
                p_s[j, rs, cs] = p.astype(BF16)
                inv_g.append(1.0 / den)
            inv.append(inv_g)
        o_all = _dot(p_s[j], vb)
        for g in range(GROUP):
            scale = jnp.where(
                seg_q[0], inv[g][0],
                jnp.where(seg_q[1], inv[g][1],
                          jnp.where(seg_q[2], inv[g][2], inv[g][3])))
            o_s[r0:r0 + BLOCK, g * KV_W:(g + 1) * KV_W] = (
                o_all[g * BLOCK:(g + 1) * BLOCK, :] * scale).astype(BF16)

    u = _dot(h, win_ref[:, C_CC:C_CX]) * _dot(h, win_ref[:, C_CX:C_Q])
    upad[8:rows + 8, :] = u

    @pl.when(i == last)
    def _():
        cst_ref[...] = upad[rows + 6:rows + 8, :]

    yc = (cw_ref[0:1, :] * upad[6:rows + 6, :]
          + cw_ref[1:2, :] * upad[7:rows + 7, :]
          + cw_ref[2:3, :] * u)
    a_in = (_dot(h, win_ref[:, C_CB:C_CC]) * yc).astype(BF16)
    a_out = _dot(a_in, wco_ref[...])
    gated_a = jax.nn.sigmoid(_dot(h, win_ref[:, C_GA:C_GB])) * a_out

    att = _dot(o_s[...], wao_ref[...])
    gate_b = jax.nn.sigmoid(_dot(h, win_ref[:, C_GB:C_END]))
    merged = (gated_a + gate_b * att).astype(BF16)
    y_ref[...] = x + _rms(_dot(merged, wo_ref[...]), g_ref[3:4, :])


def _mixer_prompt(x, batch, norm_g, w_in, w_q, conv_w, w_co, w_ao, w_o, bt,
                  sinks, layer, jobs):
    m = x.shape[0]
    seq = m // batch
    rows = MIX_ROWS
    steps = seq // rows
    c_in, c_out, c_shapes = _cast_specs(jobs, lambda b, i: b * steps + i)
    kern = functools.partial(_mixer_prompt_kernel, layer=layer, rows=rows,
                             cast_blocks=tuple(j.n_blocks for j in jobs))
    outs = pl.pallas_call(
        kern,
        grid=(batch, steps),
        in_specs=[
            pl.BlockSpec((rows, D_MODEL), lambda b, i: (b * steps + i, 0)),
            _layer_resident((N_NORMS, D_MODEL), layer),
            _resident((D_MODEL, C_END)),
            _layer_resident((D_MODEL, D_MODEL), layer),
            _layer_resident((3, D_MODEL), layer),
            _resident((D_MODEL, D_MODEL)),
            _resident((D_MODEL, D_MODEL)),
            _resident((D_MODEL, D_MODEL)),
            _resident((2, GROUP * BLOCK, N_KV * 2 * BLOCK)),
            pl.BlockSpec(memory_space=pltpu.SMEM),
        ] + c_in,
        out_specs=[
            pl.BlockSpec((rows, D_MODEL), lambda b, i: (b * steps + i, 0)),
            pl.BlockSpec((None, 2, D_MODEL), lambda b, i: (b, 0, 0)),
            pl.BlockSpec((None, BLOCK, KV_W), lambda b, i: (b, 0, 0)),
            pl.BlockSpec((None, BLOCK, KV_W), lambda b, i: (b, 0, 0)),
        ] + c_out,
        out_shape=[
            jax.ShapeDtypeStruct((m, D_MODEL), F32),
            jax.ShapeDtypeStruct((batch, 2, D_MODEL), F32),
            jax.ShapeDtypeStruct((batch, BLOCK, KV_W), F32),
            jax.ShapeDtypeStruct((batch, BLOCK, KV_W), F32),
        ] + c_shapes,
        scratch_shapes=[
            pltpu.VMEM((rows, D_MODEL), BF16),
            pltpu.VMEM((rows + BLOCK, KV_W), BF16),
            pltpu.VMEM((rows + BLOCK, KV_W), BF16),
            pltpu.VMEM((rows + 8, D_MODEL), F32),
            pltpu.VMEM((rows, D_MODEL), BF16),
            pltpu.VMEM((rows // BLOCK, GROUP * BLOCK, N_KV * 2 * BLOCK),
                       BF16),
        ],
        compiler_params=pltpu.CompilerParams(
            dimension_semantics=("arbitrary", "arbitrary"),
            vmem_limit_bytes=VMEM_LIMIT_BYTES),
        name="mixer_prompt",
    )(x, norm_g, w_in, w_q, conv_w, w_co, w_ao, w_o, bt, sinks,
      *[j.src for j in jobs])
    return outs[0], outs[1], outs[2], outs[3], outs[4:]


def _mixer_decode_kernel(*refs, n_alias):
    (x_ref, st_ref, kc_ref, vc_ref, g_ref, win_ref, wq_ref, cw_ref, wco_ref,
     wao_ref, wo_ref, bdec_ref, sink_ref) = refs[:13]
    (y_ref, nst_ref, kout_ref, vout_ref, h_s, q_s, knew_s, vnew_s, knewt_s,
     vnewt_s, ag_s, o_s) = refs[13 + n_alias:]
    i = pl.program_id(0)
    last = pl.num_programs(0) - 1

    @pl.when(i == 0)
    def _():
        x = x_ref[...]
        h = _rms(x, g_ref[2:3, :]).astype(BF16)
        h_s[...] = h
        u = _dot(h, win_ref[:, C_CC:C_CX]) * _dot(h, win_ref[:, C_CX:C_Q])
        st0 = st_ref[:, 0:D_MODEL]
        st1 = st_ref[:, D_MODEL:2 * D_MODEL]
        yc = cw_ref[0:1, :] * st0 + cw_ref[1:2, :] * st1 + cw_ref[2:3, :] * u
        nst_ref[:, 0:D_MODEL] = st1
        nst_ref[:, D_MODEL:2 * D_MODEL] = u
        a_in = (_dot(h, win_ref[:, C_CB:C_CC]) * yc).astype(BF16)
        a_out = _dot(a_in, wco_ref[...])
        ag_s[...] = jax.nn.sigmoid(_dot(h, win_ref[:, C_GA:C_GB])) * a_out
        q_s[...] = _dot(h, wq_ref[...]) * (HEAD_DIM ** -0.5)
        knew = _dot(h, win_ref[:, C_K:C_V])
        vnew = _dot(h, win_ref[:, C_V:C_GA])
        knew_s[...] = knew
        vnew_s[...] = vnew
        knewt_s[...] = knew.T
        vnewt_s[...] = vnew.T

    row = lax.broadcasted_iota(jnp.int32, (N_HEADS, KV_W), 0)
    lane = lax.broadcasted_iota(jnp.int32, (N_HEADS, KV_W), 1)
    row_g = row // N_KV
    seg = (lane // HEAD_DIM) == (row % N_KV)
    key0 = lax.broadcasted_iota(jnp.int32, (N_HEADS, WINDOW), 1) == 0
    newest = lax.broadcasted_iota(jnp.int32, (KV_W, WINDOW), 1) == WINDOW - 1
    bias = bdec_ref[...]
    sink = sink_ref[...]

    for b in range(DEC_SEQS):
        r = i * DEC_SEQS + b
        kt = kc_ref[b]
        vt = vc_ref[b]
        to_last = (WINDOW - 1) - r
        kout_ref[b] = jnp.where(
            newest, pltpu.roll(knewt_s[...], to_last, axis=1),
            pltpu.roll(kt, WINDOW - 1, axis=1))
        vout_ref[b] = jnp.where(
            newest, pltpu.roll(vnewt_s[...], to_last, axis=1),
            pltpu.roll(vt, WINDOW - 1, axis=1))
        q_row = q_s[pl.ds(r, 1), :]
        pieces = [jnp.broadcast_to(q_row[:, g * KV_W:(g + 1) * KV_W],
                                   (N_HEADS, KV_W)) for g in range(GROUP)]
        q_rows = jnp.where(row_g == 0, pieces[0],
                           jnp.where(row_g == 1, pieces[1],
                                     jnp.where(row_g == 2, pieces[2],
                                               pieces[3])))
        qm = jnp.where(seg, q_rows, 0.0).astype(BF16)
        k_row = knew_s[pl.ds(r, 1), :].astype(BF16).astype(F32)
        v_row = vnew_s[pl.ds(r, 1), :].astype(BF16).astype(F32)
        s_new = jnp.sum(qm.astype(F32) * k_row, axis=-1, keepdims=True)
        s = jnp.where(key0, s_new, _dot(qm, kt.astype(BF16))) + bias
        m = jnp.maximum(jnp.max(s, axis=-1, keepdims=True), sink)
        p = jnp.exp(s - m)
        den = jnp.sum(p, axis=-1, keepdims=True) + jnp.exp(sink - m)
        p16 = p.astype(BF16)
        p_new = p16[:, 0:1].astype(F32)
        p_old = jnp.where(key0, jnp.zeros_like(p16), p16)
        o = (_dot_t(p_old, vt.astype(BF16)) + p_new * v_row) / den
        om = jnp.where(seg, o, 0.0)
        for g in range(GROUP):
            o_s[pl.ds(r, 1), g * KV_W:(g + 1) * KV_W] = jnp.sum(
                om[g * N_KV:(g + 1) * N_KV, :], axis=0, keepdims=True)

    @pl.when(i == last)
    def _():
        att = _dot(o_s[...].astype(BF16), wao_ref[...])
        gate_b = jax.nn.sigmoid(_dot(h_s[...], win_ref[:, C_GB:C_END]))
        merged = (ag_s[...] + gate_b * att).astype(BF16)
        y_ref[...] = x_ref[...] + _rms(_dot(merged, wo_ref[...]),
                                       g_ref[3:4, :])


def _mixer_decode(x, st, kc, vc, norm_g, w_in, w_q, conv_w, w_co, w_ao, w_o,
                  bdec, sink_col, layer, prev_caches):
    n = x.shape[0]
    sb = DEC_SEQS
    cache_spec = pl.BlockSpec((None, sb, KV_W, WINDOW),
                              lambda i: (layer, i, 0, 0))
    n_alias = len(prev_caches)
    n_in = 13
    return pl.pallas_call(
        functools.partial(_mixer_decode_kernel, n_alias=n_alias),
        grid=(n // sb,),
        in_specs=[
            _resident((n, D_MODEL)),
            _resident((n, 2 * D_MODEL)),
            cache_spec,
            cache_spec,
            _layer_resident((N_NORMS, D_MODEL), layer),
            _resident((D_MODEL, C_END)),
            _layer_resident((D_MODEL, D_MODEL), layer),
            _layer_resident((3, D_MODEL), layer),
            _resident((D_MODEL, D_MODEL)),
            _resident((D_MODEL, D_MODEL)),
            _resident((D_MODEL, D_MODEL)),
            _resident((N_HEADS, WINDOW)),
            _resident((N_HEADS, 1)),
        ] + [pl.BlockSpec(memory_space=pl.ANY)] * n_alias,
        out_specs=[
            pl.BlockSpec((n, D_MODEL), lambda i: (0, 0)),
            pl.BlockSpec((n, 2 * D_MODEL), lambda i: (0, 0)),
            cache_spec,
            cache_spec,
        ],
        out_shape=[
            jax.ShapeDtypeStruct((n, D_MODEL), F32),
            jax.ShapeDtypeStruct((n, 2 * D_MODEL), F32),
            jax.ShapeDtypeStruct(kc.shape, F32),
            jax.ShapeDtypeStruct(vc.shape, F32),
        ],
        input_output_aliases={n_in + a: 2 + a for a in range(n_alias)},
        scratch_shapes=[
            pltpu.VMEM((n, D_MODEL), BF16),
            pltpu.VMEM((n, D_MODEL), F32),
            pltpu.VMEM((n, KV_W), F32),
            pltpu.VMEM((n, KV_W), F32),
            pltpu.VMEM((KV_W, n), F32),
            pltpu.VMEM((KV_W, n), F32),
            pltpu.VMEM((n, D_MODEL), F32),
            pltpu.VMEM((n, D_MODEL), F32),
        ],
        compiler_params=pltpu.CompilerParams(
            dimension_semantics=("arbitrary",),
            vmem_limit_bytes=VMEM_LIMIT_BYTES),
        name="mixer_decode",
    )(x, st, kc, vc, norm_g, w_in, w_q, conv_w, w_co, w_ao, w_o, bdec,
      sink_col, *prev_caches)


def _heads_g_major(w, axis):
    shape = w.shape
    split = shape[:axis] + (N_KV, GROUP, HEAD_DIM) + shape[axis + 1:]
    perm = list(range(len(split)))
    perm[axis], perm[axis + 1] = perm[axis + 1], perm[axis]
    return w.reshape(split).transpose(perm).reshape(shape)


def kernel(x_prompt, x_sample, state_conv, cache_k_win, cache_v_win, rel_bias,
           norm_g, w_ff1_gu, w_ff1_down, w_in, conv_w, sinks, w_conv_out,
           w_attn_out, w_out, w_ff2_gu, w_ff2_down):
    batch, seq, _ = x_prompt.shape
    n_dec = x_sample.shape[0]

    bt, ffn_w = _bias_table(rel_bias,
                            _ffn_cast_jobs(w_ff1_gu, w_ff1_down, 0))
    bdec = bt[0].reshape(GROUP, BLOCK, N_KV, 2 * BLOCK)[:, BLOCK - 1, :, BLOCK:]
    bdec = bdec.reshape(N_HEADS, WINDOW)
    bdec = jnp.concatenate([bdec[:, WINDOW - 1:], bdec[:, :WINDOW - 1]], axis=1)

    def keys_minor(c):
        return jnp.transpose(c, (0, 1, 3, 4, 2)).reshape(
            DEPTH, n_dec, KV_W, WINDOW)

    def keys_major(c):
        return jnp.transpose(
            c.reshape(DEPTH, n_dec, N_KV, HEAD_DIM, WINDOW), (0, 1, 4, 2, 3))

    kct = keys_minor(cache_k_win)
    vct = keys_minor(cache_v_win)

    wq = _heads_g_major(w_in[:, :, C_Q:C_K], 2).astype(BF16)

    xp = x_prompt.reshape(batch * seq, D_MODEL)
    xs = x_sample.reshape(n_dec, D_MODEL)
    pc, pk, pv, sc = [], [], [], []
    new_caches = ()
    for l in range(DEPTH):
        sink_col = sinks[l].reshape(N_KV, GROUP).T.reshape(N_HEADS, 1)

        xp, xs, mix_w = _ffn(
            xp, xs, norm_g, ffn_w[0], ffn_w[1], l, 0,
            _mixer_cast_jobs(w_in, w_conv_out, w_attn_out, w_out, l))
        win, wco, wao, wo = mix_w

        xp, c1, k1, v1, ffn_w = _mixer_prompt(
            xp, batch, norm_g, win, wq, conv_w, wco, wao, wo, bt, sinks, l,
            _ffn_cast_jobs(w_ff2_gu, w_ff2_down, l))
        xs, c2, k2, v2 = _mixer_decode(
            xs, state_conv[l].reshape(n_dec, 2 * D_MODEL), kct, vct,
            norm_g, win, wq, conv_w, wco, wao, wo, bdec, sink_col, l,
            new_caches)
        new_caches = (k2, v2)

        next_jobs = (_ffn_cast_jobs(w_ff1_gu, w_ff1_down, l + 1)
                     if l + 1 < DEPTH else [])
        xp, xs, ffn_w = _ffn(xp, xs, norm_g, ffn_w[0], ffn_w[1], l, 4,
                             next_jobs)

        pc.append(c1)
        pk.append(k1.reshape(batch, WINDOW, N_KV, HEAD_DIM))
        pv.append(v1.reshape(batch, WINDOW, N_KV, HEAD_DIM))
        sc.append(c2.reshape(n_dec, 2, D_MODEL))

    return (xp.reshape(batch, seq, D_MODEL), xs.reshape(n_dec, 1, D_MODEL),
            jnp.stack(pc), jnp.stack(pk), jnp.stack(pv),
            jnp.stack(sc), keys_major(new_caches[0]),
            keys_major(new_caches[1]))
```

```python
import functools
from typing import Callable, NamedTuple

import numpy as np
import jax
import jax.numpy as jnp
from jax import lax
from jax.experimental import pallas as pl
from jax.experimental.pallas import tpu as pltpu

F32 = jnp.float32
BF16 = jnp.bfloat16

D_MODEL = 1024
D_FF = 2816
N_HEADS = 16
N_KV = 4
GROUP = N_HEADS // N_KV
HEAD_DIM = 64
KV_W = N_KV * HEAD_DIM
WINDOW = 128
BLOCK = 128
N_BUCKETS = 32
MAX_DISTANCE = 128
RMS_EPS = 1e-6
NEG = -1e30
LOG2E = 1.4426950408889634
DEPTH = 2
N_NORMS = 6

C_CB, C_CC, C_CX, C_Q, C_K, C_V, C_GA, C_GB, C_END = (
    0, 1024, 2048, 3072, 4096, 4352, 4608, 5632, 6656)

VMEM_LIMIT_BYTES = 56 * 1024 * 1024

FFN_ROWS = 512
FFN_CHUNK = 256
MIX_ROWS = 512
DEC_SEQS = 8


def _t5_bucket_np(rel):
    n = np.maximum(rel, 0)
    max_exact = N_BUCKETS // 2
    nf = np.maximum(n, 1).astype(np.float32)
    large = max_exact + (
        np.log(nf / max_exact) / np.log(MAX_DISTANCE / max_exact)
        * (N_BUCKETS - max_exact)).astype(np.int32)
    large = np.minimum(large, N_BUCKETS - 1)
    return np.where(n < max_exact, n, large).astype(np.int32)


def _band_tables():
    qi = np.arange(BLOCK)[:, None]
    kj = np.arange(BLOCK)[None, :]
    rel = np.where(kj > qi, qi + BLOCK - kj, qi - kj)
    bucket = _t5_bucket_np(rel)
    mask = np.stack([np.ones_like(rel, bool), kj <= qi]).astype(np.float32)
    return bucket, mask


_BUCKET_NP, _MASK_NP = _band_tables()


def _rms(x, g):
    r = lax.rsqrt(jnp.mean(x * x, axis=-1, keepdims=True) + RMS_EPS)
    return (x * r) * g


def _dot(a, b):
    return jnp.dot(a, b, preferred_element_type=F32)


def _dot_t(a, b):
    return lax.dot_general(a, b, (((1,), (1,)), ((), ())),
                           preferred_element_type=F32)


def _resident(shape):
    return pl.BlockSpec(shape, lambda *_: (0,) * len(shape),
                        pipeline_mode=pl.Buffered(1))


def _layer_resident(shape, layer):
    return pl.BlockSpec((None,) + tuple(shape),
                        lambda *_: (layer,) + (0,) * len(shape),
                        pipeline_mode=pl.Buffered(1))


class _CastJob(NamedTuple):
    src: jax.Array
    layer: int
    rows: int
    src_block: Callable = lambda d: d

    @property
    def n_blocks(self):
        return self.src.shape[1] // self.rows


def _cast_specs(jobs, step_of):
    in_specs, out_specs, out_shapes = [], [], []
    for job in jobs:
        cols = job.src.shape[2]

        def dst(*ids, job=job):
            return jnp.minimum(step_of(*ids), job.n_blocks - 1)

        in_specs.append(pl.BlockSpec(
            (None, job.rows, cols),
            lambda *ids, job=job, dst=dst: (job.layer,
                                            job.src_block(dst(*ids)), 0)))
        out_specs.append(pl.BlockSpec(
            (job.rows, cols), lambda *ids, dst=dst: (dst(*ids), 0)))
        out_shapes.append(jax.ShapeDtypeStruct(job.src.shape[1:], BF16))
    return in_specs, out_specs, out_shapes


def _run_casts(step, n_blocks, in_refs, out_refs):
    for nb, src, dst in zip(n_blocks, in_refs, out_refs):
        @pl.when(step < nb)
        def _(src=src, dst=dst):
            dst[...] = src[...].astype(BF16)


def _ffn_cast_jobs(w_gu, w_down, layer):
    return [_CastJob(w_gu, layer, 32), _CastJob(w_down, layer, 128)]


def _mixer_cast_jobs(w_in, w_co, w_ao, w_o, layer):
    to_g_major = lambda d: (d % GROUP) * N_KV + d // GROUP
    return [_CastJob(w_in, layer, 32), _CastJob(w_co, layer, 64),
            _CastJob(w_ao, layer, HEAD_DIM, to_g_major),
            _CastJob(w_o, layer, 64)]


def _bias_kernel(*refs, cast_blocks):
    nj = len(cast_blocks)
    rb_ref, bucket_ref, mask_ref = refs[:3]
    cast_in = refs[3:3 + nj]
    plain_ref, exp2_ref = refs[3 + nj:5 + nj]
    cast_out = refs[5 + nj:]
    g = pl.program_id(1)
    hk = pl.program_id(2)
    step = (pl.program_id(0) * GROUP + g) * N_KV + hk
    _run_casts(step, cast_blocks, cast_in, cast_out)
    h = hk * GROUP + g
    bucket = bucket_ref[...]

    def body(b, acc):
        return jnp.where(bucket == b, rb_ref[b, h], acc)

    bias = lax.fori_loop(0, N_BUCKETS, body, jnp.zeros((BLOCK, BLOCK), F32))
    valid = mask_ref[...] > 0.0
    plain_ref[...] = jnp.where(valid, bias, NEG)
    exp2_ref[...] = jnp.where(valid, bias * LOG2E, NEG)


def _bias_table(rel_bias, jobs):
    c_in, c_out, c_shapes = _cast_specs(
        jobs, lambda v, g, hk: (v * GROUP + g) * N_KV + hk)
    table_spec = pl.BlockSpec((None, BLOCK, BLOCK), lambda v, g, hk: (v, g, hk))
    table_shape = jax.ShapeDtypeStruct((2, GROUP * BLOCK, N_KV * BLOCK), F32)
    outs = pl.pallas_call(
        functools.partial(_bias_kernel,
                          cast_blocks=tuple(j.n_blocks for j in jobs)),
        grid=(2, GROUP, N_KV),
        in_specs=[
            pl.BlockSpec(memory_space=pltpu.SMEM),
            pl.BlockSpec((BLOCK, BLOCK), lambda v, g, hk: (0, 0)),
            pl.BlockSpec((None, BLOCK, BLOCK), lambda v, g, hk: (v, 0, 0)),
        ] + c_in,
        out_specs=[table_spec, table_spec] + c_out,
        out_shape=[table_shape, table_shape] + c_shapes,
        compiler_params=pltpu.CompilerParams(
            dimension_semantics=("arbitrary",) * 3),
        name="bias_table",
    )(rel_bias, jnp.asarray(_BUCKET_NP), jnp.asarray(_MASK_NP),
      *[j.src for j in jobs])
    return outs[0], outs[1], outs[2:]


def _ffn_rows(x, g_ref, wgu_ref, wd_ref, act_ref, gi):
    rows = x.shape[0]
    h = _rms(x, g_ref[gi:gi + 1, :]).astype(BF16)
    for c in range(D_FF // FFN_CHUNK):
        lo = c * FFN_CHUNK
        gate = _dot(h, wgu_ref[:, lo:lo + FFN_CHUNK])
        up = _dot(h, wgu_ref[:, D_FF + lo:D_FF + lo + FFN_CHUNK])
        act_ref[0:rows, lo:lo + FFN_CHUNK] = (
            gate * jax.nn.sigmoid(gate) * up).astype(BF16)
    y = _dot(act_ref[0:rows, :], wd_ref[...])
    return x + 0.5 * _rms(y, g_ref[gi + 1:gi + 2, :])


def _ffn_kernel(*refs, gi, prompt_steps, cast_blocks):
    nj = len(cast_blocks)
    xp_ref, xs_ref, g_ref, wgu_ref, wd_ref = refs[:5]
    cast_in = refs[5:5 + nj]
    op_ref, os_ref = refs[5 + nj:7 + nj]
    cast_out = refs[7 + nj:7 + 2 * nj]
    act_ref = refs[7 + 2 * nj]
    i = pl.program_id(0)
    _run_casts(i, cast_blocks, cast_in, cast_out)

    @pl.when(i < prompt_steps)
    def _():
        op_ref[...] = _ffn_rows(xp_ref[...], g_ref, wgu_ref, wd_ref, act_ref,
                                gi)

    @pl.when(i == prompt_steps)
    def _():
        os_ref[...] = _ffn_rows(xs_ref[...], g_ref, wgu_ref, wd_ref, act_ref,
                                gi)


def _ffn(xp, xs, norm_g, w_gu, w_down, layer, gi, jobs):
    m = xp.shape[0]
    n = xs.shape[0]
    tm = FFN_ROWS
    steps = m // tm
    row_block = lambda i: (jnp.minimum(i, steps - 1), 0)
    c_in, c_out, c_shapes = _cast_specs(jobs, lambda i: i)
    outs = pl.pallas_call(
        functools.partial(_ffn_kernel, gi=gi, prompt_steps=steps,
                          cast_blocks=tuple(j.n_blocks for j in jobs)),
        grid=(steps + 1,),
        in_specs=[
            pl.BlockSpec((tm, D_MODEL), row_block),
            _resident((n, D_MODEL)),
            _layer_resident((N_NORMS, D_MODEL), layer),
            _resident((D_MODEL, 2 * D_FF)),
            _resident((D_FF, D_MODEL)),
        ] + c_in,
        out_specs=[
            pl.BlockSpec((tm, D_MODEL), row_block),
            pl.BlockSpec((n, D_MODEL), lambda i: (0, 0)),
        ] + c_out,
        out_shape=[
            jax.ShapeDtypeStruct((m, D_MODEL), F32),
            jax.ShapeDtypeStruct((n, D_MODEL), F32),
        ] + c_shapes,
        scratch_shapes=[pltpu.VMEM((tm, D_FF), BF16)],
        compiler_params=pltpu.CompilerParams(
            dimension_semantics=("arbitrary",),
            vmem_limit_bytes=VMEM_LIMIT_BYTES),
        name="ffn_half",
    )(xp, xs, norm_g, w_gu, w_down, *[j.src for j in jobs])
    return outs[0], outs[1], outs[2:]


def _lane_segment_mask(rows):
    lane = lax.broadcasted_iota(jnp.int32, (rows, KV_W), 1)
    return [(lane >= hk * HEAD_DIM) & (lane < (hk + 1) * HEAD_DIM)
            for hk in range(N_KV)]


def _mixer_prompt_kernel(*refs, layer, rows, cast_blocks):
    nj = len(cast_blocks)
    (x_ref, g_ref, win_ref, wq_ref, cw_ref, wco_ref, wao_ref, wo_ref, bt_ref,
     sinks_ref) = refs[:10]
    cast_in = refs[10:10 + nj]
    y_ref, cst_ref, kout_ref, vout_ref = refs[10 + nj:14 + nj]
    cast_out = refs[14 + nj:14 + 2 * nj]
    q_s, kbuf, vbuf, upad, o_s, p_s = refs[14 + 2 * nj:]
    i = pl.program_id(1)
    steps = pl.num_programs(1)
    last = steps - 1
    nq = rows // BLOCK
    _run_casts(pl.program_id(0) * steps + i, cast_blocks, cast_in, cast_out)

    @pl.when(i == 0)
    def _():
        upad[0:8, :] = jnp.zeros((8, D_MODEL), F32)
        kbuf[:, 0:BLOCK, :] = jnp.zeros((N_KV, BLOCK, KV_W), BF16)
        vbuf[:, 0:BLOCK, :] = jnp.zeros((N_KV, BLOCK, KV_W), BF16)

    @pl.when(i > 0)
    def _():
        upad[0:8, :] = upad[rows:rows + 8, :]
        kbuf[:, 0:BLOCK, :] = kbuf[:, rows:rows + BLOCK, :]
        vbuf[:, 0:BLOCK, :] = vbuf[:, rows:rows + BLOCK, :]

    x = x_ref[...]
    h = _rms(x, g_ref[2:3, :]).astype(BF16)

    q_s[...] = (_dot(h, wq_ref[...])
                * (HEAD_DIM ** -0.5 * LOG2E)).astype(BF16)
    k = _dot(h, win_ref[:, C_K:C_V])
    v = _dot(h, win_ref[:, C_V:C_GA])
    k16 = k.astype(BF16)
    v16 = v.astype(BF16)
    seg = _lane_segment_mask(rows)
    zero_kv = jnp.zeros((rows, KV_W), BF16)
    for hk in range(N_KV):
        kbuf[hk, BLOCK:BLOCK + rows, :] = jnp.where(seg[hk], k16, zero_kv)
        vbuf[hk, BLOCK:BLOCK + rows, :] = jnp.where(seg[hk], v16, zero_kv)

    kout_ref[...] = k[rows - BLOCK:rows, :]
    vout_ref[...] = v[rows - BLOCK:rows, :]

    q_idx = lax.broadcasted_iota(jnp.int32, (BLOCK, BLOCK), 0)
    j_idx = lax.broadcasted_iota(jnp.int32, (BLOCK, BLOCK), 1)
    from_prev = j_idx > q_idx
    prev16 = jnp.where(from_prev, 1.0, 0.0).astype(BF16)
    own16 = jnp.where(from_prev, 0.0, 1.0).astype(BF16)
    first = (i == 0).astype(jnp.int32)
    seg_q = _lane_segment_mask(BLOCK)

    for j in range(nq):
        r0 = j * BLOCK
        variant = first if j == 0 else 0
        qj = q_s[r0:r0 + BLOCK, :]
        kb = jnp.concatenate(
            [kbuf[hk, r0:r0 + 2 * BLOCK, :] for hk in range(N_KV)], axis=0)
        vb = jnp.concatenate(
            [vbuf[hk, r0:r0 + 2 * BLOCK, :] for hk in range(N_KV)], axis=0)
        qs = jnp.concatenate(
            [qj[:, g * KV_W:(g + 1) * KV_W] for g in range(GROUP)], axis=0)
        s_all = _dot_t(qs, kb)
        inv = []
        for g in range(GROUP):
            inv_g = []
            for hk in range(N_KV):
                rs = slice(g * BLOCK, (g + 1) * BLOCK)
                c0 = hk * 2 * BLOCK
                s = jnp.where(from_prev, s_all[rs, c0:c0 + BLOCK],
                              s_all[rs, c0 + BLOCK:c0 + 2 * BLOCK])
                s = s + bt_ref[variant, rs, hk * BLOCK:(hk + 1) * BLOCK]
                sink = sinks_ref[layer, hk * GROUP + g] * LOG2E
                m = jnp.maximum(jnp.max(s, axis=-1, keepdims=True), sink)
                p = jnp.exp2(s - m)
                p16 = p.astype(BF16)
                p_s[j, rs, c0:c0 + BLOCK] = p16 * prev16
                p_s[j, rs, c0 + BLOCK:c0 + 2 * BLOCK] = p16 * own16
                den = jnp.sum(p, axis=-1, keepdims=True) + jnp.exp2(sink - m)
                inv_g.append(1.0 / den)
            inv.append(inv_g)
        o_all = _dot(p_s[j], vb)
        for g in range(GROUP):
            scale = jnp.where(
                seg_q[0], inv[g][0],
                jnp.where(seg_q[1], inv[g][1],
                          jnp.where(seg_q[2], inv[g][2], inv[g][3])))
            o_s[r0:r0 + BLOCK, g * KV_W:(g + 1) * KV_W] = (
                o_all[g * BLOCK:(g + 1) * BLOCK, :] * scale).astype(BF16)

    u = _dot(h, win_ref[:, C_CC:C_CX]) * _dot(h, win_ref[:, C_CX:C_Q])
    upad[8:rows + 8, :] = u

    cst_ref[...] = u[rows - 2:rows, :]
    yc =(cw_ref[0:1, :] * upad[6:rows + 6, :]
          + cw_ref[1:2, :] * upad[7:rows + 7, :]
          + cw_ref[2:3, :] * u)
    a_in = (_dot(h, win_ref[:, C_CB:C_CC]) * yc).astype(BF16)
    a_out = _dot(a_in, wco_ref[...])
    gated_a = jax.nn.sigmoid(_dot(h, win_ref[:, C_GA:C_GB])) * a_out

    att = _dot(o_s[...], wao_ref[...])
    gate_b = jax.nn.sigmoid(_dot(h, win_ref[:, C_GB:C_END]))
    merged = (gated_a + gate_b * att).astype(BF16)
    y_ref[...] = x + _rms(_dot(merged, wo_ref[...]), g_ref[3:4, :])


def _mixer_prompt(x, batch, norm_g, w_in, w_q, conv_w, w_co, w_ao, w_o, bt,
                  sinks, layer, jobs):
    m = x.shape[0]
    seq = m // batch
    rows = MIX_ROWS
    steps = seq // rows
    c_in, c_out, c_shapes = _cast_specs(jobs, lambda b, i: b * steps + i)
    kern = functools.partial(_mixer_prompt_kernel, layer=layer, rows=rows,
                             cast_blocks=tuple(j.n_blocks for j in jobs))
    outs = pl.pallas_call(
        kern,
        grid=(batch, steps),
        in_specs=[
            pl.BlockSpec((rows, D_MODEL), lambda b, i: (b * steps + i, 0)),
            _layer_resident((N_NORMS, D_MODEL), layer),
            _resident((D_MODEL, C_END)),
            _layer_resident((D_MODEL, D_MODEL), layer),
            _layer_resident((3, D_MODEL), layer),
            _resident((D_MODEL, D_MODEL)),
            _resident((D_MODEL, D_MODEL)),
            _resident((D_MODEL, D_MODEL)),
            _resident((2, GROUP * BLOCK, N_KV * BLOCK)),
            pl.BlockSpec(memory_space=pltpu.SMEM),
        ] + c_in,
        out_specs=[
            pl.BlockSpec((rows, D_MODEL), lambda b, i: (b * steps + i, 0)),
            pl.BlockSpec((None, 2, D_MODEL), lambda b, i: (b, 0, 0)),
            pl.BlockSpec((None, BLOCK, KV_W), lambda b, i: (b, 0, 0)),
            pl.BlockSpec((None, BLOCK, KV_W), lambda b, i: (b, 0, 0)),
        ] + c_out,
        out_shape=[
            jax.ShapeDtypeStruct((m, D_MODEL), F32),
            jax.ShapeDtypeStruct((batch, 2, D_MODEL), F32),
            jax.ShapeDtypeStruct((batch, BLOCK, KV_W), F32),
            jax.ShapeDtypeStruct((batch, BLOCK, KV_W), F32),
        ] + c_shapes,
        scratch_shapes=[
            pltpu.VMEM((rows, D_MODEL), BF16),
            pltpu.VMEM((N_KV, rows + BLOCK, KV_W), BF16),
            pltpu.VMEM((N_KV, rows + BLOCK, KV_W), BF16),
            pltpu.VMEM((rows + 8, D_MODEL), F32),
            pltpu.VMEM((rows, D_MODEL), BF16),
            pltpu.VMEM((rows // BLOCK, GROUP * BLOCK, N_KV * 2 * BLOCK),
                       BF16),
        ],
        compiler_params=pltpu.CompilerParams(
            dimension_semantics=("arbitrary", "arbitrary"),
            vmem_limit_bytes=VMEM_LIMIT_BYTES,
        ),
        name="mixer_prompt",
    )(x, norm_g, w_in, w_q, conv_w, w_co, w_ao, w_o, bt, sinks,
      *[j.src for j in jobs])
    return outs[0], outs[1], outs[2], outs[3], outs[4:]


def _mixer_decode_kernel(*refs, n_alias):
    (x_ref, st_ref, kc_ref, vc_ref, g_ref, win_ref, wq_ref, cw_ref, wco_ref,
     wao_ref, wo_ref, bdec_ref, sink_ref) = refs[:13]
    (y_ref, nst_ref, kout_ref, vout_ref, h_s, q_s, knew_s, vnew_s, knewt_s,
     vnewt_s, ag_s, o_s) = refs[13 + n_alias:]
    i = pl.program_id(0)
    last = pl.num_programs(0) - 1

    @pl.when(i == 0)
    def _():
        x = x_ref[...]
        h = _rms(x, g_ref[2:3, :]).astype(BF16)
        h_s[...] = h
        u = _dot(h, win_ref[:, C_CC:C_CX]) * _dot(h, win_ref[:, C_CX:C_Q])
        st0 = st_ref[:, 0:D_MODEL]
        st1 = st_ref[:, D_MODEL:2 * D_MODEL]
        yc = cw_ref[0:1, :] * st0 + cw_ref[1:2, :] * st1 + cw_ref[2:3, :] * u
        nst_ref[:, 0:D_MODEL] = st1
        nst_ref[:, D_MODEL:2 * D_MODEL] = u
        a_in = (_dot(h, win_ref[:, C_CB:C_CC]) * yc).astype(BF16)
        a_out = _dot(a_in, wco_ref[...])
        ag_s[...] = jax.nn.sigmoid(_dot(h, win_ref[:, C_GA:C_GB])) * a_out
        q_s[...] = _dot(h, wq_ref[...]) * (HEAD_DIM ** -0.5)
        knew = _dot(h, win_ref[:, C_K:C_V])
        vnew = _dot(h, win_ref[:, C_V:C_GA])
        knew_s[...] = knew
        vnew_s[...] = vnew
        knewt_s[...] = knew.T
        vnewt_s[...] = vnew.T

    row = lax.broadcasted_iota(jnp.int32, (N_HEADS, KV_W), 0)
    lane = lax.broadcasted_iota(jnp.int32, (N_HEADS, KV_W), 1)
    row_g = row // N_KV
    seg = (lane // HEAD_DIM) == (row % N_KV)
    key0 = lax.broadcasted_iota(jnp.int32, (N_HEADS, WINDOW), 1) == 0
    newest = lax.broadcasted_iota(jnp.int32, (KV_W, WINDOW), 1) == WINDOW - 1
    bias = bdec_ref[...]
    sink = sink_ref[...]

    for b in range(DEC_SEQS):
        r = i * DEC_SEQS + b
        kt = kc_ref[b]
        vt = vc_ref[b]
        to_last = (WINDOW - 1) - r
        kout_ref[b] = jnp.where(
            newest, pltpu.roll(knewt_s[...], to_last, axis=1),
            pltpu.roll(kt, WINDOW - 1, axis=1))
        vout_ref[b] = jnp.where(
            newest, pltpu.roll(vnewt_s[...], to_last, axis=1),
            pltpu.roll(vt, WINDOW - 1, axis=1))
        q_row = q_s[pl.ds(r, 1), :]
        pieces = [jnp.broadcast_to(q_row[:, g * KV_W:(g + 1) * KV_W],
                                   (N_HEADS, KV_W)) for g in range(GROUP)]
        q_rows = jnp.where(row_g == 0, pieces[0],
                           jnp.where(row_g == 1, pieces[1],
                                     jnp.where(row_g == 2, pieces[2],
                                               pieces[3])))
        qm = jnp.where(seg, q_rows, 0.0).astype(BF16)
        k_row = knew_s[pl.ds(r, 1), :].astype(BF16).astype(F32)
        v_row = vnew_s[pl.ds(r, 1), :].astype(BF16).astype(F32)
        s_new = jnp.sum(qm.astype(F32) * k_row, axis=-1, keepdims=True)
        s = jnp.where(key0, s_new, _dot(qm, kt.astype(BF16))) + bias
        m = jnp.maximum(jnp.max(s, axis=-1, keepdims=True), sink)
        p = jnp.exp(s - m)
        den = jnp.sum(p, axis=-1, keepdims=True) + jnp.exp(sink - m)
        p16 = p.astype(BF16)
        p_new = p16[:, 0:1].astype(F32)
        p_old = jnp.where(key0, jnp.zeros_like(p16), p16)
        o = (_dot_t(p_old, vt.astype(BF16)) + p_new * v_row) / den
        om = jnp.where(seg, o, 0.0)
        for g in range(GROUP):
            o_s[pl.ds(r, 1), g * KV_W:(g + 1) * KV_W] = jnp.sum(
                om[g * N_KV:(g + 1) * N_KV, :], axis=0, keepdims=True)

    @pl.when(i == last)
    def _():
        att = _dot(o_s[...].astype(BF16), wao_ref[...])
        gate_b = jax.nn.sigmoid(_dot(h_s[...], win_ref[:, C_GB:C_END]))
        merged = (ag_s[...] + gate_b * att).astype(BF16)
        y_ref[...] = x_ref[...] + _rms(_dot(merged, wo_ref[...]),
                                       g_ref[3:4, :])


def _mixer_decode(x, st, kc, vc, norm_g, w_in, w_q, conv_w, w_co, w_ao, w_o,
                  bdec, sink_col, layer, prev_caches):
    n = x.shape[0]
    sb = DEC_SEQS
    cache_spec = pl.BlockSpec((None, sb, KV_W, WINDOW),
                              lambda i: (layer, i, 0, 0))
    n_alias = len(prev_caches)
    n_in = 13
    return pl.pallas_call(
        functools.partial(_mixer_decode_kernel, n_alias=n_alias),
        grid=(n // sb,),
        in_specs=[
            _resident((n, D_MODEL)),
            _resident((n, 2 * D_MODEL)),
            cache_spec,
            cache_spec,
            _layer_resident((N_NORMS, D_MODEL), layer),
            _resident((D_MODEL, C_END)),
            _layer_resident((D_MODEL, D_MODEL), layer),
            _layer_resident((3, D_MODEL), layer),
            _resident((D_MODEL, D_MODEL)),
            _resident((D_MODEL, D_MODEL)),
            _resident((D_MODEL, D_MODEL)),
            _resident((N_HEADS, WINDOW)),
            _resident((N_HEADS, 1)),
        ] + [pl.BlockSpec(memory_space=pl.ANY)] * n_alias,
        out_specs=[
            pl.BlockSpec((n, D_MODEL), lambda i: (0, 0)),
            pl.BlockSpec((n, 2 * D_MODEL), lambda i: (0, 0)),
            cache_spec,
            cache_spec,
        ],
        out_shape=[
            jax.ShapeDtypeStruct((n, D_MODEL), F32),
            jax.ShapeDtypeStruct((n, 2 * D_MODEL), F32),
            jax.ShapeDtypeStruct(kc.shape, F32),
            jax.ShapeDtypeStruct(vc.shape, F32),
        ],
        input_output_aliases={n_in + a: 2 + a for a in range(n_alias)},
        scratch_shapes=[
            pltpu.VMEM((n, D_MODEL), BF16),
            pltpu.VMEM((n, D_MODEL), F32),
            pltpu.VMEM((n, KV_W), F32),
            pltpu.VMEM((n, KV_W), F32),
            pltpu.VMEM((KV_W, n), F32),
            pltpu.VMEM((KV_W, n), F32),
            pltpu.VMEM((n, D_MODEL), F32),
            pltpu.VMEM((n, D_MODEL), F32),
        ],
        compiler_params=pltpu.CompilerParams(
            dimension_semantics=("arbitrary",),
            vmem_limit_bytes=VMEM_LIMIT_BYTES),
        name="mixer_decode",
    )(x, st, kc, vc, norm_g, w_in, w_q, conv_w, w_co, w_ao, w_o, bdec,
      sink_col, *prev_caches)


def _heads_g_major(w, axis):
    shape = w.shape
    split = shape[:axis] + (N_KV, GROUP, HEAD_DIM) + shape[axis + 1:]
    perm = list(range(len(split)))
    perm[axis], perm[axis + 1] = perm[axis + 1], perm[axis]
    return w.reshape(split).transpose(perm).reshape(shape)


def kernel(x_prompt, x_sample, state_conv, cache_k_win, cache_v_win, rel_bias,
           norm_g, w_ff1_gu, w_ff1_down, w_in, conv_w, sinks, w_conv_out,
           w_attn_out, w_out, w_ff2_gu, w_ff2_down):
    batch, seq, _ = x_prompt.shape
    n_dec = x_sample.shape[0]

    bt_plain, bt, ffn_w = _bias_table(
        rel_bias, _ffn_cast_jobs(w_ff1_gu, w_ff1_down, 0))
    bdec = bt_plain[0].reshape(GROUP, BLOCK, N_KV, BLOCK)[:, BLOCK - 1]
    bdec = bdec.reshape(N_HEADS, WINDOW)
    bdec = jnp.concatenate([bdec[:, WINDOW - 1:], bdec[:, :WINDOW - 1]], axis=1)

    def keys_minor(c):
        return jnp.transpose(c, (0, 1, 3, 4, 2)).reshape(
            DEPTH, n_dec, KV_W, WINDOW)

    def keys_major(c):
        return jnp.transpose(
            c.reshape(DEPTH, n_dec, N_KV, HEAD_DIM, WINDOW), (0, 1, 4, 2, 3))

    kct = keys_minor(cache_k_win)
    vct = keys_minor(cache_v_win)

    wq = _heads_g_major(w_in[:, :, C_Q:C_K], 2).astype(BF16)

    xp = x_prompt.reshape(batch * seq, D_MODEL)
    xs = x_sample.reshape(n_dec, D_MODEL)
    pc, pk, pv, sc = [], [], [], []
    new_caches = ()
    for l in range(DEPTH):
        sink_col = sinks[l].reshape(N_KV, GROUP).T.reshape(N_HEADS, 1)

        xp, xs, mix_w = _ffn(
            xp, xs, norm_g, ffn_w[0], ffn_w[1], l, 0,
            _mixer_cast_jobs(w_in, w_conv_out, w_attn_out, w_out, l))
        win, wco, wao, wo = mix_w

        xp, c1, k1, v1, ffn_w = _mixer_prompt(
            xp, batch, norm_g, win, wq, conv_w, wco, wao, wo, bt, sinks, l,
            _ffn_cast_jobs(w_ff2_gu, w_ff2_down, l))
        xs, c2, k2, v2 = _mixer_decode(
            xs, state_conv[l].reshape(n_dec, 2 * D_MODEL), kct, vct,
            norm_g, win, wq, conv_w, wco, wao, wo, bdec, sink_col, l,
            new_caches)
        new_caches = (k2, v2)

        next_jobs = (_ffn_cast_jobs(w_ff1_gu, w_ff1_down, l + 1)
                     if l + 1 < DEPTH else [])
        xp, xs, ffn_w = _ffn(xp, xs, norm_g, ffn_w[0], ffn_w[1], l, 4,
                             next_jobs)

        pc.append(c1)
        pk.append(k1.reshape(batch, WINDOW, N_KV, HEAD_DIM))
        pv.append(v1.reshape(batch, WINDOW, N_KV, HEAD_DIM))
        sc.append(c2.reshape(n_dec, 2, D_MODEL))

    return (xp.reshape(batch, seq, D_MODEL), xs.reshape(n_dec, 1, D_MODEL),
            jnp.stack(pc), jnp.stack(pk), jnp.stack(pv),
            jnp.stack(sc), keys_major(new_caches[0]),
            keys_major(new_caches[1]))
```

```python
import functools
from typing import Callable, NamedTuple

import numpy as np
import jax
import jax.numpy as jnp
from jax import lax
from jax.experimental import pallas as pl
from jax.experimental.pallas import tpu as pltpu

F32 = jnp.float32
BF16 = jnp.bfloat16

D_MODEL = 1024
D_FF = 2816
N_HEADS = 16
N_KV = 4
GROUP = N_HEADS // N_KV
HEAD_DIM = 64
KV_W = N_KV * HEAD_DIM
WINDOW = 128
BLOCK = 128
N_BUCKETS = 32
MAX_DISTANCE = 128
RMS_EPS = 1e-6
NEG = -1e30
LOG2E = 1.4426950408889634
DEPTH = 2
N_NORMS = 6

C_CB, C_CC, C_CX, C_Q, C_K, C_V, C_GA, C_GB, C_END = (
    0, 1024, 2048, 3072, 4096, 4352, 4608, 5632, 6656)

VMEM_LIMIT_BYTES = 56 * 1024 * 1024

FFN_ROWS = 512
FFN_CHUNK = 256
MIX_ROWS = 512
DEC_SEQS = 8


def _t5_bucket_np(rel):
    n = np.maximum(rel, 0)
    max_exact = N_BUCKETS // 2
    nf = np.maximum(n, 1).astype(np.float32)
    large = max_exact + (
        np.log(nf / max_exact) / np.log(MAX_DISTANCE / max_exact)
        * (N_BUCKETS - max_exact)).astype(np.int32)
    large = np.minimum(large, N_BUCKETS - 1)
    return np.where(n < max_exact, n, large).astype(np.int32)


def _band_tables():
    qi = np.arange(BLOCK)[:, None]
    kj = np.arange(BLOCK)[None, :]
    rel = np.where(kj > qi, qi + BLOCK - kj, qi - kj)
    bucket = _t5_bucket_np(rel)
    mask = np.stack([np.ones_like(rel, bool), kj <= qi]).astype(np.float32)
    return bucket, mask


_BUCKET_NP, _MASK_NP = _band_tables()


def _rms(x, g):
    r = lax.rsqrt(jnp.mean(x * x, axis=-1, keepdims=True) + RMS_EPS)
    return (x * r) * g


def _dot(a, b):
    return jnp.dot(a, b, preferred_element_type=F32)


def _dot_t(a, b):
    return lax.dot_general(a, b, (((1,), (1,)), ((), ())),
                           preferred_element_type=F32)


def _resident(shape):
    return pl.BlockSpec(shape, lambda *_: (0,) * len(shape),
                        pipeline_mode=pl.Buffered(1))


def _layer_resident(shape, layer):
    return pl.BlockSpec((None,) + tuple(shape),
                        lambda *_: (layer,) + (0,) * len(shape),
                        pipeline_mode=pl.Buffered(1))


class _CastJob(NamedTuple):
    src: jax.Array
    layer: int
    rows: int
    src_block: Callable = lambda d: d

    @property
    def n_blocks(self):
        return self.src.shape[1] // self.rows


def _cast_specs(jobs, step_of):
    in_specs, out_specs, out_shapes = [], [], []
    for job in jobs:
        cols = job.src.shape[2]

        def dst(*ids, job=job):
            return jnp.minimum(step_of(*ids), job.n_blocks - 1)

        in_specs.append(pl.BlockSpec(
            (None, job.rows, cols),
            lambda *ids, job=job, dst=dst: (job.layer,
                                            job.src_block(dst(*ids)), 0)))
        out_specs.append(pl.BlockSpec(
            (job.rows, cols), lambda *ids, dst=dst: (dst(*ids), 0)))
        out_shapes.append(jax.ShapeDtypeStruct(job.src.shape[1:], BF16))
    return in_specs, out_specs, out_shapes


def _run_casts(step, n_blocks, in_refs, out_refs):
    for nb, src, dst in zip(n_blocks, in_refs, out_refs):
        @pl.when(step < nb)
        def _(src=src, dst=dst):
            dst[...] = src[...].astype(BF16)


def _ffn_cast_jobs(w_gu, w_down, layer):
    return [_CastJob(w_gu, layer, 32), _CastJob(w_down, layer, 128)]


def _mixer_cast_jobs(w_in, w_co, w_ao, w_o, layer):
    to_g_major = lambda d: (d % GROUP) * N_KV + d // GROUP
    return [_CastJob(w_in, layer, 32), _CastJob(w_co, layer, 64),
            _CastJob(w_ao, layer, HEAD_DIM, to_g_major),
            _CastJob(w_o, layer, 64)]


def _bias_kernel(*refs, cast_blocks):
    nj = len(cast_blocks)
    rb_ref, bucket_ref, mask_ref = refs[:3]
    cast_in = refs[3:3 + nj]
    plain_ref, exp2_ref = refs[3 + nj:5 + nj]
    cast_out = refs[5 + nj:]
    g = pl.program_id(1)
    hk = pl.program_id(2)
    step = (pl.program_id(0) * GROUP + g) * N_KV + hk
    _run_casts(step, cast_blocks, cast_in, cast_out)
    h = hk * GROUP + g
    bucket = bucket_ref[...]

    def body(b, acc):
        return jnp.where(bucket == b, rb_ref[b, h], acc)

    bias = lax.fori_loop(0, N_BUCKETS, body, jnp.zeros((BLOCK, BLOCK), F32))
    valid = mask_ref[...] > 0.0
    plain_ref[...] = jnp.where(valid, bias, NEG)
    exp2_ref[...] = jnp.where(valid, bias * LOG2E, NEG)


def _bias_table(rel_bias, jobs):
    c_in, c_out, c_shapes = _cast_specs(
        jobs, lambda v, g, hk: (v * GROUP + g) * N_KV + hk)
    table_spec = pl.BlockSpec((None, BLOCK, BLOCK), lambda v, g, hk: (v, g, hk))
    table_shape = jax.ShapeDtypeStruct((2, GROUP * BLOCK, N_KV * BLOCK), F32)
    outs = pl.pallas_call(
        functools.partial(_bias_kernel,
                          cast_blocks=tuple(j.n_blocks for j in jobs)),
        grid=(2, GROUP, N_KV),
        in_specs=[
            pl.BlockSpec(memory_space=pltpu.SMEM),
            pl.BlockSpec((BLOCK, BLOCK), lambda v, g, hk: (0, 0)),
            pl.BlockSpec((None, BLOCK, BLOCK), lambda v, g, hk: (v, 0, 0)),
        ] + c_in,
        out_specs=[table_spec, table_spec] + c_out,
        out_shape=[table_shape, table_shape] + c_shapes,
        compiler_params=pltpu.CompilerParams(
            dimension_semantics=("arbitrary",) * 3),
        name="bias_table",
    )(rel_bias, jnp.asarray(_BUCKET_NP), jnp.asarray(_MASK_NP),
      *[j.src for j in jobs])
    return outs[0], outs[1], outs[2:]


def _ffn_rows(x, g_ref, wgu_ref, wd_ref, act_ref, gi):
    rows = x.shape[0]
    h = _rms(x, g_ref[gi:gi + 1, :]).astype(BF16)
    for c in range(D_FF // FFN_CHUNK):
        lo = c * FFN_CHUNK
        gate = _dot(h, wgu_ref[:, lo:lo + FFN_CHUNK])
        up = _dot(h, wgu_ref[:, D_FF + lo:D_FF + lo + FFN_CHUNK])
        act_ref[0:rows, lo:lo + FFN_CHUNK] = (
            gate * jax.nn.sigmoid(gate) * up).astype(BF16)
    y = _dot(act_ref[0:rows, :], wd_ref[...])
    return x + 0.5 * _rms(y, g_ref[gi + 1:gi + 2, :])


def _ffn_kernel(*refs, gi, prompt_steps, cast_blocks):
    nj = len(cast_blocks)
    xp_ref, xs_ref, g_ref, wgu_ref, wd_ref = refs[:5]
    cast_in = refs[5:5 + nj]
    op_ref, os_ref = refs[5 + nj:7 + nj]
    cast_out = refs[7 + nj:7 + 2 * nj]
    act_ref = refs[7 + 2 * nj]
    i = pl.program_id(0)
    _run_casts(i, cast_blocks, cast_in, cast_out)

    @pl.when(i < prompt_steps)
    def _():
        op_ref[...] = _ffn_rows(xp_ref[...], g_ref, wgu_ref, wd_ref, act_ref,
                                gi)

    @pl.when(i == prompt_steps)
    def _():
        os_ref[...] = _ffn_rows(xs_ref[...], g_ref, wgu_ref, wd_ref, act_ref,
                                gi)


def _ffn(xp, xs, norm_g, w_gu, w_down, layer, gi, jobs):
    m = xp.shape[0]
    n = xs.shape[0]
    tm = FFN_ROWS
    steps = m // tm
    row_block = lambda i: (jnp.minimum(i, steps - 1), 0)
    c_in, c_out, c_shapes = _cast_specs(jobs, lambda i: i)
    outs = pl.pallas_call(
        functools.partial(_ffn_kernel, gi=gi, prompt_steps=steps,
                          cast_blocks=tuple(j.n_blocks for j in jobs)),
        grid=(steps + 1,),
        in_specs=[
            pl.BlockSpec((tm, D_MODEL), row_block),
            _resident((n, D_MODEL)),
            _layer_resident((N_NORMS, D_MODEL), layer),
            _resident((D_MODEL, 2 * D_FF)),
            _resident((D_FF, D_MODEL)),
        ] + c_in,
        out_specs=[
            pl.BlockSpec((tm, D_MODEL), row_block),
            pl.BlockSpec((n, D_MODEL), lambda i: (0, 0)),
        ] + c_out,
        out_shape=[
            jax.ShapeDtypeStruct((m, D_MODEL), F32),
            jax.ShapeDtypeStruct((n, D_MODEL), F32),
        ] + c_shapes,
        scratch_shapes=[pltpu.VMEM((tm, D_FF), BF16)],
        compiler_params=pltpu.CompilerParams(
            dimension_semantics=("arbitrary",),
            vmem_limit_bytes=VMEM_LIMIT_BYTES),
        name="ffn_half",
    )(xp, xs, norm_g, w_gu, w_down, *[j.src for j in jobs])
    return outs[0], outs[1], outs[2:]


def _lane_segment_mask(rows):
    lane = lax.broadcasted_iota(jnp.int32, (rows, KV_W), 1)
    return [(lane >= hk * HEAD_DIM) & (lane < (hk + 1) * HEAD_DIM)
            for hk in range(N_KV)]


def _mixer_prompt_kernel(*refs, layer, rows, cast_blocks):
    nj = len(cast_blocks)
    (x_ref, g_ref, win_ref, wq_ref, cw_ref, wco_ref, wao_ref, wo_ref, bt_ref,
     sinks_ref) = refs[:10]
    cast_in = refs[10:10 + nj]
    y_ref, cst_ref, kout_ref, vout_ref = refs[10 + nj:14 + nj]
    cast_out = refs[14 + nj:14 + 2 * nj]
    q_s, kbuf, vbuf, upad, o_s, p_s = refs[14 + 2 * nj:]
    i = pl.program_id(1)
    steps = pl.num_programs(1)
    last = steps - 1
    nq = rows // BLOCK
    _run_casts(pl.program_id(0) * steps + i, cast_blocks, cast_in, cast_out)

    @pl.when(i == 0)
    def _():
        upad[0:8, :] = jnp.zeros((8, D_MODEL), F32)
        kbuf[:, 0:BLOCK, :] = jnp.zeros((N_KV, BLOCK, KV_W), BF16)
        vbuf[:, 0:BLOCK, :] = jnp.zeros((N_KV, BLOCK, KV_W), BF16)

    @pl.when(i > 0)
    def _():
        upad[0:8, :] = upad[rows:rows + 8, :]
        kbuf[:, 0:BLOCK, :] = kbuf[:, rows:rows + BLOCK, :]
        vbuf[:, 0:BLOCK, :] = vbuf[:, rows:rows + BLOCK, :]

    x = x_ref[...]
    h = _rms(x, g_ref[2:3, :]).astype(BF16)

    q_s[...] = (_dot(h, wq_ref[...])
                * (HEAD_DIM ** -0.5 * LOG2E)).astype(BF16)
    k = _dot(h, win_ref[:, C_K:C_V])
    v = _dot(h, win_ref[:, C_V:C_GA])
    k16 = k.astype(BF16)
    v16 = v.astype(BF16)
    seg = _lane_segment_mask(rows)
    zero_kv = jnp.zeros((rows, KV_W), BF16)
    for hk in range(N_KV):
        kbuf[hk, BLOCK:BLOCK + rows, :] = jnp.where(seg[hk], k16, zero_kv)
        vbuf[hk, BLOCK:BLOCK + rows, :] = jnp.where(seg[hk], v16, zero_kv)

    kout_ref[...] = k[rows - BLOCK:rows, :]
    vout_ref[...] = v[rows - BLOCK:rows, :]

    q_idx = lax.broadcasted_iota(jnp.int32, (BLOCK, BLOCK), 0)
    j_idx = lax.broadcasted_iota(jnp.int32, (BLOCK, BLOCK), 1)
    from_prev = j_idx > q_idx
    prev16 = jnp.where(from_prev, 1.0, 0.0).astype(BF16)
    own16 = jnp.where(from_prev, 0.0, 1.0).astype(BF16)
    first = (i == 0).astype(jnp.int32)
    seg_q = _lane_segment_mask(BLOCK)

    for j in range(nq):
        r0 = j * BLOCK
        variant = first if j == 0 else 0
        qj = q_s[r0:r0 + BLOCK, :]
        kb = jnp.concatenate(
            [kbuf[hk, r0:r0 + 2 * BLOCK, :] for hk in range(N_KV)], axis=0)
        vb = jnp.concatenate(
            [vbuf[hk, r0:r0 + 2 * BLOCK, :] for hk in range(N_KV)], axis=0)
        qs = jnp.concatenate(
            [qj[:, g * KV_W:(g + 1) * KV_W] for g in range(GROUP)], axis=0)
        s_all = _dot_t(qs, kb)
        inv = []
        for g in range(GROUP):
            inv_g = []
            for hk in range(N_KV):
                rs = slice(g * BLOCK, (g + 1) * BLOCK)
                c0 = hk * 2 * BLOCK
                s = jnp.where(from_prev, s_all[rs, c0:c0 + BLOCK],
                              s_all[rs, c0 + BLOCK:c0 + 2 * BLOCK])
                s = s + bt_ref[variant, rs, hk * BLOCK:(hk + 1) * BLOCK]
                sink = sinks_ref[layer, hk * GROUP + g] * LOG2E
                m = jnp.maximum(jnp.max(s, axis=-1, keepdims=True), sink)
                p = jnp.exp2(s - m)
                p16 = p.astype(BF16)
                p_s[j, rs, c0:c0 + BLOCK] = p16 * prev16
                p_s[j, rs, c0 + BLOCK:c0 + 2 * BLOCK] = p16 * own16
                den = jnp.sum(p, axis=-1, keepdims=True) + jnp.exp2(sink - m)
                inv_g.append(1.0 / den)
            inv.append(inv_g)
        o_all = _dot(p_s[j], vb)
        for g in range(GROUP):
            scale = jnp.where(
                seg_q[0], inv[g][0],
                jnp.where(seg_q[1], inv[g][1],
                          jnp.where(seg_q[2], inv[g][2], inv[g][3])))
            o_s[r0:r0 + BLOCK, g * KV_W:(g + 1) * KV_W] = (
                o_all[g * BLOCK:(g + 1) * BLOCK, :] * scale).astype(BF16)

    u = _dot(h, win_ref[:, C_CC:C_CX]) * _dot(h, win_ref[:, C_CX:C_Q])
    upad[8:rows + 8, :] = u

    cst_ref[...] = u[rows - 2:rows, :]
    yc =(cw_ref[0:1, :] * upad[6:rows + 6, :]
          + cw_ref[1:2, :] * upad[7:rows + 7, :]
          + cw_ref[2:3, :] * u)
    a_in = (_dot(h, win_ref[:, C_CB:C_CC]) * yc).astype(BF16)
    a_out = _dot(a_in, wco_ref[...])
    gated_a = jax.nn.sigmoid(_dot(h, win_ref[:, C_GA:C_GB])) * a_out

    att = _dot(o_s[...], wao_ref[...])
    gate_b = jax.nn.sigmoid(_dot(h, win_ref[:, C_GB:C_END]))
    merged = (gated_a + gate_b * att).astype(BF16)
    y_ref[...] = x + _rms(_dot(merged, wo_ref[...]), g_ref[3:4, :])


def _mixer_prompt(x, batch, norm_g, w_in, w_q, conv_w, w_co, w_ao, w_o, bt,
                  sinks, layer, jobs):
    m = x.shape[0]
    seq = m // batch
    rows = MIX_ROWS
    steps = seq // rows
    c_in, c_out, c_shapes = _cast_specs(jobs, lambda b, i: b * steps + i)
    kern = functools.partial(_mixer_prompt_kernel, layer=layer, rows=rows,
                             cast_blocks=tuple(j.n_blocks for j in jobs))
    outs = pl.pallas_call(
        kern,
        grid=(batch, steps),
        in_specs=[
            pl.BlockSpec((rows, D_MODEL), lambda b, i: (b * steps + i, 0)),
            _layer_resident((N_NORMS, D_MODEL), layer),
            _resident((D_MODEL, C_END)),
            _layer_resident((D_MODEL, D_MODEL), layer),
            _layer_resident((3, D_MODEL), layer),
            _resident((D_MODEL, D_MODEL)),
            _resident((D_MODEL, D_MODEL)),
            _resident((D_MODEL, D_MODEL)),
            _resident((2, GROUP * BLOCK, N_KV * BLOCK)),
            pl.BlockSpec(memory_space=pltpu.SMEM),
        ] + c_in,
        out_specs=[
            pl.BlockSpec((rows, D_MODEL), lambda b, i: (b * steps + i, 0)),
            pl.BlockSpec((None, 2, D_MODEL), lambda b, i: (b, 0, 0)),
            pl.BlockSpec((None, BLOCK, KV_W), lambda b, i: (b, 0, 0)),
            pl.BlockSpec((None, BLOCK, KV_W), lambda b, i: (b, 0, 0)),
        ] + c_out,
        out_shape=[
            jax.ShapeDtypeStruct((m, D_MODEL), F32),
            jax.ShapeDtypeStruct((batch, 2, D_MODEL), F32),
            jax.ShapeDtypeStruct((batch, BLOCK, KV_W), F32),
            jax.ShapeDtypeStruct((batch, BLOCK, KV_W), F32),
        ] + c_shapes,
        scratch_shapes=[
            pltpu.VMEM((rows, D_MODEL), BF16),
            pltpu.VMEM((N_KV, rows + BLOCK, KV_W), BF16),
            pltpu.VMEM((N_KV, rows + BLOCK, KV_W), BF16),
            pltpu.VMEM((rows + 8, D_MODEL), F32),
            pltpu.VMEM((rows, D_MODEL), BF16),
            pltpu.VMEM((rows // BLOCK, GROUP * BLOCK, N_KV * 2 * BLOCK),
                       BF16),
        ],
        compiler_params=pltpu.CompilerParams(
            dimension_semantics=("arbitrary", "arbitrary"),
            vmem_limit_bytes=VMEM_LIMIT_BYTES,
        ),
        name="mixer_prompt",
    )(x, norm_g, w_in, w_q, conv_w, w_co, w_ao, w_o, bt, sinks,
      *[j.src for j in jobs])
    return outs[0], outs[1], outs[2], outs[3], outs[4:]


def _mixer_decode_kernel(*refs, n_alias):
    (x_ref, st_ref, kc_ref, vc_ref, g_ref, win_ref, wq_ref, cw_ref, wco_ref,
     wao_ref, wo_ref, bdec_ref, sink_ref) = refs[:13]
    (y_ref, nst_ref, kout_ref, vout_ref, h_s, q_s, knew_s, vnew_s, knewt_s,
     vnewt_s, ag_s, o_s) = refs[13 + n_alias:]
    i = pl.program_id(0)
    last = pl.num_programs(0) - 1

    @pl.when(i == 0)
    def _():
        x = x_ref[...]
        h = _rms(x, g_ref[2:3, :]).astype(BF16)
        h_s[...] = h
        u = _dot(h, win_ref[:, C_CC:C_CX]) * _dot(h, win_ref[:, C_CX:C_Q])
        st0 = st_ref[:, 0:D_MODEL]
        st1 = st_ref[:, D_MODEL:2 * D_MODEL]
        yc = cw_ref[0:1, :] * st0 + cw_ref[1:2, :] * st1 + cw_ref[2:3, :] * u
        nst_ref[:, 0:D_MODEL] = st1
        nst_ref[:, D_MODEL:2 * D_MODEL] = u
        a_in = (_dot(h, win_ref[:, C_CB:C_CC]) * yc).astype(BF16)
        a_out = _dot(a_in, wco_ref[...])
        ag_s[...] = jax.nn.sigmoid(_dot(h, win_ref[:, C_GA:C_GB])) * a_out
        q_s[...] = _dot(h, wq_ref[...]) * (HEAD_DIM ** -0.5)
        knew = _dot(h, win_ref[:, C_K:C_V])
        vnew = _dot(h, win_ref[:, C_V:C_GA])
        knew_s[...] = knew
        vnew_s[...] = vnew
        knewt_s[...] = knew.T
        vnewt_s[...] = vnew.T

    row = lax.broadcasted_iota(jnp.int32, (N_HEADS, KV_W), 0)
    lane = lax.broadcasted_iota(jnp.int32, (N_HEADS, KV_W), 1)
    row_g = row // N_KV
    seg = (lane // HEAD_DIM) == (row % N_KV)
    key0 = lax.broadcasted_iota(jnp.int32, (N_HEADS, WINDOW), 1) == 0
    newest = lax.broadcasted_iota(jnp.int32, (KV_W, WINDOW), 1) == WINDOW - 1
    bias = bdec_ref[...]
    sink = sink_ref[...]

    for b in range(DEC_SEQS):
        to_last = (WINDOW - 1) - (i * DEC_SEQS + b)
        kout_ref[b] = jnp.where(
            newest, pltpu.roll(knewt_s[...], to_last, axis=1),
            pltpu.roll(kc_ref[b], WINDOW - 1, axis=1))
        vout_ref[b] = jnp.where(
            newest, pltpu.roll(vnewt_s[...], to_last, axis=1),
            pltpu.roll(vc_ref[b], WINDOW - 1, axis=1))

    qms, k_rows, v_rows = [], [], []
    for b in range(DEC_SEQS):
        r = i * DEC_SEQS + b
        q_row = q_s[pl.ds(r, 1), :]
        pieces = [jnp.broadcast_to(q_row[:, g * KV_W:(g + 1) * KV_W],
                                   (N_HEADS, KV_W)) for g in range(GROUP)]
        q_rows = jnp.where(row_g == 0, pieces[0],
                           jnp.where(row_g == 1, pieces[1],
                                     jnp.where(row_g == 2, pieces[2],
                                               pieces[3])))
        qms.append(jnp.where(seg, q_rows, 0.0))
        k_rows.append(knew_s[pl.ds(r, 1), :])
        v_rows.append(vnew_s[pl.ds(r, 1), :])
    qm = jnp.stack(qms).astype(BF16)
    k_row = jnp.stack(k_rows).astype(BF16).astype(F32)
    v_row = jnp.stack(v_rows).astype(BF16).astype(F32)
    s_new = jnp.sum(qm.astype(F32) * k_row, axis=-1, keepdims=True)
    s_old = jnp.einsum('bqd,bdk->bqk', qm, kc_ref[...].astype(BF16),
                       preferred_element_type=F32)
    s = jnp.where(key0, s_new, s_old) + bias
    m = jnp.maximum(jnp.max(s, axis=-1, keepdims=True), sink)
    p = jnp.exp(s - m)
    den = jnp.sum(p, axis=-1, keepdims=True) + jnp.exp(sink - m)
    p16 = p.astype(BF16)
    p_new = p16[:, :, 0:1].astype(F32)
    p_old = jnp.where(key0, jnp.zeros_like(p16), p16)
    o = jnp.einsum('bqk,bdk->bqd', p_old, vc_ref[...].astype(BF16),
                   preferred_element_type=F32)
    om = jnp.where(seg, (o + p_new * v_row) / den, 0.0)
    for b in range(DEC_SEQS):
        r = i * DEC_SEQS + b
        for g in range(GROUP):
            o_s[pl.ds(r, 1), g * KV_W:(g + 1) * KV_W] = jnp.sum(
                om[b, g * N_KV:(g + 1) * N_KV, :], axis=0, keepdims=True)

    @pl.when(i == last)
    def _():
        att = _dot(o_s[...].astype(BF16), wao_ref[...])
        gate_b = jax.nn.sigmoid(_dot(h_s[...], win_ref[:, C_GB:C_END]))
        merged = (ag_s[...] + gate_b * att).astype(BF16)
        y_ref[...] = x_ref[...] + _rms(_dot(merged, wo_ref[...]),
                                       g_ref[3:4, :])


def _mixer_decode(x, st, kc, vc, norm_g, w_in, w_q, conv_w, w_co, w_ao, w_o,
                  bdec, sink_col, layer, prev_caches):
    n = x.shape[0]
    sb = DEC_SEQS
    cache_spec = pl.BlockSpec((None, sb, KV_W, WINDOW),
                              lambda i: (layer, i, 0, 0))
    n_alias = len(prev_caches)
    n_in = 13
    return pl.pallas_call(
        functools.partial(_mixer_decode_kernel, n_alias=n_alias),
        grid=(n // sb,),
        in_specs=[
            _resident((n, D_MODEL)),
            _resident((n, 2 * D_MODEL)),
            cache_spec,
            cache_spec,
            _layer_resident((N_NORMS, D_MODEL), layer),
            _resident((D_MODEL, C_END)),
            _layer_resident((D_MODEL, D_MODEL), layer),
            _layer_resident((3, D_MODEL), layer),
            _resident((D_MODEL, D_MODEL)),
            _resident((D_MODEL, D_MODEL)),
            _resident((D_MODEL, D_MODEL)),
            _resident((N_HEADS, WINDOW)),
            _resident((N_HEADS, 1)),
        ] + [pl.BlockSpec(memory_space=pl.ANY)] * n_alias,
        out_specs=[
            pl.BlockSpec((n, D_MODEL), lambda i: (0, 0)),
            pl.BlockSpec((n, 2 * D_MODEL), lambda i: (0, 0)),
            cache_spec,
            cache_spec,
        ],
        out_shape=[
            jax.ShapeDtypeStruct((n, D_MODEL), F32),
            jax.ShapeDtypeStruct((n, 2 * D_MODEL), F32),
            jax.ShapeDtypeStruct(kc.shape, F32),
            jax.ShapeDtypeStruct(vc.shape, F32),
        ],
        input_output_aliases={n_in + a: 2 + a for a in range(n_alias)},
        scratch_shapes=[
            pltpu.VMEM((n, D_MODEL), BF16),
            pltpu.VMEM((n, D_MODEL), F32),
            pltpu.VMEM((n, KV_W), F32),
            pltpu.VMEM((n, KV_W), F32),
            pltpu.VMEM((KV_W, n), F32),
            pltpu.VMEM((KV_W, n), F32),
            pltpu.VMEM((n, D_MODEL), F32),
            pltpu.VMEM((n, D_MODEL), F32),
        ],
        compiler_params=pltpu.CompilerParams(
            dimension_semantics=("arbitrary",),
            vmem_limit_bytes=VMEM_LIMIT_BYTES),
        name="mixer_decode",
    )(x, st, kc, vc, norm_g, w_in, w_q, conv_w, w_co, w_ao, w_o, bdec,
      sink_col, *prev_caches)


def _heads_g_major(w, axis):
    shape = w.shape
    split = shape[:axis] + (N_KV, GROUP, HEAD_DIM) + shape[axis + 1:]
    perm = list(range(len(split)))
    perm[axis], perm[axis + 1] = perm[axis + 1], perm[axis]
    return w.reshape(split).transpose(perm).reshape(shape)


def kernel(x_prompt, x_sample, state_conv, cache_k_win, cache_v_win, rel_bias,
           norm_g, w_ff1_gu, w_ff1_down, w_in, conv_w, sinks, w_conv_out,
           w_attn_out, w_out, w_ff2_gu, w_ff2_down):
    batch, seq, _ = x_prompt.shape
    n_dec = x_sample.shape[0]

    bt_plain, bt, ffn_w = _bias_table(
        rel_bias, _ffn_cast_jobs(w_ff1_gu, w_ff1_down, 0))
    bdec = bt_plain[0].reshape(GROUP, BLOCK, N_KV, BLOCK)[:, BLOCK - 1]
    bdec = bdec.reshape(N_HEADS, WINDOW)
    bdec = jnp.concatenate([bdec[:, WINDOW - 1:], bdec[:, :WINDOW - 1]], axis=1)

    def keys_minor(c):
        return jnp.transpose(c, (0, 1, 3, 4, 2)).reshape(
            DEPTH, n_dec, KV_W, WINDOW)

    def keys_major(c):
        return jnp.transpose(
            c.reshape(DEPTH, n_dec, N_KV, HEAD_DIM, WINDOW), (0, 1, 4, 2, 3))

    kct = keys_minor(cache_k_win)
    vct = keys_minor(cache_v_win)

    wq = _heads_g_major(w_in[:, :, C_Q:C_K], 2).astype(BF16)

    xp = x_prompt.reshape(batch * seq, D_MODEL)
    xs = x_sample.reshape(n_dec, D_MODEL)
    pc, pk, pv, sc = [], [], [], []
    new_caches = ()
    for l in range(DEPTH):
        sink_col = sinks[l].reshape(N_KV, GROUP).T.reshape(N_HEADS, 1)

        xp, xs, mix_w = _ffn(
            xp, xs, norm_g, ffn_w[0], ffn_w[1], l, 0,
            _mixer_cast_jobs(w_in, w_conv_out, w_attn_out, w_out, l))
        win, wco, wao, wo = mix_w

        xp, c1, k1, v1, ffn_w = _mixer_prompt(
            xp, batch, norm_g, win, wq, conv_w, wco, wao, wo, bt, sinks, l,
            _ffn_cast_jobs(w_ff2_gu, w_ff2_down, l))
        xs, c2, k2, v2 = _mixer_decode(
            xs, state_conv[l].reshape(n_dec, 2 * D_MODEL), kct, vct,
            norm_g, win, wq, conv_w, wco, wao, wo, bdec, sink_col, l,
            new_caches)
        new_caches = (k2, v2)

        next_jobs = (_ffn_cast_jobs(w_ff1_gu, w_ff1_down, l + 1)
                     if l + 1 < DEPTH else [])
        xp, xs, ffn_w = _ffn(xp, xs, norm_g, ffn_w[0], ffn_w[1], l, 4,
                             next_jobs)

        pc.append(c1)
        pk.append(k1.reshape(batch, WINDOW, N_KV, HEAD_DIM))
        pv.append(v1.reshape(batch, WINDOW, N_KV, HEAD_DIM))
        sc.append(c2.reshape(n_dec, 2, D_MODEL))

    return (xp.reshape(batch, seq, D_MODEL), xs.reshape(n_dec, 1, D_MODEL),
            jnp.stack(pc), jnp.stack(pk), jnp.stack(pv),
            jnp.stack(sc), keys_major(new_caches[0]),
            keys_major(new_caches[1]))
```

```python
import functools
from typing import Callable, NamedTuple

import numpy as np
import jax
import jax.numpy as jnp
from jax import lax
from jax.experimental import pallas as pl
from jax.experimental.pallas import tpu as pltpu

F32 = jnp.float32
BF16 = jnp.bfloat16

D_MODEL = 1024
D_FF = 2816
N_HEADS = 16
N_KV = 4
GROUP = N_HEADS // N_KV
HEAD_DIM = 64
KV_W = N_KV * HEAD_DIM
WINDOW = 128
BLOCK = 128
N_BUCKETS = 32
MAX_DISTANCE = 128
RMS_EPS = 1e-6
NEG = -1e30
LOG2E = 1.4426950408889634
DEPTH = 2
N_NORMS = 6

C_CB, C_CC, C_CX, C_Q, C_K, C_V, C_GA, C_GB, C_END = (
    0, 1024, 2048, 3072, 4096, 4352, 4608, 5632, 6656)

VMEM_LIMIT_BYTES = 56 * 1024 * 1024

FFN_ROWS = 512
FFN_CHUNK = 256
MIX_ROWS = 512
DEC_SEQS = 8


def _t5_bucket_np(rel):
    n = np.maximum(rel, 0)
    max_exact = N_BUCKETS // 2
    nf = np.maximum(n, 1).astype(np.float32)
    large = max_exact + (
        np.log(nf / max_exact) / np.log(MAX_DISTANCE / max_exact)
        * (N_BUCKETS - max_exact)).astype(np.int32)
    large = np.minimum(large, N_BUCKETS - 1)
    return np.where(n < max_exact, n, large).astype(np.int32)


def _band_tables():
    qi = np.arange(BLOCK)[:, None]
    kj = np.arange(BLOCK)[None, :]
    rel = np.where(kj > qi, qi + BLOCK - kj, qi - kj)
    bucket = _t5_bucket_np(rel)
    mask = np.stack([np.ones_like(rel, bool), kj <= qi]).astype(np.float32)
    return bucket, mask


_BUCKET_NP, _MASK_NP = _band_tables()


def _rms(x, g):
    r = lax.rsqrt(jnp.mean(x * x, axis=-1, keepdims=True) + RMS_EPS)
    return (x * r) * g


def _dot(a, b):
    return jnp.dot(a, b, preferred_element_type=F32)


def _dot_t(a, b):
    return lax.dot_general(a, b, (((1,), (1,)), ((), ())),
                           preferred_element_type=F32)


def _resident(shape):
    return pl.BlockSpec(shape, lambda *_: (0,) * len(shape),
                        pipeline_mode=pl.Buffered(1))


def _layer_resident(shape, layer):
    return pl.BlockSpec((None,) + tuple(shape),
                        lambda *_: (layer,) + (0,) * len(shape),
                        pipeline_mode=pl.Buffered(1))


class _CastJob(NamedTuple):
    src: jax.Array
    layer: int
    rows: int
    src_block: Callable = lambda d: d

    @property
    def n_blocks(self):
        return self.src.shape[1] // self.rows


def _cast_specs(jobs, step_of):
    in_specs, out_specs, out_shapes = [], [], []
    for job in jobs:
        cols = job.src.shape[2]

        def dst(*ids, job=job):
            return jnp.minimum(step_of(*ids), job.n_blocks - 1)

        in_specs.append(pl.BlockSpec(
            (None, job.rows, cols),
            lambda *ids, job=job, dst=dst: (job.layer,
                                            job.src_block(dst(*ids)), 0)))
        out_specs.append(pl.BlockSpec(
            (job.rows, cols), lambda *ids, dst=dst: (dst(*ids), 0)))
        out_shapes.append(jax.ShapeDtypeStruct(job.src.shape[1:], BF16))
    return in_specs, out_specs, out_shapes


def _run_casts(in_refs, out_refs):
    for src, dst in zip(in_refs, out_refs):
        dst[...] = src[...].astype(BF16)


def _ffn_cast_jobs(w_gu, w_down, layer):
    return [_CastJob(w_gu, layer, 32), _CastJob(w_down, layer, 128)]


def _mixer_cast_jobs(w_in, w_co, w_ao, w_o, layer):
    to_g_major = lambda d: (d % GROUP) * N_KV + d // GROUP
    return [_CastJob(w_in, layer, 32), _CastJob(w_co, layer, 64),
            _CastJob(w_ao, layer, HEAD_DIM, to_g_major),
            _CastJob(w_o, layer, 64)]


def _bias_kernel(*refs, cast_blocks):
    nj = len(cast_blocks)
    rb_ref, bucket_ref, mask_ref = refs[:3]
    cast_in = refs[3:3 + nj]
    plain_ref, exp2_ref = refs[3 + nj:5 + nj]
    cast_out = refs[5 + nj:]
    g = pl.program_id(1)
    hk = pl.program_id(2)
    _run_casts(cast_in, cast_out)
    h = hk * GROUP + g
    bucket = bucket_ref[...]

    def body(b, acc):
        return jnp.where(bucket == b, rb_ref[b, h], acc)

    bias = lax.fori_loop(0, N_BUCKETS, body, jnp.zeros((BLOCK, BLOCK), F32))
    valid = mask_ref[...] > 0.0
    plain_ref[...] = jnp.where(valid, bias, NEG)
    exp2_ref[...] = jnp.where(valid, bias * LOG2E, NEG)


def _bias_table(rel_bias, jobs):
    c_in, c_out, c_shapes = _cast_specs(
        jobs, lambda v, g, hk: (v * GROUP + g) * N_KV + hk)
    table_spec = pl.BlockSpec((None, BLOCK, BLOCK), lambda v, g, hk: (v, g, hk))
    table_shape = jax.ShapeDtypeStruct((2, GROUP * BLOCK, N_KV * BLOCK), F32)
    outs = pl.pallas_call(
        functools.partial(_bias_kernel,
                          cast_blocks=tuple(j.n_blocks for j in jobs)),
        grid=(2, GROUP, N_KV),
        in_specs=[
            pl.BlockSpec(memory_space=pltpu.SMEM),
            pl.BlockSpec((BLOCK, BLOCK), lambda v, g, hk: (0, 0)),
            pl.BlockSpec((None, BLOCK, BLOCK), lambda v, g, hk: (v, 0, 0)),
        ] + c_in,
        out_specs=[table_spec, table_spec] + c_out,
        out_shape=[table_shape, table_shape] + c_shapes,
        compiler_params=pltpu.CompilerParams(
            dimension_semantics=("arbitrary",) * 3),
        name="bias_table",
    )(rel_bias, jnp.asarray(_BUCKET_NP), jnp.asarray(_MASK_NP),
      *[j.src for j in jobs])
    return outs[0], outs[1], outs[2:]


def _ffn_inner(x, g_ref, wgu_ref, wd_ref, act_ref, gi):
    rows = x.shape[0]
    h = _rms(x, g_ref[gi:gi + 1, :]).astype(BF16)
    for c in range(D_FF // FFN_CHUNK):
        lo = c * FFN_CHUNK
        gate = _dot(h, wgu_ref[:, lo:lo + FFN_CHUNK])
        up = _dot(h, wgu_ref[:, D_FF + lo:D_FF + lo + FFN_CHUNK])
        act_ref[0:rows, lo:lo + FFN_CHUNK] = (
            gate * jax.nn.sigmoid(gate) * up).astype(BF16)
    return _dot(act_ref[0:rows, :], wd_ref[...])


def _ffn_finish(x, y, g_ref, gi):
    return x + 0.5 * _rms(y, g_ref[gi + 1:gi + 2, :])


def _ffn_kernel(*refs, gi, prompt_steps, cast_blocks):
    nj = len(cast_blocks)
    xa_ref, xc_ref, xs_ref, g_ref, wgu_ref, wd_ref = refs[:6]
    cast_in = refs[6:6 + nj]
    op_ref, os_ref = refs[6 + nj:8 + nj]
    cast_out = refs[8 + nj:8 + 2 * nj]
    act_ref, y_s = refs[8 + 2 * nj:]
    i = pl.program_id(0)

    @pl.when(i == 0)
    def _():
        y_s[1] = jnp.zeros(y_s.shape[1:], F32)

    @pl.when(i < prompt_steps)
    def _():
        _run_casts(cast_in, cast_out)
        op_ref[...] = _ffn_finish(xc_ref[...], y_s[(i + 1) % 2], g_ref, gi)
        y_s[i % 2] = _ffn_inner(xa_ref[...], g_ref, wgu_ref, wd_ref, act_ref,
                                gi)

    @pl.when(i == prompt_steps)
    def _():
        _run_casts(cast_in, cast_out)
        op_ref[...] = _ffn_finish(xc_ref[...], y_s[(i + 1) % 2], g_ref, gi)

    @pl.when(i == prompt_steps + 1)
    def _():
        _run_casts(cast_in, cast_out)
        xs = xs_ref[...]
        os_ref[...] = _ffn_finish(
            xs, _ffn_inner(xs, g_ref, wgu_ref, wd_ref, act_ref, gi), g_ref, gi)


def _ffn(xp, xs, norm_g, w_gu, w_down, layer, gi, jobs):
    m = xp.shape[0]
    n = xs.shape[0]
    tm = FFN_ROWS
    steps = m // tm
    cur_block = lambda i: (jnp.minimum(i, steps - 1), 0)
    prev_block = lambda i: (jnp.clip(i - 1, 0, steps - 1), 0)
    c_in, c_out, c_shapes = _cast_specs(jobs, lambda i: i)
    outs = pl.pallas_call(
        functools.partial(_ffn_kernel, gi=gi, prompt_steps=steps,
                          cast_blocks=tuple(j.n_blocks for j in jobs)),
        grid=(steps + 2,),
        in_specs=[
            pl.BlockSpec((tm, D_MODEL), cur_block),
            pl.BlockSpec((tm, D_MODEL), prev_block),
            _resident((n, D_MODEL)),
            _layer_resident((N_NORMS, D_MODEL), layer),
            _resident((D_MODEL, 2 * D_FF)),
            _resident((D_FF, D_MODEL)),
        ] + c_in,
        out_specs=[
            pl.BlockSpec((tm, D_MODEL), prev_block),
            pl.BlockSpec((n, D_MODEL), lambda i: (0, 0)),
        ] + c_out,
        out_shape=[
            jax.ShapeDtypeStruct((m, D_MODEL), F32),
            jax.ShapeDtypeStruct((n, D_MODEL), F32),
        ] + c_shapes,
        scratch_shapes=[
            pltpu.VMEM((tm, D_FF), BF16),
            pltpu.VMEM((2, tm, D_MODEL), F32),
        ],
        compiler_params=pltpu.CompilerParams(
            dimension_semantics=("arbitrary",),
            vmem_limit_bytes=VMEM_LIMIT_BYTES),
        name="ffn_half",
    )(xp, xp, xs, norm_g, w_gu, w_down, *[j.src for j in jobs])
    return outs[0], outs[1], outs[2:]


def _lane_segment_mask(rows):
    lane = lax.broadcasted_iota(jnp.int32, (rows, KV_W), 1)
    return [(lane >= hk * HEAD_DIM) & (lane < (hk + 1) * HEAD_DIM)
            for hk in range(N_KV)]


def _mixer_prompt_kernel(*refs, layer, rows, cast_blocks):
    nj = len(cast_blocks)
    (x_ref, g_ref, win_ref, wq_ref, cw_ref, wco_ref, wao_ref, wo_ref, bt_ref,
     sinks_ref) = refs[:10]
    cast_in = refs[10:10 + nj]
    y_ref, cst_ref, kout_ref, vout_ref = refs[10 + nj:14 + nj]
    cast_out = refs[14 + nj:14 + 2 * nj]
    q_s, kbuf, vbuf, upad, o_s, p_s = refs[14 + 2 * nj:]
    i = pl.program_id(1)
    nq = rows // BLOCK

    @pl.when(i == 0)
    def _():
        upad[0:8, :] = jnp.zeros((8, D_MODEL), F32)
        kbuf[:, 0:BLOCK, :] = jnp.zeros((N_KV, BLOCK, KV_W), BF16)
        vbuf[:, 0:BLOCK, :] = jnp.zeros((N_KV, BLOCK, KV_W), BF16)

    @pl.when(i > 0)
    def _():
        upad[0:8, :] = upad[rows:rows + 8, :]
        kbuf[:, 0:BLOCK, :] = kbuf[:, rows:rows + BLOCK, :]
        vbuf[:, 0:BLOCK, :] = vbuf[:, rows:rows + BLOCK, :]

    _run_casts(cast_in, cast_out)
    x = x_ref[...]
    h = _rms(x, g_ref[2:3, :]).astype(BF16)

    q_s[...] = (_dot(h, wq_ref[...])
                * (HEAD_DIM ** -0.5 * LOG2E)).astype(BF16)
    k = _dot(h, win_ref[:, C_K:C_V])
    v = _dot(h, win_ref[:, C_V:C_GA])
    k16 = k.astype(BF16)
    v16 = v.astype(BF16)
    seg = _lane_segment_mask(rows)
    zero_kv = jnp.zeros((rows, KV_W), BF16)
    for hk in range(N_KV):
        kbuf[hk, BLOCK:BLOCK + rows, :] = jnp.where(seg[hk], k16, zero_kv)
        vbuf[hk, BLOCK:BLOCK + rows, :] = jnp.where(seg[hk], v16, zero_kv)

    kout_ref[...] = k[rows - BLOCK:rows, :]
    vout_ref[...] = v[rows - BLOCK:rows, :]

    q_idx = lax.broadcasted_iota(jnp.int32, (BLOCK, BLOCK), 0)
    j_idx = lax.broadcasted_iota(jnp.int32, (BLOCK, BLOCK), 1)
    from_prev = j_idx > q_idx
    prev16 = jnp.where(from_prev, 1.0, 0.0).astype(BF16)
    own16 = jnp.where(from_prev, 0.0, 1.0).astype(BF16)
    first = (i == 0).astype(jnp.int32)
    seg_q = _lane_segment_mask(BLOCK)

    for j in range(nq):
        r0 = j * BLOCK
        variant = first if j == 0 else 0
        qj = q_s[r0:r0 + BLOCK, :]
        kb = jnp.concatenate(
            [kbuf[hk, r0:r0 + 2 * BLOCK, :] for hk in range(N_KV)], axis=0)
        vb = jnp.concatenate(
            [vbuf[hk, r0:r0 + 2 * BLOCK, :] for hk in range(N_KV)], axis=0)
        qs = jnp.concatenate(
            [qj[:, g * KV_W:(g + 1) * KV_W] for g in range(GROUP)], axis=0)
        s_all = _dot_t(qs, kb)
        inv = []
        for g in range(GROUP):
            inv_g = []
            for hk in range(N_KV):
                rs = slice(g * BLOCK, (g + 1) * BLOCK)
                c0 = hk * 2 * BLOCK
                s = jnp.where(from_prev, s_all[rs, c0:c0 + BLOCK],
                              s_all[rs, c0 + BLOCK:c0 + 2 * BLOCK])
                s = s + bt_ref[variant, rs, hk * BLOCK:(hk + 1) * BLOCK]
                sink = sinks_ref[layer, hk * GROUP + g] * LOG2E
                m = jnp.maximum(jnp.max(s, axis=-1, keepdims=True), sink)
                p = jnp.exp2(s - m)
                p16 = p.astype(BF16)
                p_s[j, rs, c0:c0 + BLOCK] = p16 * prev16
                p_s[j, rs, c0 + BLOCK:c0 + 2 * BLOCK] = p16 * own16
                den = jnp.sum(p, axis=-1, keepdims=True) + jnp.exp2(sink - m)
                inv_g.append(1.0 / den)
            inv.append(inv_g)
        o_all = _dot(p_s[j], vb)
        for g in range(GROUP):
            scale = jnp.where(
                seg_q[0], inv[g][0],
                jnp.where(seg_q[1], inv[g][1],
                          jnp.where(seg_q[2], inv[g][2], inv[g][3])))
            o_s[r0:r0 + BLOCK, g * KV_W:(g + 1) * KV_W] = (
                o_all[g * BLOCK:(g + 1) * BLOCK, :] * scale).astype(BF16)

    u = _dot(h, win_ref[:, C_CC:C_CX]) * _dot(h, win_ref[:, C_CX:C_Q])
    upad[8:rows + 8, :] = u

    cst_ref[...] = u[rows - 2:rows, :]
    yc =(cw_ref[0:1, :] * upad[6:rows + 6, :]
          + cw_ref[1:2, :] * upad[7:rows + 7, :]
          + cw_ref[2:3, :] * u)
    a_in = (_dot(h, win_ref[:, C_CB:C_CC]) * yc).astype(BF16)
    a_out = _dot(a_in, wco_ref[...])
    gated_a = jax.nn.sigmoid(_dot(h, win_ref[:, C_GA:C_GB])) * a_out

    att = _dot(o_s[...], wao_ref[...])
    gate_b = jax.nn.sigmoid(_dot(h, win_ref[:, C_GB:C_END]))
    merged = (gated_a + gate_b * att).astype(BF16)
    y_ref[...] = x + _rms(_dot(merged, wo_ref[...]), g_ref[3:4, :])


def _mixer_prompt(x, batch, norm_g, w_in, w_q, conv_w, w_co, w_ao, w_o, bt,
                  sinks, layer, jobs):
    m = x.shape[0]
    seq = m // batch
    rows = MIX_ROWS
    steps = seq // rows
    c_in, c_out, c_shapes = _cast_specs(jobs, lambda b, i: b * steps + i)
    kern = functools.partial(_mixer_prompt_kernel, layer=layer, rows=rows,
                             cast_blocks=tuple(j.n_blocks for j in jobs))
    outs = pl.pallas_call(
        kern,
        grid=(batch, steps),
        in_specs=[
            pl.BlockSpec((rows, D_MODEL), lambda b, i: (b * steps + i, 0)),
            _layer_resident((N_NORMS, D_MODEL), layer),
            _resident((D_MODEL, C_END)),
            _layer_resident((D_MODEL, D_MODEL), layer),
            _layer_resident((3, D_MODEL), layer),
            _resident((D_MODEL, D_MODEL)),
            _resident((D_MODEL, D_MODEL)),
            _resident((D_MODEL, D_MODEL)),
            _resident((2, GROUP * BLOCK, N_KV * BLOCK)),
            pl.BlockSpec(memory_space=pltpu.SMEM),
        ] + c_in,
        out_specs=[
            pl.BlockSpec((rows, D_MODEL), lambda b, i: (b * steps + i, 0)),
            pl.BlockSpec((None, 2, D_MODEL), lambda b, i: (b, 0, 0)),
            pl.BlockSpec((None, BLOCK, KV_W), lambda b, i: (b, 0, 0)),
            pl.BlockSpec((None, BLOCK, KV_W), lambda b, i: (b, 0, 0)),
        ] + c_out,
        out_shape=[
            jax.ShapeDtypeStruct((m, D_MODEL), F32),
            jax.ShapeDtypeStruct((batch, 2, D_MODEL), F32),
            jax.ShapeDtypeStruct((batch, BLOCK, KV_W), F32),
            jax.ShapeDtypeStruct((batch, BLOCK, KV_W), F32),
        ] + c_shapes,
        scratch_shapes=[
            pltpu.VMEM((rows, D_MODEL), BF16),
            pltpu.VMEM((N_KV, rows + BLOCK, KV_W), BF16),
            pltpu.VMEM((N_KV, rows + BLOCK, KV_W), BF16),
            pltpu.VMEM((rows + 8, D_MODEL), F32),
            pltpu.VMEM((rows, D_MODEL), BF16),
            pltpu.VMEM((rows // BLOCK, GROUP * BLOCK, N_KV * 2 * BLOCK),
                       BF16),
        ],
        compiler_params=pltpu.CompilerParams(
            dimension_semantics=("arbitrary", "arbitrary"),
            vmem_limit_bytes=VMEM_LIMIT_BYTES,
        ),
        name="mixer_prompt",
    )(x, norm_g, w_in, w_q, conv_w, w_co, w_ao, w_o, bt, sinks,
      *[j.src for j in jobs])
    return outs[0], outs[1], outs[2], outs[3], outs[4:]


def _mixer_decode_kernel(*refs, n_alias):
    (x_ref, st_ref, kc_ref, vc_ref, g_ref, win_ref, wq_ref, cw_ref, wco_ref,
     wao_ref, wo_ref, bdec_ref, sink_ref) = refs[:13]
    (y_ref, nst_ref, kout_ref, vout_ref, h_s, q_s, knew_s, vnew_s, knewt_s,
     vnewt_s, ag_s, o_s) = refs[13 + n_alias:]
    i = pl.program_id(0)
    last = pl.num_programs(0) - 1

    @pl.when(i == 0)
    def _():
        x = x_ref[...]
        h = _rms(x, g_ref[2:3, :]).astype(BF16)
        h_s[...] = h
        u = _dot(h, win_ref[:, C_CC:C_CX]) * _dot(h, win_ref[:, C_CX:C_Q])
        st0 = st_ref[:, 0:D_MODEL]
        st1 = st_ref[:, D_MODEL:2 * D_MODEL]
        yc = cw_ref[0:1, :] * st0 + cw_ref[1:2, :] * st1 + cw_ref[2:3, :] * u
        nst_ref[:, 0:D_MODEL] = st1
        nst_ref[:, D_MODEL:2 * D_MODEL] = u
        a_in = (_dot(h, win_ref[:, C_CB:C_CC]) * yc).astype(BF16)
        a_out = _dot(a_in, wco_ref[...])
        ag_s[...] = jax.nn.sigmoid(_dot(h, win_ref[:, C_GA:C_GB])) * a_out
        q_s[...] = _dot(h, wq_ref[...]) * (HEAD_DIM ** -0.5)
        knew = _dot(h, win_ref[:, C_K:C_V])
        vnew = _dot(h, win_ref[:, C_V:C_GA])
        knew_s[...] = knew
        vnew_s[...] = vnew
        knewt_s[...] = knew.T
        vnewt_s[...] = vnew.T

    row = lax.broadcasted_iota(jnp.int32, (N_HEADS, KV_W), 0)
    lane = lax.broadcasted_iota(jnp.int32, (N_HEADS, KV_W), 1)
    row_g = row // N_KV
    seg = (lane // HEAD_DIM) == (row % N_KV)
    key0 = lax.broadcasted_iota(jnp.int32, (N_HEADS, WINDOW), 1) == 0
    newest = lax.broadcasted_iota(jnp.int32, (KV_W, WINDOW), 1) == WINDOW - 1
    bias = bdec_ref[...]
    sink = sink_ref[...]

    for b in range(DEC_SEQS):
        to_last = (WINDOW - 1) - (i * DEC_SEQS + b)
        kout_ref[b] = jnp.where(
            newest, pltpu.roll(knewt_s[...], to_last, axis=1),
            pltpu.roll(kc_ref[b], WINDOW - 1, axis=1))
        vout_ref[b] = jnp.where(
            newest, pltpu.roll(vnewt_s[...], to_last, axis=1),
            pltpu.roll(vc_ref[b], WINDOW - 1, axis=1))

    qms, k_rows, v_rows = [], [], []
    for b in range(DEC_SEQS):
        r = i * DEC_SEQS + b
        q_row = q_s[pl.ds(r, 1), :]
        pieces = [jnp.broadcast_to(q_row[:, g * KV_W:(g + 1) * KV_W],
                                   (N_HEADS, KV_W)) for g in range(GROUP)]
        q_rows = jnp.where(row_g == 0, pieces[0],
                           jnp.where(row_g == 1, pieces[1],
                                     jnp.where(row_g == 2, pieces[2],
                                               pieces[3])))
        qms.append(jnp.where(seg, q_rows, 0.0))
        k_rows.append(knew_s[pl.ds(r, 1), :])
        v_rows.append(vnew_s[pl.ds(r, 1), :])
    qm = jnp.stack(qms).astype(BF16)
    k_row = jnp.stack(k_rows).astype(BF16).astype(F32)
    v_row = jnp.stack(v_rows).astype(BF16).astype(F32)
    s_new = jnp.sum(qm.astype(F32) * k_row, axis=-1, keepdims=True)
    s_old = jnp.einsum('bqd,bdk->bqk', qm, kc_ref[...].astype(BF16),
                       preferred_element_type=F32)
    s = jnp.where(key0, s_new, s_old) + bias
    m = jnp.maximum(jnp.max(s, axis=-1, keepdims=True), sink)
    p = jnp.exp(s - m)
    den = jnp.sum(p, axis=-1, keepdims=True) + jnp.exp(sink - m)
    p16 = p.astype(BF16)
    p_new = p16[:, :, 0:1].astype(F32)
    p_old = jnp.where(key0, jnp.zeros_like(p16), p16)
    o = jnp.einsum('bqk,bdk->bqd', p_old, vc_ref[...].astype(BF16),
                   preferred_element_type=F32)
    om = jnp.where(seg, (o + p_new * v_row) / den, 0.0)
    for b in range(DEC_SEQS):
        r = i * DEC_SEQS + b
        for g in range(GROUP):
            o_s[pl.ds(r, 1), g * KV_W:(g + 1) * KV_W] = jnp.sum(
                om[b, g * N_KV:(g + 1) * N_KV, :], axis=0, keepdims=True)

    @pl.when(i == last)
    def _():
        att = _dot(o_s[...].astype(BF16), wao_ref[...])
        gate_b = jax.nn.sigmoid(_dot(h_s[...], win_ref[:, C_GB:C_END]))
        merged = (ag_s[...] + gate_b * att).astype(BF16)
        y_ref[...] = x_ref[...] + _rms(_dot(merged, wo_ref[...]),
                                       g_ref[3:4, :])


def _mixer_decode(x, st, kc, vc, norm_g, w_in, w_q, conv_w, w_co, w_ao, w_o,
                  bdec, sink_col, layer, prev_caches):
    n = x.shape[0]
    sb = DEC_SEQS
    cache_spec = pl.BlockSpec((None, sb, KV_W, WINDOW),
                              lambda i: (layer, i, 0, 0))
    n_alias = len(prev_caches)
    n_in = 13
    return pl.pallas_call(
        functools.partial(_mixer_decode_kernel, n_alias=n_alias),
        grid=(n // sb,),
        in_specs=[
            _resident((n, D_MODEL)),
            _resident((n, 2 * D_MODEL)),
            cache_spec,
            cache_spec,
            _layer_resident((N_NORMS, D_MODEL), layer),
            _resident((D_MODEL, C_END)),
            _layer_resident((D_MODEL, D_MODEL), layer),
            _layer_resident((3, D_MODEL), layer),
            _resident((D_MODEL, D_MODEL)),
            _resident((D_MODEL, D_MODEL)),
            _resident((D_MODEL, D_MODEL)),
            _resident((N_HEADS, WINDOW)),
            _resident((N_HEADS, 1)),
        ] + [pl.BlockSpec(memory_space=pl.ANY)] * n_alias,
        out_specs=[
            pl.BlockSpec((n, D_MODEL), lambda i: (0, 0)),
            pl.BlockSpec((n, 2 * D_MODEL), lambda i: (0, 0)),
            cache_spec,
            cache_spec,
        ],
        out_shape=[
            jax.ShapeDtypeStruct((n, D_MODEL), F32),
            jax.ShapeDtypeStruct((n, 2 * D_MODEL), F32),
            jax.ShapeDtypeStruct(kc.shape, F32),
            jax.ShapeDtypeStruct(vc.shape, F32),
        ],
        input_output_aliases={n_in + a: 2 + a for a in range(n_alias)},
        scratch_shapes=[
            pltpu.VMEM((n, D_MODEL), BF16),
            pltpu.VMEM((n, D_MODEL), F32),
            pltpu.VMEM((n, KV_W), F32),
            pltpu.VMEM((n, KV_W), F32),
            pltpu.VMEM((KV_W, n), F32),
            pltpu.VMEM((KV_W, n), F32),
            pltpu.VMEM((n, D_MODEL), F32),
            pltpu.VMEM((n, D_MODEL), F32),
        ],
        compiler_params=pltpu.CompilerParams(
            dimension_semantics=("arbitrary",),
            vmem_limit_bytes=VMEM_LIMIT_BYTES),
        name="mixer_decode",
    )(x, st, kc, vc, norm_g, w_in, w_q, conv_w, w_co, w_ao, w_o, bdec,
      sink_col, *prev_caches)


def _heads_g_major(w, axis):
    shape = w.shape
    split = shape[:axis] + (N_KV, GROUP, HEAD_DIM) + shape[axis + 1:]
    perm = list(range(len(split)))
    perm[axis], perm[axis + 1] = perm[axis + 1], perm[axis]
    return w.reshape(split).transpose(perm).reshape(shape)


def kernel(x_prompt, x_sample, state_conv, cache_k_win, cache_v_win, rel_bias,
           norm_g, w_ff1_gu, w_ff1_down, w_in, conv_w, sinks, w_conv_out,
           w_attn_out, w_out, w_ff2_gu, w_ff2_down):
    batch, seq, _ = x_prompt.shape
    n_dec = x_sample.shape[0]

    bt_plain, bt, ffn_w = _bias_table(
        rel_bias, _ffn_cast_jobs(w_ff1_gu, w_ff1_down, 0))
    bdec = bt_plain[0].reshape(GROUP, BLOCK, N_KV, BLOCK)[:, BLOCK - 1]
    bdec = bdec.reshape(N_HEADS, WINDOW)
    bdec = jnp.concatenate([bdec[:, WINDOW - 1:], bdec[:, :WINDOW - 1]], axis=1)

    def keys_minor(c):
        return jnp.transpose(c, (0, 1, 3, 4, 2)).reshape(
            DEPTH, n_dec, KV_W, WINDOW)

    def keys_major(c):
        return jnp.transpose(
            c.reshape(DEPTH, n_dec, N_KV, HEAD_DIM, WINDOW), (0, 1, 4, 2, 3))

    kct = keys_minor(cache_k_win)
    vct = keys_minor(cache_v_win)

    wq = _heads_g_major(w_in[:, :, C_Q:C_K], 2).astype(BF16)

    xp = x_prompt.reshape(batch * seq, D_MODEL)
    xs = x_sample.reshape(n_dec, D_MODEL)
    pc, pk, pv, sc = [], [], [], []
    new_caches = ()
    for l in range(DEPTH):
        sink_col = sinks[l].reshape(N_KV, GROUP).T.reshape(N_HEADS, 1)

        xp, xs, mix_w = _ffn(
            xp, xs, norm_g, ffn_w[0], ffn_w[1], l, 0,
            _mixer_cast_jobs(w_in, w_conv_out, w_attn_out, w_out, l))
        win, wco, wao, wo = mix_w

        xp, c1, k1, v1, ffn_w = _mixer_prompt(
            xp, batch, norm_g, win, wq, conv_w, wco, wao, wo, bt, sinks, l,
            _ffn_cast_jobs(w_ff2_gu, w_ff2_down, l))
        xs, c2, k2, v2 = _mixer_decode(
            xs, state_conv[l].reshape(n_dec, 2 * D_MODEL), kct, vct,
            norm_g, win, wq, conv_w, wco, wao, wo, bdec, sink_col, l,
            new_caches)
        new_caches = (k2, v2)

        next_jobs = (_ffn_cast_jobs(w_ff1_gu, w_ff1_down, l + 1)
                     if l + 1 < DEPTH else [])
        xp, xs, ffn_w = _ffn(xp, xs, norm_g, ffn_w[0], ffn_w[1], l, 4,
                             next_jobs)

        pc.append(c1)
        pk.append(k1.reshape(batch, WINDOW, N_KV, HEAD_DIM))
        pv.append(v1.reshape(batch, WINDOW, N_KV, HEAD_DIM))
        sc.append(c2.reshape(n_dec, 2, D_MODEL))

    return (xp.reshape(batch, seq, D_MODEL), xs.reshape(n_dec, 1, D_MODEL),
            jnp.stack(pc), jnp.stack(pk), jnp.stack(pv),
            jnp.stack(sc), keys_major(new_caches[0]),
            keys_major(new_caches[1]))
```

```python
import functools
from typing import Callable, NamedTuple

import numpy as np
import jax
import jax.numpy as jnp
from jax import lax
from jax.experimental import pallas as pl
from jax.experimental.pallas import tpu as pltpu

F32 = jnp.float32
BF16 = jnp.bfloat16

D_MODEL = 1024
D_FF = 2816
N_HEADS = 16
N_KV = 4
GROUP = N_HEADS // N_KV
HEAD_DIM = 64
KV_W = N_KV * HEAD_DIM
WINDOW = 128
BLOCK = 128
N_BUCKETS = 32
MAX_DISTANCE = 128
RMS_EPS = 1e-6
NEG = -1e30
LOG2E = 1.4426950408889634
DEPTH = 2
N_NORMS = 6

C_CB, C_CC, C_CX, C_Q, C_K, C_V, C_GA, C_GB, C_END = (
    0, 1024, 2048, 3072, 4096, 4352, 4608, 5632, 6656)

VMEM_LIMIT_BYTES = 56 * 1024 * 1024

FFN_ROWS = 1024
FFN_SUB_ROWS = 512
FFN_CHUNK = 256
MIX_ROWS = 512
DEC_SEQS = 8
BIAS_STEPS = 2 * N_HEADS


def _t5_bucket_np(rel):
    n = np.maximum(rel, 0)
    max_exact = N_BUCKETS // 2
    nf = np.maximum(n, 1).astype(np.float32)
    large = max_exact + (
        np.log(nf / max_exact) / np.log(MAX_DISTANCE / max_exact)
        * (N_BUCKETS - max_exact)).astype(np.int32)
    large = np.minimum(large, N_BUCKETS - 1)
    return np.where(n < max_exact, n, large).astype(np.int32)


def _band_tables():
    qi = np.arange(BLOCK)[:, None]
    kj = np.arange(BLOCK)[None, :]
    rel = np.where(kj > qi, qi + BLOCK - kj, qi - kj)
    bucket = _t5_bucket_np(rel)
    mask = np.stack([np.ones_like(rel, bool), kj <= qi]).astype(np.float32)
    return bucket, mask


_BUCKET_NP, _MASK_NP = _band_tables()


def _rms(x, g):
    r = lax.rsqrt(jnp.mean(x * x, axis=-1, keepdims=True) + RMS_EPS)
    return (x * r) * g


def _dot(a, b):
    return jnp.dot(a, b, preferred_element_type=F32)


def _dot_t(a, b):
    return lax.dot_general(a, b, (((1,), (1,)), ((), ())),
                           preferred_element_type=F32)


def _resident(shape):
    return pl.BlockSpec(shape, lambda *_: (0,) * len(shape),
                        pipeline_mode=pl.Buffered(1))


def _layer_resident(shape, layer):
    return pl.BlockSpec((None,) + tuple(shape),
                        lambda *_: (layer,) + (0,) * len(shape),
                        pipeline_mode=pl.Buffered(1))


class _CastJob(NamedTuple):
    src: jax.Array
    layer: int
    rows: int
    src_block: Callable = lambda d: d

    @property
    def n_blocks(self):
        return self.src.shape[1] // self.rows


def _cast_rows(total_rows, host_steps, align=16):
    rows = align
    while total_rows % rows or total_rows // rows > host_steps:
        rows += align
    return rows


def _cast_specs(jobs, step_of, host_steps):
    in_specs, out_specs, out_shapes = [], [], []
    for job in jobs:
        assert job.n_blocks <= host_steps, (job.n_blocks, host_steps)
        cols = job.src.shape[2]

        def dst(*ids, job=job):
            return jnp.minimum(step_of(*ids), job.n_blocks - 1)

        in_specs.append(pl.BlockSpec(
            (None, job.rows, cols),
            lambda *ids, job=job, dst=dst: (job.layer,
                                            job.src_block(dst(*ids)), 0)))
        out_specs.append(pl.BlockSpec(
            (job.rows, cols), lambda *ids, dst=dst: (dst(*ids), 0)))
        out_shapes.append(jax.ShapeDtypeStruct(job.src.shape[1:], BF16))
    return in_specs, out_specs, out_shapes


def _run_casts(in_refs, out_refs):
    for src, dst in zip(in_refs, out_refs):
        dst[...] = src[...].astype(BF16)


def _ffn_cast_jobs(w_gu, w_down, layer, host_steps):
    return [_CastJob(w_gu, layer, _cast_rows(D_MODEL, host_steps)),
            _CastJob(w_down, layer, _cast_rows(D_FF, host_steps))]


def _mixer_cast_jobs(w_in, w_co, w_ao, w_o, layer, host_steps):
    to_g_major = lambda d: (d % GROUP) * N_KV + d // GROUP
    rows = _cast_rows(D_MODEL, host_steps)
    assert N_HEADS <= host_steps
    return [_CastJob(w_in, layer, rows), _CastJob(w_co, layer, rows),
            _CastJob(w_ao, layer, HEAD_DIM, to_g_major),
            _CastJob(w_o, layer, rows)]


def _bias_kernel(*refs, cast_blocks):
    nj = len(cast_blocks)
    rb_ref, bucket_ref, mask_ref = refs[:3]
    cast_in = refs[3:3 + nj]
    plain_ref, exp2_ref = refs[3 + nj:5 + nj]
    cast_out = refs[5 + nj:]
    g = pl.program_id(1)
    hk = pl.program_id(2)
    _run_casts(cast_in, cast_out)
    h = hk * GROUP + g
    bucket = bucket_ref[...]

    def body(b, acc):
        return jnp.where(bucket == b, rb_ref[b, h], acc)

    bias = lax.fori_loop(0, N_BUCKETS, body, jnp.zeros((BLOCK, BLOCK), F32))
    valid = mask_ref[...] > 0.0
    plain_ref[...] = jnp.where(valid, bias, NEG)
    exp2_ref[...] = jnp.where(valid, bias * LOG2E, NEG)


def _bias_table(rel_bias, jobs):
    c_in, c_out, c_shapes = _cast_specs(
        jobs, lambda v, g, hk: (v * GROUP + g) * N_KV + hk, BIAS_STEPS)
    table_spec = pl.BlockSpec((None, BLOCK, BLOCK), lambda v, g, hk: (v, g, hk))
    table_shape = jax.ShapeDtypeStruct((2, GROUP * BLOCK, N_KV * BLOCK), F32)
    outs = pl.pallas_call(
        functools.partial(_bias_kernel,
                          cast_blocks=tuple(j.n_blocks for j in jobs)),
        grid=(2, GROUP, N_KV),
        in_specs=[
            pl.BlockSpec(memory_space=pltpu.SMEM),
            pl.BlockSpec((BLOCK, BLOCK), lambda v, g, hk: (0, 0)),
            pl.BlockSpec((None, BLOCK, BLOCK), lambda v, g, hk: (v, 0, 0)),
        ] + c_in,
        out_specs=[table_spec, table_spec] + c_out,
        out_shape=[table_shape, table_shape] + c_shapes,
        compiler_params=pltpu.CompilerParams(
            dimension_semantics=("arbitrary",) * 3),
        name="bias_table",
    )(rel_bias, jnp.asarray(_BUCKET_NP), jnp.asarray(_MASK_NP),
      *[j.src for j in jobs])
    return outs[0], outs[1], outs[2:]


def _ffn_rows(x, g_ref, wgu_ref, wd_ref, act_ref, gi):
    rows = x.shape[0]
    h = _rms(x, g_ref[gi:gi + 1, :]).astype(BF16)
    for c in range(D_FF // FFN_CHUNK):
        lo = c * FFN_CHUNK
        gate = _dot(h, wgu_ref[:, lo:lo + FFN_CHUNK])
        up = _dot(h, wgu_ref[:, D_FF + lo:D_FF + lo + FFN_CHUNK])
        act_ref[0:rows, lo:lo + FFN_CHUNK] = (
            gate * jax.nn.sigmoid(gate) * up).astype(BF16)
    y = _dot(act_ref[0:rows, :], wd_ref[...])
    return x + 0.5 * _rms(y, g_ref[gi + 1:gi + 2, :])


def _ffn_kernel(*refs, gi, prompt_steps, cast_blocks):
    nj = len(cast_blocks)
    xp_ref, xs_ref, g_ref, wgu_ref, wd_ref = refs[:5]
    cast_in = refs[5:5 + nj]
    op_ref, os_ref = refs[5 + nj:7 + nj]
    cast_out = refs[7 + nj:7 + 2 * nj]
    act_refs = refs[7 + 2 * nj:]
    i = pl.program_id(0)
    _run_casts(cast_in, cast_out)

    @pl.when(i < prompt_steps)
    def _():
        for t, act_ref in enumerate(act_refs):
            rs = slice(t * FFN_SUB_ROWS, (t + 1) * FFN_SUB_ROWS)
            op_ref[rs, :] = _ffn_rows(xp_ref[rs, :], g_ref, wgu_ref, wd_ref,
                                      act_ref, gi)

    @pl.when(i == prompt_steps)
    def _():
        os_ref[...] = _ffn_rows(xs_ref[...], g_ref, wgu_ref, wd_ref,
                                act_refs[0], gi)


def _ffn(xp, xs, norm_g, w_gu, w_down, layer, gi, jobs):
    m = xp.shape[0]
    n = xs.shape[0]
    tm = FFN_ROWS
    steps = m // tm
    row_block = lambda i: (jnp.minimum(i, steps - 1), 0)
    c_in, c_out, c_shapes = _cast_specs(jobs, lambda i: i, steps + 1)
    outs = pl.pallas_call(
        functools.partial(_ffn_kernel, gi=gi, prompt_steps=steps,
                          cast_blocks=tuple(j.n_blocks for j in jobs)),
        grid=(steps + 1,),
        in_specs=[
            pl.BlockSpec((tm, D_MODEL), row_block),
            _resident((n, D_MODEL)),
            _layer_resident((N_NORMS, D_MODEL), layer),
            _resident((D_MODEL, 2 * D_FF)),
            _resident((D_FF, D_MODEL)),
        ] + c_in,
        out_specs=[
            pl.BlockSpec((tm, D_MODEL), row_block),
            pl.BlockSpec((n, D_MODEL), lambda i: (0, 0)),
        ] + c_out,
        out_shape=[
            jax.ShapeDtypeStruct((m, D_MODEL), F32),
            jax.ShapeDtypeStruct((n, D_MODEL), F32),
        ] + c_shapes,
        scratch_shapes=[pltpu.VMEM((FFN_SUB_ROWS, D_FF), BF16)
                        for _ in range(FFN_ROWS // FFN_SUB_ROWS)],
        compiler_params=pltpu.CompilerParams(
            dimension_semantics=("arbitrary",),
            vmem_limit_bytes=VMEM_LIMIT_BYTES),
        name="ffn_half",
    )(xp, xs, norm_g, w_gu, w_down, *[j.src for j in jobs])
    return outs[0], outs[1], outs[2:]


def _lane_segment_mask(rows):
    lane = lax.broadcasted_iota(jnp.int32, (rows, KV_W), 1)
    return [(lane >= hk * HEAD_DIM) & (lane < (hk + 1) * HEAD_DIM)
            for hk in range(N_KV)]


def _mixer_prompt_kernel(*refs, layer, rows, cast_blocks):
    nj = len(cast_blocks)
    (x_ref, g_ref, win_ref, wq_ref, cw_ref, wco_ref, wao_ref, wo_ref, bt_ref,
     sinks_ref) = refs[:10]
    cast_in = refs[10:10 + nj]
    y_ref, cst_ref, kout_ref, vout_ref = refs[10 + nj:14 + nj]
    cast_out = refs[14 + nj:14 + 2 * nj]
    q_s, kbuf, vbuf, upad, o_s, p_s = refs[14 + 2 * nj:]
    i = pl.program_id(1)
    nq = rows // BLOCK

    @pl.when(i == 0)
    def _():
        upad[0:8, :] = jnp.zeros((8, D_MODEL), F32)
        kbuf[:, 0:BLOCK, :] = jnp.zeros((N_KV, BLOCK, KV_W), BF16)
        vbuf[:, 0:BLOCK, :] = jnp.zeros((N_KV, BLOCK, KV_W), BF16)

    @pl.when(i > 0)
    def _():
        upad[0:8, :] = upad[rows:rows + 8, :]
        kbuf[:, 0:BLOCK, :] = kbuf[:, rows:rows + BLOCK, :]
        vbuf[:, 0:BLOCK, :] = vbuf[:, rows:rows + BLOCK, :]

    _run_casts(cast_in, cast_out)
    x = x_ref[...]
    h = _rms(x, g_ref[2:3, :]).astype(BF16)

    q_s[...] = (_dot(h, wq_ref[...])
                * (HEAD_DIM ** -0.5 * LOG2E)).astype(BF16)
    k = _dot(h, win_ref[:, C_K:C_V])
    v = _dot(h, win_ref[:, C_V:C_GA])
    k16 = k.astype(BF16)
    v16 = v.astype(BF16)
    seg = _lane_segment_mask(rows)
    zero_kv = jnp.zeros((rows, KV_W), BF16)
    for hk in range(N_KV):
        kbuf[hk, BLOCK:BLOCK + rows, :] = jnp.where(seg[hk], k16, zero_kv)
        vbuf[hk, BLOCK:BLOCK + rows, :] = jnp.where(seg[hk], v16, zero_kv)

    kout_ref[...] = k[rows - BLOCK:rows, :]
    vout_ref[...] = v[rows - BLOCK:rows, :]

    q_idx = lax.broadcasted_iota(jnp.int32, (BLOCK, BLOCK), 0)
    j_idx = lax.broadcasted_iota(jnp.int32, (BLOCK, BLOCK), 1)
    from_prev = j_idx > q_idx
    prev16 = jnp.where(from_prev, 1.0, 0.0).astype(BF16)
    own16 = jnp.where(from_prev, 0.0, 1.0).astype(BF16)
    first = (i == 0).astype(jnp.int32)
    seg_q = _lane_segment_mask(BLOCK)

    for j in range(nq):
        r0 = j * BLOCK
        variant = first if j == 0 else 0
        qj = q_s[r0:r0 + BLOCK, :]
        kb = jnp.concatenate(
            [kbuf[hk, r0:r0 + 2 * BLOCK, :] for hk in range(N_KV)], axis=0)
        vb = jnp.concatenate(
            [vbuf[hk, r0:r0 + 2 * BLOCK, :] for hk in range(N_KV)], axis=0)
        qs = jnp.concatenate(
            [qj[:, g * KV_W:(g + 1) * KV_W] for g in range(GROUP)], axis=0)
        s_all = _dot_t(qs, kb)
        inv = []
        for g in range(GROUP):
            inv_g = []
            for hk in range(N_KV):
                rs = slice(g * BLOCK, (g + 1) * BLOCK)
                c0 = hk * 2 * BLOCK
                s = jnp.where(from_prev, s_all[rs, c0:c0 + BLOCK],
                              s_all[rs, c0 + BLOCK:c0 + 2 * BLOCK])
                s = s + bt_ref[variant, rs, hk * BLOCK:(hk + 1) * BLOCK]
                sink = sinks_ref[layer, hk * GROUP + g] * LOG2E
                m = jnp.maximum(jnp.max(s, axis=-1, keepdims=True), sink)
                p = jnp.exp2(s - m)
                p16 = p.astype(BF16)
                p_s[j, rs, c0:c0 + BLOCK] = p16 * prev16
                p_s[j, rs, c0 + BLOCK:c0 + 2 * BLOCK] = p16 * own16
                den = jnp.sum(p, axis=-1, keepdims=True) + jnp.exp2(sink - m)
                inv_g.append(1.0 / den)
            inv.append(inv_g)
        o_all = _dot(p_s[j], vb)
        for g in range(GROUP):
            scale = jnp.where(
                seg_q[0], inv[g][0],
                jnp.where(seg_q[1], inv[g][1],
                          jnp.where(seg_q[2], inv[g][2], inv[g][3])))
            o_s[r0:r0 + BLOCK, g * KV_W:(g + 1) * KV_W] = (
                o_all[g * BLOCK:(g + 1) * BLOCK, :] * scale).astype(BF16)

    u = _dot(h, win_ref[:, C_CC:C_CX]) * _dot(h, win_ref[:, C_CX:C_Q])
    upad[8:rows + 8, :] = u

    cst_ref[...] = u[rows - 2:rows, :]
    yc =(cw_ref[0:1, :] * upad[6:rows + 6, :]
          + cw_ref[1:2, :] * upad[7:rows + 7, :]
          + cw_ref[2:3, :] * u)
    a_in = (_dot(h, win_ref[:, C_CB:C_CC]) * yc).astype(BF16)
    a_out = _dot(a_in, wco_ref[...])
    gated_a = jax.nn.sigmoid(_dot(h, win_ref[:, C_GA:C_GB])) * a_out

    att = _dot(o_s[...], wao_ref[...])
    gate_b = jax.nn.sigmoid(_dot(h, win_ref[:, C_GB:C_END]))
    merged = (gated_a + gate_b * att).astype(BF16)
    y_ref[...] = x + _rms(_dot(merged, wo_ref[...]), g_ref[3:4, :])


def _mixer_prompt(x, batch, norm_g, w_in, w_q, conv_w, w_co, w_ao, w_o, bt,
                  sinks, layer, jobs):
    m = x.shape[0]
    seq = m // batch
    rows = MIX_ROWS
    steps = seq // rows
    c_in, c_out, c_shapes = _cast_specs(jobs, lambda b, i: b * steps + i,
                                        batch * steps)
    kern = functools.partial(_mixer_prompt_kernel, layer=layer, rows=rows,
                             cast_blocks=tuple(j.n_blocks for j in jobs))
    outs = pl.pallas_call(
        kern,
        grid=(batch, steps),
        in_specs=[
            pl.BlockSpec((rows, D_MODEL), lambda b, i: (b * steps + i, 0)),
            _layer_resident((N_NORMS, D_MODEL), layer),
            _resident((D_MODEL, C_END)),
            _layer_resident((D_MODEL, D_MODEL), layer),
            _layer_resident((3, D_MODEL), layer),
            _resident((D_MODEL, D_MODEL)),
            _resident((D_MODEL, D_MODEL)),
            _resident((D_MODEL, D_MODEL)),
            _resident((2, GROUP * BLOCK, N_KV * BLOCK)),
            pl.BlockSpec(memory_space=pltpu.SMEM),
        ] + c_in,
        out_specs=[
            pl.BlockSpec((rows, D_MODEL), lambda b, i: (b * steps + i, 0)),
            pl.BlockSpec((None, 2, D_MODEL), lambda b, i: (b, 0, 0)),
            pl.BlockSpec((None, BLOCK, KV_W), lambda b, i: (b, 0, 0)),
            pl.BlockSpec((None, BLOCK, KV_W), lambda b, i: (b, 0, 0)),
        ] + c_out,
        out_shape=[
            jax.ShapeDtypeStruct((m, D_MODEL), F32),
            jax.ShapeDtypeStruct((batch, 2, D_MODEL), F32),
            jax.ShapeDtypeStruct((batch, BLOCK, KV_W), F32),
            jax.ShapeDtypeStruct((batch, BLOCK, KV_W), F32),
        ] + c_shapes,
        scratch_shapes=[
            pltpu.VMEM((rows, D_MODEL), BF16),
            pltpu.VMEM((N_KV, rows + BLOCK, KV_W), BF16),
            pltpu.VMEM((N_KV, rows + BLOCK, KV_W), BF16),
            pltpu.VMEM((rows + 8, D_MODEL), F32),
            pltpu.VMEM((rows, D_MODEL), BF16),
            pltpu.VMEM((rows // BLOCK, GROUP * BLOCK, N_KV * 2 * BLOCK),
                       BF16),
        ],
        compiler_params=pltpu.CompilerParams(
            dimension_semantics=("arbitrary", "arbitrary"),
            vmem_limit_bytes=VMEM_LIMIT_BYTES,
        ),
        name="mixer_prompt",
    )(x, norm_g, w_in, w_q, conv_w, w_co, w_ao, w_o, bt, sinks,
      *[j.src for j in jobs])
    return outs[0], outs[1], outs[2], outs[3], outs[4:]


def _mixer_decode_kernel(*refs, n_alias):
    (x_ref, st_ref, kc_ref, vc_ref, g_ref, win_ref, wq_ref, cw_ref, wco_ref,
     wao_ref, wo_ref, bdec_ref, sink_ref) = refs[:13]
    (y_ref, nst_ref, kout_ref, vout_ref, h_s, q_s, knew_s, vnew_s, knewt_s,
     vnewt_s, ag_s, o_s) = refs[13 + n_alias:]
    i = pl.program_id(0)
    last = pl.num_programs(0) - 1

    @pl.when(i == 0)
    def _():
        x = x_ref[...]
        h = _rms(x, g_ref[2:3, :]).astype(BF16)
        h_s[...] = h
        u = _dot(h, win_ref[:, C_CC:C_CX]) * _dot(h, win_ref[:, C_CX:C_Q])
        st0 = st_ref[:, 0:D_MODEL]
        st1 = st_ref[:, D_MODEL:2 * D_MODEL]
        yc = cw_ref[0:1, :] * st0 + cw_ref[1:2, :] * st1 + cw_ref[2:3, :] * u
        nst_ref[:, 0:D_MODEL] = st1
        nst_ref[:, D_MODEL:2 * D_MODEL] = u
        a_in = (_dot(h, win_ref[:, C_CB:C_CC]) * yc).astype(BF16)
        a_out = _dot(a_in, wco_ref[...])
        ag_s[...] = jax.nn.sigmoid(_dot(h, win_ref[:, C_GA:C_GB])) * a_out
        q_s[...] = _dot(h, wq_ref[...]) * (HEAD_DIM ** -0.5)
        knew = _dot(h, win_ref[:, C_K:C_V])
        vnew = _dot(h, win_ref[:, C_V:C_GA])
        knew_s[...] = knew
        vnew_s[...] = vnew
        knewt_s[...] = knew.T
        vnewt_s[...] = vnew.T

    row = lax.broadcasted_iota(jnp.int32, (N_HEADS, KV_W), 0)
    lane = lax.broadcasted_iota(jnp.int32, (N_HEADS, KV_W), 1)
    row_g = row // N_KV
    seg = (lane // HEAD_DIM) == (row % N_KV)
    key0 = lax.broadcasted_iota(jnp.int32, (N_HEADS, WINDOW), 1) == 0
    newest = lax.broadcasted_iota(jnp.int32, (KV_W, WINDOW), 1) == WINDOW - 1
    bias = bdec_ref[...]
    sink = sink_ref[...]

    for b in range(DEC_SEQS):
        to_last = (WINDOW - 1) - (i * DEC_SEQS + b)
        kout_ref[b] = jnp.where(
            newest, pltpu.roll(knewt_s[...], to_last, axis=1),
            pltpu.roll(kc_ref[b], WINDOW - 1, axis=1))
        vout_ref[b] = jnp.where(
            newest, pltpu.roll(vnewt_s[...], to_last, axis=1),
            pltpu.roll(vc_ref[b], WINDOW - 1, axis=1))

    qms, k_rows, v_rows = [], [], []
    for b in range(DEC_SEQS):
        r = i * DEC_SEQS + b
        q_row = q_s[pl.ds(r, 1), :]
        pieces = [jnp.broadcast_to(q_row[:, g * KV_W:(g + 1) * KV_W],
                                   (N_HEADS, KV_W)) for g in range(GROUP)]
        q_rows = jnp.where(row_g == 0, pieces[0],
                           jnp.where(row_g == 1, pieces[1],
                                     jnp.where(row_g == 2, pieces[2],
                                               pieces[3])))
        qms.append(jnp.where(seg, q_rows, 0.0))
        k_rows.append(knew_s[pl.ds(r, 1), :])
        v_rows.append(vnew_s[pl.ds(r, 1), :])
    qm = jnp.stack(qms).astype(BF16)
    k_row = jnp.stack(k_rows).astype(BF16).astype(F32)
    v_row = jnp.stack(v_rows).astype(BF16).astype(F32)
    s_new = jnp.sum(qm.astype(F32) * k_row, axis=-1, keepdims=True)
    s_old = jnp.einsum('bqd,bdk->bqk', qm, kc_ref[...].astype(BF16),
                       preferred_element_type=F32)
    s = jnp.where(key0, s_new, s_old) + bias
    m = jnp.maximum(jnp.max(s, axis=-1, keepdims=True), sink)
    p = jnp.exp(s - m)
    den = jnp.sum(p, axis=-1, keepdims=True) + jnp.exp(sink - m)
    p16 = p.astype(BF16)
    p_new = p16[:, :, 0:1].astype(F32)
    p_old = jnp.where(key0, jnp.zeros_like(p16), p16)
    o = jnp.einsum('bqk,bdk->bqd', p_old, vc_ref[...].astype(BF16),
                   preferred_element_type=F32)
    om = jnp.where(seg, (o + p_new * v_row) / den, 0.0)
    for b in range(DEC_SEQS):
        r = i * DEC_SEQS + b
        for g in range(GROUP):
            o_s[pl.ds(r, 1), g * KV_W:(g + 1) * KV_W] = jnp.sum(
                om[b, g * N_KV:(g + 1) * N_KV, :], axis=0, keepdims=True)

    @pl.when(i == last)
    def _():
        att = _dot(o_s[...].astype(BF16), wao_ref[...])
        gate_b = jax.nn.sigmoid(_dot(h_s[...], win_ref[:, C_GB:C_END]))
        merged = (ag_s[...] + gate_b * att).astype(BF16)
        y_ref[...] = x_ref[...] + _rms(_dot(merged, wo_ref[...]),
                                       g_ref[3:4, :])


def _mixer_decode(x, st, kc, vc, norm_g, w_in, w_q, conv_w, w_co, w_ao, w_o,
                  bdec, sink_col, layer, prev_caches):
    n = x.shape[0]
    sb = DEC_SEQS
    cache_spec = pl.BlockSpec((None, sb, KV_W, WINDOW),
                              lambda i: (layer, i, 0, 0))
    n_alias = len(prev_caches)
    n_in = 13
    return pl.pallas_call(
        functools.partial(_mixer_decode_kernel, n_alias=n_alias),
        grid=(n // sb,),
        in_specs=[
            _resident((n, D_MODEL)),
            _resident((n, 2 * D_MODEL)),
            cache_spec,
            cache_spec,
            _layer_resident((N_NORMS, D_MODEL), layer),
            _resident((D_MODEL, C_END)),
            _layer_resident((D_MODEL, D_MODEL), layer),
            _layer_resident((3, D_MODEL), layer),
            _resident((D_MODEL, D_MODEL)),
            _resident((D_MODEL, D_MODEL)),
            _resident((D_MODEL, D_MODEL)),
            _resident((N_HEADS, WINDOW)),
            _resident((N_HEADS, 1)),
        ] + [pl.BlockSpec(memory_space=pl.ANY)] * n_alias,
        out_specs=[
            pl.BlockSpec((n, D_MODEL), lambda i: (0, 0)),
            pl.BlockSpec((n, 2 * D_MODEL), lambda i: (0, 0)),
            cache_spec,
            cache_spec,
        ],
        out_shape=[
            jax.ShapeDtypeStruct((n, D_MODEL), F32),
            jax.ShapeDtypeStruct((n, 2 * D_MODEL), F32),
            jax.ShapeDtypeStruct(kc.shape, F32),
            jax.ShapeDtypeStruct(vc.shape, F32),
        ],
        input_output_aliases={n_in + a: 2 + a for a in range(n_alias)},
        scratch_shapes=[
            pltpu.VMEM((n, D_MODEL), BF16),
            pltpu.VMEM((n, D_MODEL), F32),
            pltpu.VMEM((n, KV_W), F32),
            pltpu.VMEM((n, KV_W), F32),
            pltpu.VMEM((KV_W, n), F32),
            pltpu.VMEM((KV_W, n), F32),
            pltpu.VMEM((n, D_MODEL), F32),
            pltpu.VMEM((n, D_MODEL), F32),
        ],
        compiler_params=pltpu.CompilerParams(
            dimension_semantics=("arbitrary",),
            vmem_limit_bytes=VMEM_LIMIT_BYTES),
        name="mixer_decode",
    )(x, st, kc, vc, norm_g, w_in, w_q, conv_w, w_co, w_ao, w_o, bdec,
      sink_col, *prev_caches)


def _heads_g_major(w, axis):
    shape = w.shape
    split = shape[:axis] + (N_KV, GROUP, HEAD_DIM) + shape[axis + 1:]
    perm = list(range(len(split)))
    perm[axis], perm[axis + 1] = perm[axis + 1], perm[axis]
    return w.reshape(split).transpose(perm).reshape(shape)


def kernel(x_prompt, x_sample, state_conv, cache_k_win, cache_v_win, rel_bias,
           norm_g, w_ff1_gu, w_ff1_down, w_in, conv_w, sinks, w_conv_out,
           w_attn_out, w_out, w_ff2_gu, w_ff2_down):
    batch, seq, _ = x_prompt.shape
    n_dec = x_sample.shape[0]

    ffn_steps = (batch * seq) // FFN_ROWS + 1
    mix_steps = (batch * seq) // MIX_ROWS
    bt_plain, bt, ffn_w = _bias_table(
        rel_bias, _ffn_cast_jobs(w_ff1_gu, w_ff1_down, 0, BIAS_STEPS))
    bdec = bt_plain[0].reshape(GROUP, BLOCK, N_KV, BLOCK)[:, BLOCK - 1]
    bdec = bdec.reshape(N_HEADS, WINDOW)
    bdec = jnp.concatenate([bdec[:, WINDOW - 1:], bdec[:, :WINDOW - 1]], axis=1)

    def keys_minor(c):
        return jnp.transpose(c, (0, 1, 3, 4, 2)).reshape(
            DEPTH, n_dec, KV_W, WINDOW)

    def keys_major(c):
        return jnp.transpose(
            c.reshape(DEPTH, n_dec, N_KV, HEAD_DIM, WINDOW), (0, 1, 4, 2, 3))

    kct = keys_minor(cache_k_win)
    vct = keys_minor(cache_v_win)

    wq = _heads_g_major(w_in[:, :, C_Q:C_K], 2).astype(BF16)

    xp = x_prompt.reshape(batch * seq, D_MODEL)
    xs = x_sample.reshape(n_dec, D_MODEL)
    pc, pk, pv, sc = [], [], [], []
    new_caches = ()
    for l in range(DEPTH):
        sink_col = sinks[l].reshape(N_KV, GROUP).T.reshape(N_HEADS, 1)

        xp, xs, mix_w = _ffn(
            xp, xs, norm_g, ffn_w[0], ffn_w[1], l, 0,
            _mixer_cast_jobs(w_in, w_conv_out, w_attn_out, w_out, l,
                             ffn_steps))
        win, wco, wao, wo = mix_w

        xp, c1, k1, v1, ffn_w = _mixer_prompt(
            xp, batch, norm_g, win, wq, conv_w, wco, wao, wo, bt, sinks, l,
            _ffn_cast_jobs(w_ff2_gu, w_ff2_down, l, mix_steps))
        xs, c2, k2, v2 = _mixer_decode(
            xs, state_conv[l].reshape(n_dec, 2 * D_MODEL), kct, vct,
            norm_g, win, wq, conv_w, wco, wao, wo, bdec, sink_col, l,
            new_caches)
        new_caches = (k2, v2)

        next_jobs = (_ffn_cast_jobs(w_ff1_gu, w_ff1_down, l + 1, ffn_steps)
                     if l + 1 < DEPTH else [])
        xp, xs, ffn_w = _ffn(xp, xs, norm_g, ffn_w[0], ffn_w[1], l, 4,
                             next_jobs)

        pc.append(c1)
        pk.append(k1.reshape(batch, WINDOW, N_KV, HEAD_DIM))
        pv.append(v1.reshape(batch, WINDOW, N_KV, HEAD_DIM))
        sc.append(c2.reshape(n_dec, 2, D_MODEL))

    return (xp.reshape(batch, seq, D_MODEL), xs.reshape(n_dec, 1, D_MODEL),
            jnp.stack(pc), jnp.stack(pk), jnp.stack(pv),
            jnp.stack(sc), keys_major(new_caches[0]),
            keys_major(new_caches[1]))
```

```python
import functools
from typing import Callable, NamedTuple

import numpy as np
import jax
import jax.numpy as jnp
from jax import lax
from jax.experimental import pallas as pl
from jax.experimental.pallas import tpu as pltpu

F32 = jnp.float32
BF16 = jnp.bfloat16

D_MODEL = 1024
D_FF = 2816
N_HEADS = 16
N_KV = 4
GROUP = N_HEADS // N_KV
HEAD_DIM = 64
KV_W = N_KV * HEAD_DIM
WINDOW = 128
BLOCK = 128
N_BUCKETS = 32
MAX_DISTANCE = 128
RMS_EPS = 1e-6
NEG = -1e30
LOG2E = 1.4426950408889634
DEPTH = 2
N_NORMS = 6

C_CB, C_CC, C_CX, C_Q, C_K, C_V, C_GA, C_GB, C_END = (
    0, 1024, 2048, 3072, 4096, 4352, 4608, 5632, 6656)

VMEM_LIMIT_BYTES = 56 * 1024 * 1024

FFN_ROWS = 1024
FFN_SUB_ROWS = 512
FFN_CHUNK = 256
MIX_ROWS = 512
DEC_SEQS = 16
BIAS_STEPS = N_HEADS


def _t5_bucket_np(rel):
    n = np.maximum(rel, 0)
    max_exact = N_BUCKETS // 2
    nf = np.maximum(n, 1).astype(np.float32)
    large = max_exact + (
        np.log(nf / max_exact) / np.log(MAX_DISTANCE / max_exact)
        * (N_BUCKETS - max_exact)).astype(np.int32)
    large = np.minimum(large, N_BUCKETS - 1)
    return np.where(n < max_exact, n, large).astype(np.int32)


def _band_tables():
    qi = np.arange(BLOCK)[:, None]
    kj = np.arange(BLOCK)[None, :]
    rel = np.where(kj > qi, qi + BLOCK - kj, qi - kj)
    bucket = _t5_bucket_np(rel)
    mask = np.stack([np.ones_like(rel, bool), kj <= qi]).astype(np.float32)
    return bucket, mask


_BUCKET_NP, _MASK_NP = _band_tables()


def _rms(x, g):
    r = lax.rsqrt(jnp.mean(x * x, axis=-1, keepdims=True) + RMS_EPS)
    return (x * r) * g


def _dot(a, b):
    return jnp.dot(a, b, preferred_element_type=F32)


def _dot_t(a, b):
    return lax.dot_general(a, b, (((1,), (1,)), ((), ())),
                           preferred_element_type=F32)


def _resident(shape):
    return pl.BlockSpec(shape, lambda *_: (0,) * len(shape),
                        pipeline_mode=pl.Buffered(1))


def _layer_resident(shape, layer):
    return pl.BlockSpec((None,) + tuple(shape),
                        lambda *_: (layer,) + (0,) * len(shape),
                        pipeline_mode=pl.Buffered(1))


class _CastJob(NamedTuple):
    src: jax.Array
    layer: int
    rows: int
    src_block: Callable = lambda d: d

    @property
    def n_blocks(self):
        return self.src.shape[1] // self.rows


def _cast_rows(total_rows, host_steps, align=16):
    rows = align
    while total_rows % rows or total_rows // rows > host_steps:
        rows += align
    return rows


def _cast_specs(jobs, step_of, host_steps):
    in_specs, out_specs, out_shapes = [], [], []
    for job in jobs:
        assert job.n_blocks <= host_steps, (job.n_blocks, host_steps)
        cols = job.src.shape[2]

        def dst(*ids, job=job):
            return jnp.minimum(step_of(*ids), job.n_blocks - 1)

        in_specs.append(pl.BlockSpec(
            (None, job.rows, cols),
            lambda *ids, job=job, dst=dst: (job.layer,
                                            job.src_block(dst(*ids)), 0)))
        out_specs.append(pl.BlockSpec(
            (job.rows, cols), lambda *ids, dst=dst: (dst(*ids), 0)))
        out_shapes.append(jax.ShapeDtypeStruct(job.src.shape[1:], BF16))
    return in_specs, out_specs, out_shapes


def _run_casts(in_refs, out_refs):
    for src, dst in zip(in_refs, out_refs):
        dst[...] = src[...].astype(BF16)


def _ffn_cast_jobs(w_gu, w_down, layer, host_steps):
    return [_CastJob(w_gu, layer, _cast_rows(D_MODEL, host_steps)),
            _CastJob(w_down, layer, _cast_rows(D_FF, host_steps))]


def _mixer_cast_jobs(w_in, w_co, w_ao, w_o, layer, host_steps):
    to_g_major = lambda d: (d % GROUP) * N_KV + d // GROUP
    rows = _cast_rows(D_MODEL, host_steps)
    assert N_HEADS <= host_steps
    return [_CastJob(w_in, layer, rows), _CastJob(w_co, layer, rows),
            _CastJob(w_ao, layer, HEAD_DIM, to_g_major),
            _CastJob(w_o, layer, rows)]


def _bias_kernel(*refs, cast_blocks):
    nj = len(cast_blocks)
    rb_ref, bucket_ref, mask_ref = refs[:3]
    cast_in = refs[3:3 + nj]
    plain_ref, exp2_ref = refs[3 + nj:5 + nj]
    cast_out = refs[5 + nj:]
    g = pl.program_id(0)
    hk = pl.program_id(1)
    _run_casts(cast_in, cast_out)
    h = hk * GROUP + g
    bucket = bucket_ref[...]

    def body(b, acc):
        return jnp.where(bucket == b, rb_ref[b, h], acc)

    bias = lax.fori_loop(0, N_BUCKETS, body, jnp.zeros((BLOCK, BLOCK), F32))
    bias2 = bias * LOG2E
    for variant in range(2):
        valid = mask_ref[variant] > 0.0
        plain_ref[variant] = jnp.where(valid, bias, NEG)
        exp2_ref[variant] = jnp.where(valid, bias2, NEG)


def _bias_table(rel_bias, jobs):
    c_in, c_out, c_shapes = _cast_specs(
        jobs, lambda g, hk: g * N_KV + hk, BIAS_STEPS)
    table_spec = pl.BlockSpec((2, BLOCK, BLOCK), lambda g, hk: (0, g, hk))
    table_shape = jax.ShapeDtypeStruct((2, GROUP * BLOCK, N_KV * BLOCK), F32)
    outs = pl.pallas_call(
        functools.partial(_bias_kernel,
                          cast_blocks=tuple(j.n_blocks for j in jobs)),
        grid=(GROUP, N_KV),
        in_specs=[
            pl.BlockSpec(memory_space=pltpu.SMEM),
            pl.BlockSpec((BLOCK, BLOCK), lambda g, hk: (0, 0)),
            pl.BlockSpec((2, BLOCK, BLOCK), lambda g, hk: (0, 0, 0)),
        ] + c_in,
        out_specs=[table_spec, table_spec] + c_out,
        out_shape=[table_shape, table_shape] + c_shapes,
        compiler_params=pltpu.CompilerParams(
            dimension_semantics=("arbitrary",) * 2),
        name="bias_table",
    )(rel_bias, jnp.asarray(_BUCKET_NP), jnp.asarray(_MASK_NP),
      *[j.src for j in jobs])
    return outs[0], outs[1], outs[2:]


def _ffn_rows(x, g_ref, wgu_ref, wd_ref, act_ref, gi):
    rows = x.shape[0]
    h = _rms(x, g_ref[gi:gi + 1, :]).astype(BF16)
    for c in range(D_FF // FFN_CHUNK):
        lo = c * FFN_CHUNK
        gate = _dot(h, wgu_ref[:, lo:lo + FFN_CHUNK])
        up = _dot(h, wgu_ref[:, D_FF + lo:D_FF + lo + FFN_CHUNK])
        act_ref[0:rows, lo:lo + FFN_CHUNK] = (
            gate * jax.nn.sigmoid(gate) * up).astype(BF16)
    y = _dot(act_ref[0:rows, :], wd_ref[...])
    return x + 0.5 * _rms(y, g_ref[gi + 1:gi + 2, :])


def _ffn_kernel(*refs, gi, prompt_steps, cast_blocks):
    nj = len(cast_blocks)
    xp_ref, xs_ref, g_ref, wgu_ref, wd_ref = refs[:5]
    cast_in = refs[5:5 + nj]
    op_ref, os_ref = refs[5 + nj:7 + nj]
    cast_out = refs[7 + nj:7 + 2 * nj]
    act_refs = refs[7 + 2 * nj:]
    i = pl.program_id(0)
    _run_casts(cast_in, cast_out)

    @pl.when(i < prompt_steps)
    def _():
        for t, act_ref in enumerate(act_refs):
            rs = slice(t * FFN_SUB_ROWS, (t + 1) * FFN_SUB_ROWS)
            op_ref[rs, :] = _ffn_rows(xp_ref[rs, :], g_ref, wgu_ref, wd_ref,
                                      act_ref, gi)

    @pl.when(i == prompt_steps)
    def _():
        os_ref[...] = _ffn_rows(xs_ref[...], g_ref, wgu_ref, wd_ref,
                                act_refs[0], gi)


def _ffn(xp, xs, norm_g, w_gu, w_down, layer, gi, jobs):
    m = xp.shape[0]
    n = xs.shape[0]
    tm = FFN_ROWS
    steps = m // tm
    row_block = lambda i: (jnp.minimum(i, steps - 1), 0)
    c_in, c_out, c_shapes = _cast_specs(jobs, lambda i: i, steps + 1)
    outs = pl.pallas_call(
        functools.partial(_ffn_kernel, gi=gi, prompt_steps=steps,
                          cast_blocks=tuple(j.n_blocks for j in jobs)),
        grid=(steps + 1,),
        in_specs=[
            pl.BlockSpec((tm, D_MODEL), row_block),
            _resident((n, D_MODEL)),
            _layer_resident((N_NORMS, D_MODEL), layer),
            _resident((D_MODEL, 2 * D_FF)),
            _resident((D_FF, D_MODEL)),
        ] + c_in,
        out_specs=[
            pl.BlockSpec((tm, D_MODEL), row_block),
            pl.BlockSpec((n, D_MODEL), lambda i: (0, 0)),
        ] + c_out,
        out_shape=[
            jax.ShapeDtypeStruct((m, D_MODEL), F32),
            jax.ShapeDtypeStruct((n, D_MODEL), F32),
        ] + c_shapes,
        scratch_shapes=[pltpu.VMEM((FFN_SUB_ROWS, D_FF), BF16)
                        for _ in range(FFN_ROWS // FFN_SUB_ROWS)],
        compiler_params=pltpu.CompilerParams(
            dimension_semantics=("arbitrary",),
            vmem_limit_bytes=VMEM_LIMIT_BYTES),
        name="ffn_half",
    )(xp, xs, norm_g, w_gu, w_down, *[j.src for j in jobs])
    return outs[0], outs[1], outs[2:]


def _lane_segment_mask(rows):
    lane = lax.broadcasted_iota(jnp.int32, (rows, KV_W), 1)
    return [(lane >= hk * HEAD_DIM) & (lane < (hk + 1) * HEAD_DIM)
            for hk in range(N_KV)]


def _mixer_prompt_kernel(*refs, layer, rows, cast_blocks):
    nj = len(cast_blocks)
    (x_ref, g_ref, win_ref, wq_ref, cw_ref, wco_ref, wao_ref, wo_ref, bt_ref,
     sinks_ref) = refs[:10]
    cast_in = refs[10:10 + nj]
    y_ref, cst_ref, kout_ref, vout_ref = refs[10 + nj:14 + nj]
    cast_out = refs[14 + nj:14 + 2 * nj]
    q_s, kbuf, vbuf, upad, o_s, p_s = refs[14 + 2 * nj:]
    i = pl.program_id(1)
    nq = rows // BLOCK

    @pl.when(i == 0)
    def _():
        upad[0:8, :] = jnp.zeros((8, D_MODEL), F32)
        kbuf[:, 0:BLOCK, :] = jnp.zeros((N_KV, BLOCK, KV_W), BF16)
        vbuf[:, 0:BLOCK, :] = jnp.zeros((N_KV, BLOCK, KV_W), BF16)

    @pl.when(i > 0)
    def _():
        upad[0:8, :] = upad[rows:rows + 8, :]
        kbuf[:, 0:BLOCK, :] = kbuf[:, rows:rows + BLOCK, :]
        vbuf[:, 0:BLOCK, :] = vbuf[:, rows:rows + BLOCK, :]

    _run_casts(cast_in, cast_out)
    x = x_ref[...]
    h = _rms(x, g_ref[2:3, :]).astype(BF16)

    q_s[...] = (_dot(h, wq_ref[...])
                * (HEAD_DIM ** -0.5 * LOG2E)).astype(BF16)
    k = _dot(h, win_ref[:, C_K:C_V])
    v = _dot(h, win_ref[:, C_V:C_GA])
    k16 = k.astype(BF16)
    v16 = v.astype(BF16)
    seg = _lane_segment_mask(rows)
    zero_kv = jnp.zeros((rows, KV_W), BF16)
    for hk in range(N_KV):
        kbuf[hk, BLOCK:BLOCK + rows, :] = jnp.where(seg[hk], k16, zero_kv)
        vbuf[hk, BLOCK:BLOCK + rows, :] = jnp.where(seg[hk], v16, zero_kv)

    kout_ref[...] = k[rows - BLOCK:rows, :]
    vout_ref[...] = v[rows - BLOCK:rows, :]

    q_idx = lax.broadcasted_iota(jnp.int32, (BLOCK, BLOCK), 0)
    j_idx = lax.broadcasted_iota(jnp.int32, (BLOCK, BLOCK), 1)
    from_prev = j_idx > q_idx
    prev16 = jnp.where(from_prev, 1.0, 0.0).astype(BF16)
    own16 = jnp.where(from_prev, 0.0, 1.0).astype(BF16)
    first = (i == 0).astype(jnp.int32)
    seg_q = _lane_segment_mask(BLOCK)

    for j in range(nq):
        r0 = j * BLOCK
        variant = first if j == 0 else 0
        qj = q_s[r0:r0 + BLOCK, :]
        kb = jnp.concatenate(
            [kbuf[hk, r0:r0 + 2 * BLOCK, :] for hk in range(N_KV)], axis=0)
        vb = jnp.concatenate(
            [vbuf[hk, r0:r0 + 2 * BLOCK, :] for hk in range(N_KV)], axis=0)
        qs = jnp.concatenate(
            [qj[:, g * KV_W:(g + 1) * KV_W] for g in range(GROUP)], axis=0)
        s_all = _dot_t(qs, kb)
        inv = []
        for g in range(GROUP):
            inv_g = []
            for hk in range(N_KV):
                rs = slice(g * BLOCK, (g + 1) * BLOCK)
                c0 = hk * 2 * BLOCK
                s = jnp.where(from_prev, s_all[rs, c0:c0 + BLOCK],
                              s_all[rs, c0 + BLOCK:c0 + 2 * BLOCK])
                s = s + bt_ref[variant, rs, hk * BLOCK:(hk + 1) * BLOCK]
                sink = sinks_ref[layer, hk * GROUP + g] * LOG2E
                m = jnp.maximum(jnp.max(s, axis=-1, keepdims=True), sink)
                p = jnp.exp2(s - m)
                p16 = p.astype(BF16)
                p_s[j, rs, c0:c0 + BLOCK] = p16 * prev16
                p_s[j, rs, c0 + BLOCK:c0 + 2 * BLOCK] = p16 * own16
                den = jnp.sum(p, axis=-1, keepdims=True) + jnp.exp2(sink - m)
                inv_g.append(1.0 / den)
            inv.append(inv_g)
        o_all = _dot(p_s[j], vb)
        for g in range(GROUP):
            scale = jnp.where(
                seg_q[0], inv[g][0],
                jnp.where(seg_q[1], inv[g][1],
                          jnp.where(seg_q[2], inv[g][2], inv[g][3])))
            o_s[r0:r0 + BLOCK, g * KV_W:(g + 1) * KV_W] = (
                o_all[g * BLOCK:(g + 1) * BLOCK, :] * scale).astype(BF16)

    u = _dot(h, win_ref[:, C_CC:C_CX]) * _dot(h, win_ref[:, C_CX:C_Q])
    upad[8:rows + 8, :] = u

    cst_ref[...] = u[rows - 2:rows, :]
    yc =(cw_ref[0:1, :] * upad[6:rows + 6, :]
          + cw_ref[1:2, :] * upad[7:rows + 7, :]
          + cw_ref[2:3, :] * u)
    a_in = (_dot(h, win_ref[:, C_CB:C_CC]) * yc).astype(BF16)
    a_out = _dot(a_in, wco_ref[...])
    gated_a = jax.nn.sigmoid(_dot(h, win_ref[:, C_GA:C_GB])) * a_out

    att = _dot(o_s[...], wao_ref[...])
    gate_b = jax.nn.sigmoid(_dot(h, win_ref[:, C_GB:C_END]))
    merged = (gated_a + gate_b * att).astype(BF16)
    y_ref[...] = x + _rms(_dot(merged, wo_ref[...]), g_ref[3:4, :])


def _mixer_prompt(x, batch, norm_g, w_in, w_q, conv_w, w_co, w_ao, w_o, bt,
                  sinks, layer, jobs):
    m = x.shape[0]
    seq = m // batch
    rows = MIX_ROWS
    steps = seq // rows
    c_in, c_out, c_shapes = _cast_specs(jobs, lambda b, i: b * steps + i,
                                        batch * steps)
    kern = functools.partial(_mixer_prompt_kernel, layer=layer, rows=rows,
                             cast_blocks=tuple(j.n_blocks for j in jobs))
    outs = pl.pallas_call(
        kern,
        grid=(batch, steps),
        in_specs=[
            pl.BlockSpec((rows, D_MODEL), lambda b, i: (b * steps + i, 0)),
            _layer_resident((N_NORMS, D_MODEL), layer),
            _resident((D_MODEL, C_END)),
            _layer_resident((D_MODEL, D_MODEL), layer),
            _layer_resident((3, D_MODEL), layer),
            _resident((D_MODEL, D_MODEL)),
            _resident((D_MODEL, D_MODEL)),
            _resident((D_MODEL, D_MODEL)),
            _resident((2, GROUP * BLOCK, N_KV * BLOCK)),
            pl.BlockSpec(memory_space=pltpu.SMEM),
        ] + c_in,
        out_specs=[
            pl.BlockSpec((rows, D_MODEL), lambda b, i: (b * steps + i, 0)),
            pl.BlockSpec((None, 2, D_MODEL), lambda b, i: (b, 0, 0)),
            pl.BlockSpec((None, BLOCK, KV_W), lambda b, i: (b, 0, 0)),
            pl.BlockSpec((None, BLOCK, KV_W), lambda b, i: (b, 0, 0)),
        ] + c_out,
        out_shape=[
            jax.ShapeDtypeStruct((m, D_MODEL), F32),
            jax.ShapeDtypeStruct((batch, 2, D_MODEL), F32),
            jax.ShapeDtypeStruct((batch, BLOCK, KV_W), F32),
            jax.ShapeDtypeStruct((batch, BLOCK, KV_W), F32),
        ] + c_shapes,
        scratch_shapes=[
            pltpu.VMEM((rows, D_MODEL), BF16),
            pltpu.VMEM((N_KV, rows + BLOCK, KV_W), BF16),
            pltpu.VMEM((N_KV, rows + BLOCK, KV_W), BF16),
            pltpu.VMEM((rows + 8, D_MODEL), F32),
            pltpu.VMEM((rows, D_MODEL), BF16),
            pltpu.VMEM((rows // BLOCK, GROUP * BLOCK, N_KV * 2 * BLOCK),
                       BF16),
        ],
        compiler_params=pltpu.CompilerParams(
            dimension_semantics=("arbitrary", "arbitrary"),
            vmem_limit_bytes=VMEM_LIMIT_BYTES,
        ),
        name="mixer_prompt",
    )(x, norm_g, w_in, w_q, conv_w, w_co, w_ao, w_o, bt, sinks,
      *[j.src for j in jobs])
    return outs[0], outs[1], outs[2], outs[3], outs[4:]


def _mixer_decode_kernel(*refs, n_alias):
    (x_ref, st_ref, kc_ref, vc_ref, g_ref, win_ref, wq_ref, cw_ref, wco_ref,
     wao_ref, wo_ref, bdec_ref, sink_ref) = refs[:13]
    (y_ref, nst_ref, kout_ref, vout_ref, h_s, q_s, knew_s, vnew_s, knewt_s,
     vnewt_s, ag_s, o_s) = refs[13 + n_alias:]
    i = pl.program_id(0)
    last = pl.num_programs(0) - 1

    @pl.when(i == 0)
    def _():
        x = x_ref[...]
        h = _rms(x, g_ref[2:3, :]).astype(BF16)
        h_s[...] = h
        u = _dot(h, win_ref[:, C_CC:C_CX]) * _dot(h, win_ref[:, C_CX:C_Q])
        st0 = st_ref[:, 0:D_MODEL]
        st1 = st_ref[:, D_MODEL:2 * D_MODEL]
        yc = cw_ref[0:1, :] * st0 + cw_ref[1:2, :] * st1 + cw_ref[2:3, :] * u
        nst_ref[:, 0:D_MODEL] = st1
        nst_ref[:, D_MODEL:2 * D_MODEL] = u
        a_in = (_dot(h, win_ref[:, C_CB:C_CC]) * yc).astype(BF16)
        a_out = _dot(a_in, wco_ref[...])
        ag_s[...] = jax.nn.sigmoid(_dot(h, win_ref[:, C_GA:C_GB])) * a_out
        q_s[...] = _dot(h, wq_ref[...]) * (HEAD_DIM ** -0.5)
        knew = _dot(h, win_ref[:, C_K:C_V])
        vnew = _dot(h, win_ref[:, C_V:C_GA])
        knew_s[...] = knew
        vnew_s[...] = vnew
        knewt_s[...] = knew.T
        vnewt_s[...] = vnew.T

    row = lax.broadcasted_iota(jnp.int32, (N_HEADS, KV_W), 0)
    lane = lax.broadcasted_iota(jnp.int32, (N_HEADS, KV_W), 1)
    row_g = row // N_KV
    seg = (lane // HEAD_DIM) == (row % N_KV)
    key0 = lax.broadcasted_iota(jnp.int32, (N_HEADS, WINDOW), 1) == 0
    newest = lax.broadcasted_iota(jnp.int32, (KV_W, WINDOW), 1) == WINDOW - 1
    bias = bdec_ref[...]
    sink = sink_ref[...]

    for b in range(DEC_SEQS):
        to_last = (WINDOW - 1) - (i * DEC_SEQS + b)
        kout_ref[b] = jnp.where(
            newest, pltpu.roll(knewt_s[...], to_last, axis=1),
            pltpu.roll(kc_ref[b], WINDOW - 1, axis=1))
        vout_ref[b] = jnp.where(
            newest, pltpu.roll(vnewt_s[...], to_last, axis=1),
            pltpu.roll(vc_ref[b], WINDOW - 1, axis=1))

    qms, k_rows, v_rows = [], [], []
    for b in range(DEC_SEQS):
        r = i * DEC_SEQS + b
        q_row = q_s[pl.ds(r, 1), :]
        pieces = [jnp.broadcast_to(q_row[:, g * KV_W:(g + 1) * KV_W],
                                   (N_HEADS, KV_W)) for g in range(GROUP)]
        q_rows = jnp.where(row_g == 0, pieces[0],
                           jnp.where(row_g == 1, pieces[1],
                                     jnp.where(row_g == 2, pieces[2],
                                               pieces[3])))
        qms.append(jnp.where(seg, q_rows, 0.0))
        k_rows.append(knew_s[pl.ds(r, 1), :])
        v_rows.append(vnew_s[pl.ds(r, 1), :])
    qm = jnp.stack(qms).astype(BF16)
    k_row = jnp.stack(k_rows).astype(BF16).astype(F32)
    v_row = jnp.stack(v_rows).astype(BF16).astype(F32)
    s_new = jnp.sum(qm.astype(F32) * k_row, axis=-1, keepdims=True)
    s_old = jnp.einsum('bqd,bdk->bqk', qm, kc_ref[...].astype(BF16),
                       preferred_element_type=F32)
    s = jnp.where(key0, s_new, s_old) + bias
    m = jnp.maximum(jnp.max(s, axis=-1, keepdims=True), sink)
    p = jnp.exp(s - m)
    den = jnp.sum(p, axis=-1, keepdims=True) + jnp.exp(sink - m)
    p16 = p.astype(BF16)
    p_new = p16[:, :, 0:1].astype(F32)
    p_old = jnp.where(key0, jnp.zeros_like(p16), p16)
    o = jnp.einsum('bqk,bdk->bqd', p_old, vc_ref[...].astype(BF16),
                   preferred_element_type=F32)
    om = jnp.where(seg, (o + p_new * v_row) / den, 0.0)
    for b in range(DEC_SEQS):
        r = i * DEC_SEQS + b
        for g in range(GROUP):
            o_s[pl.ds(r, 1), g * KV_W:(g + 1) * KV_W] = jnp.sum(
                om[b, g * N_KV:(g + 1) * N_KV, :], axis=0, keepdims=True)

    @pl.when(i == last)
    def _():
        att = _dot(o_s[...].astype(BF16), wao_ref[...])
        gate_b = jax.nn.sigmoid(_dot(h_s[...], win_ref[:, C_GB:C_END]))
        merged = (ag_s[...] + gate_b * att).astype(BF16)
        y_ref[...] = x_ref[...] + _rms(_dot(merged, wo_ref[...]),
                                       g_ref[3:4, :])


def _mixer_decode(x, st, kc, vc, norm_g, w_in, w_q, conv_w, w_co, w_ao, w_o,
                  bdec, sink_col, layer, prev_caches):
    n = x.shape[0]
    sb = DEC_SEQS
    cache_spec = pl.BlockSpec((None, sb, KV_W, WINDOW),
                              lambda i: (layer, i, 0, 0))
    n_alias = len(prev_caches)
    n_in = 13
    return pl.pallas_call(
        functools.partial(_mixer_decode_kernel, n_alias=n_alias),
        grid=(n // sb,),
        in_specs=[
            _resident((n, D_MODEL)),
            _resident((n, 2 * D_MODEL)),
            cache_spec,
            cache_spec,
            _layer_resident((N_NORMS, D_MODEL), layer),
            _resident((D_MODEL, C_END)),
            _layer_resident((D_MODEL, D_MODEL), layer),
            _layer_resident((3, D_MODEL), layer),
            _resident((D_MODEL, D_MODEL)),
            _resident((D_MODEL, D_MODEL)),
            _resident((D_MODEL, D_MODEL)),
            _resident((N_HEADS, WINDOW)),
            _resident((N_HEADS, 1)),
        ] + [pl.BlockSpec(memory_space=pl.ANY)] * n_alias,
        out_specs=[
            pl.BlockSpec((n, D_MODEL), lambda i: (0, 0)),
            pl.BlockSpec((n, 2 * D_MODEL), lambda i: (0, 0)),
            cache_spec,
            cache_spec,
        ],
        out_shape=[
            jax.ShapeDtypeStruct((n, D_MODEL), F32),
            jax.ShapeDtypeStruct((n, 2 * D_MODEL), F32),
            jax.ShapeDtypeStruct(kc.shape, F32),
            jax.ShapeDtypeStruct(vc.shape, F32),
        ],
        input_output_aliases={n_in + a: 2 + a for a in range(n_alias)},
        scratch_shapes=[
            pltpu.VMEM((n, D_MODEL), BF16),
            pltpu.VMEM((n, D_MODEL), F32),
            pltpu.VMEM((n, KV_W), F32),
            pltpu.VMEM((n, KV_W), F32),
            pltpu.VMEM((KV_W, n), F32),
            pltpu.VMEM((KV_W, n), F32),
            pltpu.VMEM((n, D_MODEL), F32),
            pltpu.VMEM((n, D_MODEL), F32),
        ],
        compiler_params=pltpu.CompilerParams(
            dimension_semantics=("arbitrary",),
            vmem_limit_bytes=VMEM_LIMIT_BYTES),
        name="mixer_decode",
    )(x, st, kc, vc, norm_g, w_in, w_q, conv_w, w_co, w_ao, w_o, bdec,
      sink_col, *prev_caches)


def _heads_g_major(w, axis):
    shape = w.shape
    split = shape[:axis] + (N_KV, GROUP, HEAD_DIM) + shape[axis + 1:]
    perm = list(range(len(split)))
    perm[axis], perm[axis + 1] = perm[axis + 1], perm[axis]
    return w.reshape(split).transpose(perm).reshape(shape)


def kernel(x_prompt, x_sample, state_conv, cache_k_win, cache_v_win, rel_bias,
           norm_g, w_ff1_gu, w_ff1_down, w_in, conv_w, sinks, w_conv_out,
           w_attn_out, w_out, w_ff2_gu, w_ff2_down):
    batch, seq, _ = x_prompt.shape
    n_dec = x_sample.shape[0]

    ffn_steps = (batch * seq) // FFN_ROWS + 1
    mix_steps = (batch * seq) // MIX_ROWS
    bt_plain, bt, ffn_w = _bias_table(
        rel_bias, _ffn_cast_jobs(w_ff1_gu, w_ff1_down, 0, BIAS_STEPS))
    bdec = bt_plain[0].reshape(GROUP, BLOCK, N_KV, BLOCK)[:, BLOCK - 1]
    bdec = bdec.reshape(N_HEADS, WINDOW)
    bdec = jnp.concatenate([bdec[:, WINDOW - 1:], bdec[:, :WINDOW - 1]], axis=1)

    def keys_minor(c):
        return jnp.transpose(c, (0, 1, 3, 4, 2)).reshape(
            DEPTH, n_dec, KV_W, WINDOW)

    def keys_major(c):
        return jnp.transpose(
            c.reshape(DEPTH, n_dec, N_KV, HEAD_DIM, WINDOW), (0, 1, 4, 2, 3))

    kct = keys_minor(cache_k_win)
    vct = keys_minor(cache_v_win)

    wq = _heads_g_major(w_in[:, :, C_Q:C_K], 2).astype(BF16)

    xp = x_prompt.reshape(batch * seq, D_MODEL)
    xs = x_sample.reshape(n_dec, D_MODEL)
    pc, pk, pv, sc = [], [], [], []
    new_caches = ()
    for l in range(DEPTH):
        sink_col = sinks[l].reshape(N_KV, GROUP).T.reshape(N_HEADS, 1)

        xp, xs, mix_w = _ffn(
            xp, xs, norm_g, ffn_w[0], ffn_w[1], l, 0,
            _mixer_cast_jobs(w_in, w_conv_out, w_attn_out, w_out, l,
                             ffn_steps))
        win, wco, wao, wo = mix_w

        xp, c1, k1, v1, ffn_w = _mixer_prompt(
            xp, batch, norm_g, win, wq, conv_w, wco, wao, wo, bt, sinks, l,
            _ffn_cast_jobs(w_ff2_gu, w_ff2_down, l, mix_steps))
        xs, c2, k2, v2 = _mixer_decode(
            xs, state_conv[l].reshape(n_dec, 2 * D_MODEL), kct, vct,
            norm_g, win, wq, conv_w, wco, wao, wo, bdec, sink_col, l,
            new_caches)
        new_caches = (k2, v2)

        next_jobs = (_ffn_cast_jobs(w_ff1_gu, w_ff1_down, l + 1, ffn_steps)
                     if l + 1 < DEPTH else [])
        xp, xs, ffn_w = _ffn(xp, xs, norm_g, ffn_w[0], ffn_w[1], l, 4,
                             next_jobs)

        pc.append(c1)
        pk.append(k1.reshape(batch, WINDOW, N_KV, HEAD_DIM))
        pv.append(v1.reshape(batch, WINDOW, N_KV, HEAD_DIM))
        sc.append(c2.reshape(n_dec, 2, D_MODEL))

    return (xp.reshape(batch, seq, D_MODEL), xs.reshape(n_dec, 1, D_MODEL),
            jnp.stack(pc), jnp.stack(pk), jnp.stack(pv),
            jnp.stack(sc), keys_major(new_caches[0]),
            keys_major(new_caches[1]))
```

```python
import functools
from typing import Callable, NamedTuple

import numpy as np
import jax
import jax.numpy as jnp
from jax import lax
from jax.experimental import pallas as pl
from jax.experimental.pallas import tpu as pltpu

F32 = jnp.float32
BF16 = jnp.bfloat16

D_MODEL = 1024
D_FF = 2816
N_HEADS = 16
N_KV = 4
GROUP = N_HEADS // N_KV
HEAD_DIM = 64
KV_W = N_KV * HEAD_DIM
WINDOW = 128
BLOCK = 128
N_BUCKETS = 32
MAX_DISTANCE = 128
RMS_EPS = 1e-6
NEG = -1e30
LOG2E = 1.4426950408889634
DEPTH = 2
N_NORMS = 6

C_CB, C_CC, C_CX, C_Q, C_K, C_V, C_GA, C_GB, C_END = (
    0, 1024, 2048, 3072, 4096, 4352, 4608, 5632, 6656)

VMEM_LIMIT_BYTES = 56 * 1024 * 1024

FFN_ROWS = 1024
FFN_SUB_ROWS = 512
FFN_CHUNK = 256
MIX_ROWS = 512
DEC_SEQS = 16
BIAS_STEPS = N_HEADS


def _t5_bucket_np(rel):
    n = np.maximum(rel, 0)
    max_exact = N_BUCKETS // 2
    nf = np.maximum(n, 1).astype(np.float32)
    large = max_exact + (
        np.log(nf / max_exact) / np.log(MAX_DISTANCE / max_exact)
        * (N_BUCKETS - max_exact)).astype(np.int32)
    large = np.minimum(large, N_BUCKETS - 1)
    return np.where(n < max_exact, n, large).astype(np.int32)


def _band_tables():
    qi = np.arange(BLOCK)[:, None]
    kj = np.arange(BLOCK)[None, :]
    rel = np.where(kj > qi, qi + BLOCK - kj, qi - kj)
    bucket = _t5_bucket_np(rel)
    mask = np.stack([np.ones_like(rel, bool), kj <= qi]).astype(np.float32)
    return bucket, mask


_BUCKET_NP, _MASK_NP = _band_tables()


def _head_perm():
    old = np.arange(D_MODEL)
    hk, g, d = old // KV_W, (old // HEAD_DIM) % GROUP, old % HEAD_DIM
    perm = np.zeros((D_MODEL, D_MODEL), np.float32)
    perm[old, g * KV_W + hk * HEAD_DIM + d] = 1.0
    return perm


_HEAD_PERM_NP = _head_perm()


def _rms(x, g):
    r = lax.rsqrt(jnp.mean(x * x, axis=-1, keepdims=True) + RMS_EPS)
    return (x * r) * g


def _dot(a, b):
    return jnp.dot(a, b, preferred_element_type=F32)


def _dot_t(a, b):
    return lax.dot_general(a, b, (((1,), (1,)), ((), ())),
                           preferred_element_type=F32)


def _resident(shape):
    return pl.BlockSpec(shape, lambda *_: (0,) * len(shape),
                        pipeline_mode=pl.Buffered(1))


def _layer_resident(shape, layer):
    return pl.BlockSpec((None,) + tuple(shape),
                        lambda *_: (layer,) + (0,) * len(shape),
                        pipeline_mode=pl.Buffered(1))


class _CastJob(NamedTuple):
    src: jax.Array
    layer: int
    rows: int
    src_block: Callable = lambda d: d
    col_block: int | None = None
    head_perm: bool = False

    @property
    def n_blocks(self):
        return self.src.shape[1] // self.rows


def _cast_rows(total_rows, host_steps, align=16):
    rows = align
    while total_rows % rows or total_rows // rows > host_steps:
        rows += align
    return rows


def _cast_specs(jobs, step_of, host_steps):
    in_specs, out_specs, out_shapes = [], [], []
    for job in jobs:
        assert job.n_blocks <= host_steps, (job.n_blocks, host_steps)
        cols = job.src.shape[2] if job.col_block is None else D_MODEL
        col = job.col_block or 0

        def dst(*ids, job=job):
            return jnp.minimum(step_of(*ids), job.n_blocks - 1)

        in_specs.append(pl.BlockSpec(
            (None, job.rows, cols),
            lambda *ids, job=job, dst=dst, col=col: (
                job.layer, job.src_block(dst(*ids)), col)))
        out_specs.append(pl.BlockSpec(
            (job.rows, cols), lambda *ids, dst=dst: (dst(*ids), 0)))
        out_shapes.append(
            jax.ShapeDtypeStruct((job.src.shape[1], cols), BF16))
    return in_specs, out_specs, out_shapes


def _run_casts(in_refs, out_refs, head_perm=None, perm_ref=None):
    for n, (src, dst) in enumerate(zip(in_refs, out_refs)):
        w = src[...].astype(BF16)
        if head_perm is not None and head_perm[n]:
            w = _dot(w, perm_ref[...]).astype(BF16)
        dst[...] = w


def _ffn_cast_jobs(w_gu, w_down, layer, host_steps):
    return [_CastJob(w_gu, layer, _cast_rows(D_MODEL, host_steps)),
            _CastJob(w_down, layer, _cast_rows(D_FF, host_steps))]


def _mixer_cast_jobs(w_in, w_co, w_ao, w_o, layer, host_steps):
    to_g_major = lambda d: (d % GROUP) * N_KV + d // GROUP
    rows = _cast_rows(D_MODEL, host_steps)
    assert N_HEADS <= host_steps
    return [_CastJob(w_in, layer, rows), _CastJob(w_co, layer, rows),
            _CastJob(w_ao, layer, HEAD_DIM, to_g_major),
            _CastJob(w_o, layer, rows),
            _CastJob(w_in, layer, rows, col_block=C_Q // D_MODEL,
                     head_perm=True)]


def _bias_kernel(*refs, cast_blocks):
    nj = len(cast_blocks)
    rb_ref, bucket_ref, mask_ref = refs[:3]
    cast_in = refs[3:3 + nj]
    plain_ref, exp2_ref = refs[3 + nj:5 + nj]
    cast_out = refs[5 + nj:]
    g = pl.program_id(0)
    hk = pl.program_id(1)
    _run_casts(cast_in, cast_out)
    h = hk * GROUP + g
    bucket = bucket_ref[...]

    def body(b, acc):
        return jnp.where(bucket == b, rb_ref[b, h], acc)

    bias = lax.fori_loop(0, N_BUCKETS, body, jnp.zeros((BLOCK, BLOCK), F32))
    bias2 = bias * LOG2E
    for variant in range(2):
        valid = mask_ref[variant] > 0.0
        plain_ref[variant] = jnp.where(valid, bias, NEG)
        exp2_ref[variant] = jnp.where(valid, bias2, NEG)


def _bias_table(rel_bias, jobs):
    c_in, c_out, c_shapes = _cast_specs(
        jobs, lambda g, hk: g * N_KV + hk, BIAS_STEPS)
    table_spec = pl.BlockSpec((2, BLOCK, BLOCK), lambda g, hk: (0, g, hk))
    table_shape = jax.ShapeDtypeStruct((2, GROUP * BLOCK, N_KV * BLOCK), F32)
    outs = pl.pallas_call(
        functools.partial(_bias_kernel,
                          cast_blocks=tuple(j.n_blocks for j in jobs)),
        grid=(GROUP, N_KV),
        in_specs=[
            pl.BlockSpec(memory_space=pltpu.SMEM),
            pl.BlockSpec((BLOCK, BLOCK), lambda g, hk: (0, 0)),
            pl.BlockSpec((2, BLOCK, BLOCK), lambda g, hk: (0, 0, 0)),
        ] + c_in,
        out_specs=[table_spec, table_spec] + c_out,
        out_shape=[table_shape, table_shape] + c_shapes,
        compiler_params=pltpu.CompilerParams(
            dimension_semantics=("arbitrary",) * 2),
        name="bias_table",
    )(rel_bias, jnp.asarray(_BUCKET_NP), jnp.asarray(_MASK_NP),
      *[j.src for j in jobs])
    return outs[0], outs[1], outs[2:]


def _ffn_rows(x, g_ref, wgu_ref, wd_ref, act_ref, gi):
    rows = x.shape[0]
    h = _rms(x, g_ref[gi:gi + 1, :]).astype(BF16)
    for c in range(D_FF // FFN_CHUNK):
        lo = c * FFN_CHUNK
        gate = _dot(h, wgu_ref[:, lo:lo + FFN_CHUNK])
        up = _dot(h, wgu_ref[:, D_FF + lo:D_FF + lo + FFN_CHUNK])
        act_ref[0:rows, lo:lo + FFN_CHUNK] = (
            gate * jax.nn.sigmoid(gate) * up).astype(BF16)
    y = _dot(act_ref[0:rows, :], wd_ref[...])
    return x + 0.5 * _rms(y, g_ref[gi + 1:gi + 2, :])


def _ffn_kernel(*refs, gi, prompt_steps, cast_perm):
    nj = len(cast_perm)
    xp_ref, xs_ref, g_ref, wgu_ref, wd_ref, perm_ref = refs[:6]
    cast_in = refs[6:6 + nj]
    op_ref, os_ref = refs[6 + nj:8 + nj]
    cast_out = refs[8 + nj:8 + 2 * nj]
    act_refs = refs[8 + 2 * nj:]
    i = pl.program_id(0)
    _run_casts(cast_in, cast_out, cast_perm, perm_ref)

    @pl.when(i < prompt_steps)
    def _():
        for t, act_ref in enumerate(act_refs):
            rs = slice(t * FFN_SUB_ROWS, (t + 1) * FFN_SUB_ROWS)
            op_ref[rs, :] = _ffn_rows(xp_ref[rs, :], g_ref, wgu_ref, wd_ref,
                                      act_ref, gi)

    @pl.when(i == prompt_steps)
    def _():
        os_ref[...] = _ffn_rows(xs_ref[...], g_ref, wgu_ref, wd_ref,
                                act_refs[0], gi)


def _ffn(xp, xs, norm_g, w_gu, w_down, layer, gi, jobs):
    m = xp.shape[0]
    n = xs.shape[0]
    tm = FFN_ROWS
    steps = m // tm
    row_block = lambda i: (jnp.minimum(i, steps - 1), 0)
    c_in, c_out, c_shapes = _cast_specs(jobs, lambda i: i, steps + 1)
    outs = pl.pallas_call(
        functools.partial(_ffn_kernel, gi=gi, prompt_steps=steps,
                          cast_perm=tuple(j.head_perm for j in jobs)),
        grid=(steps + 1,),
        in_specs=[
            pl.BlockSpec((tm, D_MODEL), row_block),
            _resident((n, D_MODEL)),
            _layer_resident((N_NORMS, D_MODEL), layer),
            _resident((D_MODEL, 2 * D_FF)),
            _resident((D_FF, D_MODEL)),
            _resident((D_MODEL, D_MODEL)),
        ] + c_in,
        out_specs=[
            pl.BlockSpec((tm, D_MODEL), row_block),
            pl.BlockSpec((n, D_MODEL), lambda i: (0, 0)),
        ] + c_out,
        out_shape=[
            jax.ShapeDtypeStruct((m, D_MODEL), F32),
            jax.ShapeDtypeStruct((n, D_MODEL), F32),
        ] + c_shapes,
        scratch_shapes=[pltpu.VMEM((FFN_SUB_ROWS, D_FF), BF16)
                        for _ in range(FFN_ROWS // FFN_SUB_ROWS)],
        compiler_params=pltpu.CompilerParams(
            dimension_semantics=("arbitrary",),
            vmem_limit_bytes=VMEM_LIMIT_BYTES),
        name="ffn_half",
    )(xp, xs, norm_g, w_gu, w_down, jnp.asarray(_HEAD_PERM_NP, BF16),
      *[j.src for j in jobs])
    return outs[0], outs[1], outs[2:]


def _lane_segment_mask(rows):
    lane = lax.broadcasted_iota(jnp.int32, (rows, KV_W), 1)
    return [(lane >= hk * HEAD_DIM) & (lane < (hk + 1) * HEAD_DIM)
            for hk in range(N_KV)]


def _mixer_prompt_kernel(*refs, layer, rows, cast_blocks):
    nj = len(cast_blocks)
    (x_ref, g_ref, win_ref, wq_ref, cw_ref, wco_ref, wao_ref, wo_ref, bt_ref,
     sinks_ref) = refs[:10]
    cast_in = refs[10:10 + nj]
    y_ref, cst_ref, kout_ref, vout_ref = refs[10 + nj:14 + nj]
    cast_out = refs[14 + nj:14 + 2 * nj]
    q_s, kbuf, vbuf, upad, o_s, p_s = refs[14 + 2 * nj:]
    i = pl.program_id(1)
    nq = rows // BLOCK

    @pl.when(i == 0)
    def _():
        upad[0:8, :] = jnp.zeros((8, D_MODEL), F32)
        kbuf[:, 0:BLOCK, :] = jnp.zeros((N_KV, BLOCK, KV_W), BF16)
        vbuf[:, 0:BLOCK, :] = jnp.zeros((N_KV, BLOCK, KV_W), BF16)

    @pl.when(i > 0)
    def _():
        upad[0:8, :] = upad[rows:rows + 8, :]
        kbuf[:, 0:BLOCK, :] = kbuf[:, rows:rows + BLOCK, :]
        vbuf[:, 0:BLOCK, :] = vbuf[:, rows:rows + BLOCK, :]

    _run_casts(cast_in, cast_out)
    x = x_ref[...]
    h = _rms(x, g_ref[2:3, :]).astype(BF16)

    q_s[...] = (_dot(h, wq_ref[...])
                * (HEAD_DIM ** -0.5 * LOG2E)).astype(BF16)
    k = _dot(h, win_ref[:, C_K:C_V])
    v = _dot(h, win_ref[:, C_V:C_GA])
    k16 = k.astype(BF16)
    v16 = v.astype(BF16)
    seg = _lane_segment_mask(rows)
    zero_kv = jnp.zeros((rows, KV_W), BF16)
    for hk in range(N_KV):
        kbuf[hk, BLOCK:BLOCK + rows, :] = jnp.where(seg[hk], k16, zero_kv)
        vbuf[hk, BLOCK:BLOCK + rows, :] = jnp.where(seg[hk], v16, zero_kv)

    kout_ref[...] = k[rows - BLOCK:rows, :]
    vout_ref[...] = v[rows - BLOCK:rows, :]

    q_idx = lax.broadcasted_iota(jnp.int32, (BLOCK, BLOCK), 0)
    j_idx = lax.broadcasted_iota(jnp.int32, (BLOCK, BLOCK), 1)
    from_prev = j_idx > q_idx
    prev16 = jnp.where(from_prev, 1.0, 0.0).astype(BF16)
    own16 = jnp.where(from_prev, 0.0, 1.0).astype(BF16)
    first = (i == 0).astype(jnp.int32)
    seg_q = _lane_segment_mask(BLOCK)

    for j in range(nq):
        r0 = j * BLOCK
        variant = first if j == 0 else 0
        qj = q_s[r0:r0 + BLOCK, :]
        kb = jnp.concatenate(
            [kbuf[hk, r0:r0 + 2 * BLOCK, :] for hk in range(N_KV)], axis=0)
        vb = jnp.concatenate(
            [vbuf[hk, r0:r0 + 2 * BLOCK, :] for hk in range(N_KV)], axis=0)
        qs = jnp.concatenate(
            [qj[:, g * KV_W:(g + 1) * KV_W] for g in range(GROUP)], axis=0)
        s_all = _dot_t(qs, kb)
        inv = []
        for g in range(GROUP):
            inv_g = []
            for hk in range(N_KV):
                rs = slice(g * BLOCK, (g + 1) * BLOCK)
                c0 = hk * 2 * BLOCK
                s = jnp.where(from_prev, s_all[rs, c0:c0 + BLOCK],
                              s_all[rs, c0 + BLOCK:c0 + 2 * BLOCK])
                s = s + bt_ref[variant, rs, hk * BLOCK:(hk + 1) * BLOCK]
                sink = sinks_ref[layer, hk * GROUP + g] * LOG2E
                m = jnp.maximum(jnp.max(s, axis=-1, keepdims=True), sink)
                p = jnp.exp2(s - m)
                p16 = p.astype(BF16)
                p_s[j, rs, c0:c0 + BLOCK] = p16 * prev16
                p_s[j, rs, c0 + BLOCK:c0 + 2 * BLOCK] = p16 * own16
                den = jnp.sum(p, axis=-1, keepdims=True) + jnp.exp2(sink - m)
                inv_g.append(1.0 / den)
            inv.append(inv_g)
        o_all = _dot(p_s[j], vb)
        for g in range(GROUP):
            scale = jnp.where(
                seg_q[0], inv[g][0],
                jnp.where(seg_q[1], inv[g][1],
                          jnp.where(seg_q[2], inv[g][2], inv[g][3])))
            o_s[r0:r0 + BLOCK, g * KV_W:(g + 1) * KV_W] = (
                o_all[g * BLOCK:(g + 1) * BLOCK, :] * scale).astype(BF16)

    u = _dot(h, win_ref[:, C_CC:C_CX]) * _dot(h, win_ref[:, C_CX:C_Q])
    upad[8:rows + 8, :] = u

    cst_ref[...] = u[rows - 2:rows, :]
    yc =(cw_ref[0:1, :] * upad[6:rows + 6, :]
          + cw_ref[1:2, :] * upad[7:rows + 7, :]
          + cw_ref[2:3, :] * u)
    a_in = (_dot(h, win_ref[:, C_CB:C_CC]) * yc).astype(BF16)
    a_out = _dot(a_in, wco_ref[...])
    gated_a = jax.nn.sigmoid(_dot(h, win_ref[:, C_GA:C_GB])) * a_out

    att = _dot(o_s[...], wao_ref[...])
    gate_b = jax.nn.sigmoid(_dot(h, win_ref[:, C_GB:C_END]))
    merged = (gated_a + gate_b * att).astype(BF16)
    y_ref[...] = x + _rms(_dot(merged, wo_ref[...]), g_ref[3:4, :])


def _mixer_prompt(x, batch, norm_g, w_in, w_q, conv_w, w_co, w_ao, w_o, bt,
                  sinks, layer, jobs):
    m = x.shape[0]
    seq = m // batch
    rows = MIX_ROWS
    steps = seq // rows
    c_in, c_out, c_shapes = _cast_specs(jobs, lambda b, i: b * steps + i,
                                        batch * steps)
    kern = functools.partial(_mixer_prompt_kernel, layer=layer, rows=rows,
                             cast_blocks=tuple(j.n_blocks for j in jobs))
    outs = pl.pallas_call(
        kern,
        grid=(batch, steps),
        in_specs=[
            pl.BlockSpec((rows, D_MODEL), lambda b, i: (b * steps + i, 0)),
            _layer_resident((N_NORMS, D_MODEL), layer),
            _resident((D_MODEL, C_END)),
            _resident((D_MODEL, D_MODEL)),
            _layer_resident((3, D_MODEL), layer),
            _resident((D_MODEL, D_MODEL)),
            _resident((D_MODEL, D_MODEL)),
            _resident((D_MODEL, D_MODEL)),
            _resident((2, GROUP * BLOCK, N_KV * BLOCK)),
            pl.BlockSpec(memory_space=pltpu.SMEM),
        ] + c_in,
        out_specs=[
            pl.BlockSpec((rows, D_MODEL), lambda b, i: (b * steps + i, 0)),
            pl.BlockSpec((None, 2, D_MODEL), lambda b, i: (b, 0, 0)),
            pl.BlockSpec((None, BLOCK, KV_W), lambda b, i: (b, 0, 0)),
            pl.BlockSpec((None, BLOCK, KV_W), lambda b, i: (b, 0, 0)),
        ] + c_out,
        out_shape=[
            jax.ShapeDtypeStruct((m, D_MODEL), F32),
            jax.ShapeDtypeStruct((batch, 2, D_MODEL), F32),
            jax.ShapeDtypeStruct((batch, BLOCK, KV_W), F32),
            jax.ShapeDtypeStruct((batch, BLOCK, KV_W), F32),
        ] + c_shapes,
        scratch_shapes=[
            pltpu.VMEM((rows, D_MODEL), BF16),
            pltpu.VMEM((N_KV, rows + BLOCK, KV_W), BF16),
            pltpu.VMEM((N_KV, rows + BLOCK, KV_W), BF16),
            pltpu.VMEM((rows + 8, D_MODEL), F32),
            pltpu.VMEM((rows, D_MODEL), BF16),
            pltpu.VMEM((rows // BLOCK, GROUP * BLOCK, N_KV * 2 * BLOCK),
                       BF16),
        ],
        compiler_params=pltpu.CompilerParams(
            dimension_semantics=("arbitrary", "arbitrary"),
            vmem_limit_bytes=VMEM_LIMIT_BYTES,
        ),
        name="mixer_prompt",
    )(x, norm_g, w_in, w_q, conv_w, w_co, w_ao, w_o, bt, sinks,
      *[j.src for j in jobs])
    return outs[0], outs[1], outs[2], outs[3], outs[4:]


def _mixer_decode_kernel(*refs, n_alias):
    (x_ref, st_ref, kc_ref, vc_ref, g_ref, win_ref, wq_ref, cw_ref, wco_ref,
     wao_ref, wo_ref, bdec_ref, sink_ref) = refs[:13]
    (y_ref, nst_ref, kout_ref, vout_ref, h_s, q_s, knew_s, vnew_s, knewt_s,
     vnewt_s, ag_s, o_s) = refs[13 + n_alias:]
    i = pl.program_id(0)
    last = pl.num_programs(0) - 1

    @pl.when(i == 0)
    def _():
        x = x_ref[...]
        h = _rms(x, g_ref[2:3, :]).astype(BF16)
        h_s[...] = h
        u = _dot(h, win_ref[:, C_CC:C_CX]) * _dot(h, win_ref[:, C_CX:C_Q])
        st0 = st_ref[:, 0:D_MODEL]
        st1 = st_ref[:, D_MODEL:2 * D_MODEL]
        yc = cw_ref[0:1, :] * st0 + cw_ref[1:2, :] * st1 + cw_ref[2:3, :] * u
        nst_ref[:, 0:D_MODEL] = st1
        nst_ref[:, D_MODEL:2 * D_MODEL] = u
        a_in = (_dot(h, win_ref[:, C_CB:C_CC]) * yc).astype(BF16)
        a_out = _dot(a_in, wco_ref[...])
        ag_s[...] = jax.nn.sigmoid(_dot(h, win_ref[:, C_GA:C_GB])) * a_out
        q_s[...] = _dot(h, wq_ref[...]) * (HEAD_DIM ** -0.5)
        knew = _dot(h, win_ref[:, C_K:C_V])
        vnew = _dot(h, win_ref[:, C_V:C_GA])
        knew_s[...] = knew
        vnew_s[...] = vnew
        knewt_s[...] = knew.T
        vnewt_s[...] = vnew.T

    row = lax.broadcasted_iota(jnp.int32, (N_HEADS, KV_W), 0)
    lane = lax.broadcasted_iota(jnp.int32, (N_HEADS, KV_W), 1)
    row_g = row // N_KV
    seg = (lane // HEAD_DIM) == (row % N_KV)
    key0 = lax.broadcasted_iota(jnp.int32, (N_HEADS, WINDOW), 1) == 0
    newest = lax.broadcasted_iota(jnp.int32, (KV_W, WINDOW), 1) == WINDOW - 1
    bias = bdec_ref[...]
    sink = sink_ref[...]

    for b in range(DEC_SEQS):
        to_last = (WINDOW - 1) - (i * DEC_SEQS + b)
        kout_ref[b] = jnp.where(
            newest, pltpu.roll(knewt_s[...], to_last, axis=1),
            pltpu.roll(kc_ref[b], WINDOW - 1, axis=1))
        vout_ref[b] = jnp.where(
            newest, pltpu.roll(vnewt_s[...], to_last, axis=1),
            pltpu.roll(vc_ref[b], WINDOW - 1, axis=1))

    qms, k_rows, v_rows = [], [], []
    for b in range(DEC_SEQS):
        r = i * DEC_SEQS + b
        q_row = q_s[pl.ds(r, 1), :]
        pieces = [jnp.broadcast_to(q_row[:, g * KV_W:(g + 1) * KV_W],
                                   (N_HEADS, KV_W)) for g in range(GROUP)]
        q_rows = jnp.where(row_g == 0, pieces[0],
                           jnp.where(row_g == 1, pieces[1],
                                     jnp.where(row_g == 2, pieces[2],
                                               pieces[3])))
        qms.append(jnp.where(seg, q_rows, 0.0))
        k_rows.append(knew_s[pl.ds(r, 1), :])
        v_rows.append(vnew_s[pl.ds(r, 1), :])
    qm = jnp.stack(qms).astype(BF16)
    k_row = jnp.stack(k_rows).astype(BF16).astype(F32)
    v_row = jnp.stack(v_rows).astype(BF16).astype(F32)
    s_new = jnp.sum(qm.astype(F32) * k_row, axis=-1, keepdims=True)
    s_old = jnp.einsum('bqd,bdk->bqk', qm, kc_ref[...].astype(BF16),
                       preferred_element_type=F32)
    s = jnp.where(key0, s_new, s_old) + bias
    m = jnp.maximum(jnp.max(s, axis=-1, keepdims=True), sink)
    p = jnp.exp(s - m)
    den = jnp.sum(p, axis=-1, keepdims=True) + jnp.exp(sink - m)
    p16 = p.astype(BF16)
    p_new = p16[:, :, 0:1].astype(F32)
    p_old = jnp.where(key0, jnp.zeros_like(p16), p16)
    o = jnp.einsum('bqk,bdk->bqd', p_old, vc_ref[...].astype(BF16),
                   preferred_element_type=F32)
    om = jnp.where(seg, (o + p_new * v_row) / den, 0.0)
    for b in range(DEC_SEQS):
        r = i * DEC_SEQS + b
        for g in range(GROUP):
            o_s[pl.ds(r, 1), g * KV_W:(g + 1) * KV_W] = jnp.sum(
                om[b, g * N_KV:(g + 1) * N_KV, :], axis=0, keepdims=True)

    @pl.when(i == last)
    def _():
        att = _dot(o_s[...].astype(BF16), wao_ref[...])
        gate_b = jax.nn.sigmoid(_dot(h_s[...], win_ref[:, C_GB:C_END]))
        merged = (ag_s[...] + gate_b * att).astype(BF16)
        y_ref[...] = x_ref[...] + _rms(_dot(merged, wo_ref[...]),
                                       g_ref[3:4, :])


def _mixer_decode(x, st, kc, vc, norm_g, w_in, w_q, conv_w, w_co, w_ao, w_o,
                  bdec, sink_col, layer, prev_caches):
    n = x.shape[0]
    sb = DEC_SEQS
    cache_spec = pl.BlockSpec((None, sb, KV_W, WINDOW),
                              lambda i: (layer, i, 0, 0))
    n_alias = len(prev_caches)
    n_in = 13
    return pl.pallas_call(
        functools.partial(_mixer_decode_kernel, n_alias=n_alias),
        grid=(n // sb,),
        in_specs=[
            _resident((n, D_MODEL)),
            _resident((n, 2 * D_MODEL)),
            cache_spec,
            cache_spec,
            _layer_resident((N_NORMS, D_MODEL), layer),
            _resident((D_MODEL, C_END)),
            _resident((D_MODEL, D_MODEL)),
            _layer_resident((3, D_MODEL), layer),
            _resident((D_MODEL, D_MODEL)),
            _resident((D_MODEL, D_MODEL)),
            _resident((D_MODEL, D_MODEL)),
            _resident((N_HEADS, WINDOW)),
            _resident((N_HEADS, 1)),
        ] + [pl.BlockSpec(memory_space=pl.ANY)] * n_alias,
        out_specs=[
            pl.BlockSpec((n, D_MODEL), lambda i: (0, 0)),
            pl.BlockSpec((n, 2 * D_MODEL), lambda i: (0, 0)),
            cache_spec,
            cache_spec,
        ],
        out_shape=[
            jax.ShapeDtypeStruct((n, D_MODEL), F32),
            jax.ShapeDtypeStruct((n, 2 * D_MODEL), F32),
            jax.ShapeDtypeStruct(kc.shape, F32),
            jax.ShapeDtypeStruct(vc.shape, F32),
        ],
        input_output_aliases={n_in + a: 2 + a for a in range(n_alias)},
        scratch_shapes=[
            pltpu.VMEM((n, D_MODEL), BF16),
            pltpu.VMEM((n, D_MODEL), F32),
            pltpu.VMEM((n, KV_W), F32),
            pltpu.VMEM((n, KV_W), F32),
            pltpu.VMEM((KV_W, n), F32),
            pltpu.VMEM((KV_W, n), F32),
            pltpu.VMEM((n, D_MODEL), F32),
            pltpu.VMEM((n, D_MODEL), F32),
        ],
        compiler_params=pltpu.CompilerParams(
            dimension_semantics=("arbitrary",),
            vmem_limit_bytes=VMEM_LIMIT_BYTES),
        name="mixer_decode",
    )(x, st, kc, vc, norm_g, w_in, w_q, conv_w, w_co, w_ao, w_o, bdec,
      sink_col, *prev_caches)


def kernel(x_prompt, x_sample, state_conv, cache_k_win, cache_v_win, rel_bias,
           norm_g, w_ff1_gu, w_ff1_down, w_in, conv_w, sinks, w_conv_out,
           w_attn_out, w_out, w_ff2_gu, w_ff2_down):
    batch, seq, _ = x_prompt.shape
    n_dec = x_sample.shape[0]

    ffn_steps = (batch * seq) // FFN_ROWS + 1
    mix_steps = (batch * seq) // MIX_ROWS
    bt_plain, bt, ffn_w = _bias_table(
        rel_bias, _ffn_cast_jobs(w_ff1_gu, w_ff1_down, 0, BIAS_STEPS))
    bdec = bt_plain[0].reshape(GROUP, BLOCK, N_KV, BLOCK)[:, BLOCK - 1]
    bdec = bdec.reshape(N_HEADS, WINDOW)
    bdec = jnp.concatenate([bdec[:, WINDOW - 1:], bdec[:, :WINDOW - 1]], axis=1)

    def keys_minor(c):
        return jnp.transpose(c, (0, 1, 3, 4, 2)).reshape(
            DEPTH, n_dec, KV_W, WINDOW)

    def keys_major(c):
        return jnp.transpose(
            c.reshape(DEPTH, n_dec, N_KV, HEAD_DIM, WINDOW), (0, 1, 4, 2, 3))

    kct = keys_minor(cache_k_win)
    vct = keys_minor(cache_v_win)

    xp = x_prompt.reshape(batch * seq, D_MODEL)
    xs = x_sample.reshape(n_dec, D_MODEL)
    pc, pk, pv, sc = [], [], [], []
    new_caches = ()
    for l in range(DEPTH):
        sink_col = sinks[l].reshape(N_KV, GROUP).T.reshape(N_HEADS, 1)

        xp, xs, mix_w = _ffn(
            xp, xs, norm_g, ffn_w[0], ffn_w[1], l, 0,
            _mixer_cast_jobs(w_in, w_conv_out, w_attn_out, w_out, l,
                             ffn_steps))
        win, wco, wao, wo, wq = mix_w

        xp, c1, k1, v1, ffn_w = _mixer_prompt(
            xp, batch, norm_g, win, wq, conv_w, wco, wao, wo, bt, sinks, l,
            _ffn_cast_jobs(w_ff2_gu, w_ff2_down, l, mix_steps))
        xs, c2, k2, v2 = _mixer_decode(
            xs, state_conv[l].reshape(n_dec, 2 * D_MODEL), kct, vct,
            norm_g, win, wq, conv_w, wco, wao, wo, bdec, sink_col, l,
            new_caches)
        new_caches = (k2, v2)

        next_jobs = (_ffn_cast_jobs(w_ff1_gu, w_ff1_down, l + 1, ffn_steps)
                     if l + 1 < DEPTH else [])
        xp, xs, ffn_w = _ffn(xp, xs, norm_g, ffn_w[0], ffn_w[1], l, 4,
                             next_jobs)

        pc.append(c1)
        pk.append(k1.reshape(batch, WINDOW, N_KV, HEAD_DIM))
        pv.append(v1.reshape(batch, WINDOW, N_KV, HEAD_DIM))
        sc.append(c2.reshape(n_dec, 2, D_MODEL))

    return (xp.reshape(batch, seq, D_MODEL), xs.reshape(n_dec, 1, D_MODEL),
            jnp.stack(pc), jnp.stack(pk), jnp.stack(pv),
            jnp.stack(sc), keys_major(new_caches[0]),
            keys_major(new_caches[1]))
```

```python
import functools
from typing import Callable, NamedTuple

import numpy as np
import jax
import jax.numpy as jnp
from jax import lax
from jax.experimental import pallas as pl
from jax.experimental.pallas import tpu as pltpu

F32 = jnp.float32
BF16 = jnp.bfloat16

D_MODEL = 1024
D_FF = 2816
N_HEADS = 16
N_KV = 4
GROUP = N_HEADS // N_KV
HEAD_DIM = 64
KV_W = N_KV * HEAD_DIM
WINDOW = 128
BLOCK = 128
N_BUCKETS = 32
MAX_DISTANCE = 128
RMS_EPS = 1e-6
NEG = -1e30
LOG2E = 1.4426950408889634
DEPTH = 2
N_NORMS = 6

C_CB, C_CC, C_CX, C_Q, C_K, C_V, C_GA, C_GB, C_END = (
    0, 1024, 2048, 3072, 4096, 4352, 4608, 5632, 6656)

VMEM_LIMIT_BYTES = 56 * 1024 * 1024

FFN_ROWS = 1024
FFN_SUB_ROWS = 512
FFN_CHUNK = 256
MIX_ROWS = 512
DEC_SEQS = 16
BIAS_STEPS = N_HEADS


def _t5_bucket_np(rel):
    n = np.maximum(rel, 0)
    max_exact = N_BUCKETS // 2
    nf = np.maximum(n, 1).astype(np.float32)
    large = max_exact + (
        np.log(nf / max_exact) / np.log(MAX_DISTANCE / max_exact)
        * (N_BUCKETS - max_exact)).astype(np.int32)
    large = np.minimum(large, N_BUCKETS - 1)
    return np.where(n < max_exact, n, large).astype(np.int32)


def _band_tables():
    qi = np.arange(BLOCK)[:, None]
    kj = np.arange(BLOCK)[None, :]
    rel = np.where(kj > qi, qi + BLOCK - kj, qi - kj)
    bucket = _t5_bucket_np(rel)
    mask = np.stack([np.ones_like(rel, bool), kj <= qi]).astype(np.float32)
    return bucket, mask


_BUCKET_NP, _MASK_NP = _band_tables()


def _head_perm():
    old = np.arange(D_MODEL)
    hk, g, d = old // KV_W, (old // HEAD_DIM) % GROUP, old % HEAD_DIM
    perm = np.zeros((D_MODEL, D_MODEL), np.float32)
    perm[old, g * KV_W + hk * HEAD_DIM + d] = 1.0
    return perm


_HEAD_PERM_NP = _head_perm()


def _rms(x, g):
    r = lax.rsqrt(jnp.mean(x * x, axis=-1, keepdims=True) + RMS_EPS)
    return (x * r) * g


def _dot(a, b):
    return jnp.dot(a, b, preferred_element_type=F32)


def _dot_t(a, b):
    return lax.dot_general(a, b, (((1,), (1,)), ((), ())),
                           preferred_element_type=F32)


def _resident(shape):
    return pl.BlockSpec(shape, lambda *_: (0,) * len(shape),
                        pipeline_mode=pl.Buffered(1))


def _layer_resident(shape, layer):
    return pl.BlockSpec((None,) + tuple(shape),
                        lambda *_: (layer,) + (0,) * len(shape),
                        pipeline_mode=pl.Buffered(1))


class _CastJob(NamedTuple):
    src: jax.Array
    layer: int
    rows: int
    src_block: Callable = lambda d: d
    col_block: int | None = None
    head_perm: bool = False

    @property
    def n_blocks(self):
        return self.src.shape[1] // self.rows


def _cast_rows(total_rows, host_steps, align=16):
    rows = align
    while total_rows % rows or total_rows // rows > host_steps:
        rows += align
    return rows


def _cast_specs(jobs, step_of, host_steps):
    in_specs, out_specs, out_shapes = [], [], []
    for job in jobs:
        assert job.n_blocks <= host_steps, (job.n_blocks, host_steps)
        cols = job.src.shape[2] if job.col_block is None else D_MODEL
        col = job.col_block or 0

        def dst(*ids, job=job):
            return jnp.minimum(step_of(*ids), job.n_blocks - 1)

        in_specs.append(pl.BlockSpec(
            (None, job.rows, cols),
            lambda *ids, job=job, dst=dst, col=col: (
                job.layer, job.src_block(dst(*ids)), col)))
        out_specs.append(pl.BlockSpec(
            (job.rows, cols), lambda *ids, dst=dst: (dst(*ids), 0)))
        out_shapes.append(
            jax.ShapeDtypeStruct((job.src.shape[1], cols), BF16))
    return in_specs, out_specs, out_shapes


def _run_casts(in_refs, out_refs, perm_blocks=None, perm_ref=None, step=None):
    for n, (src, dst) in enumerate(zip(in_refs, out_refs)):
        if perm_blocks is not None and perm_blocks[n]:
            @pl.when(step < perm_blocks[n])
            def _(src=src, dst=dst):
                dst[...] = _dot(src[...].astype(BF16),
                                perm_ref[...]).astype(BF16)
        else:
            dst[...] = src[...].astype(BF16)


def _ffn_cast_jobs(w_gu, w_down, layer, host_steps):
    return [_CastJob(w_gu, layer, _cast_rows(D_MODEL, host_steps)),
            _CastJob(w_down, layer, _cast_rows(D_FF, host_steps))]


def _mixer_cast_jobs(w_in, w_co, w_ao, w_o, layer, host_steps):
    to_g_major = lambda d: (d % GROUP) * N_KV + d // GROUP
    rows = _cast_rows(D_MODEL, host_steps)
    assert N_HEADS <= host_steps
    return [_CastJob(w_in, layer, rows), _CastJob(w_co, layer, rows),
            _CastJob(w_ao, layer, HEAD_DIM, to_g_major),
            _CastJob(w_o, layer, rows),
            _CastJob(w_in, layer, 2 * BLOCK, col_block=C_Q // D_MODEL,
                     head_perm=True)]


def _bias_kernel(*refs, cast_blocks):
    nj = len(cast_blocks)
    rb_ref, bucket_ref, mask_ref = refs[:3]
    cast_in = refs[3:3 + nj]
    plain_ref, exp2_ref = refs[3 + nj:5 + nj]
    cast_out = refs[5 + nj:]
    g = pl.program_id(0)
    hk = pl.program_id(1)
    _run_casts(cast_in, cast_out)
    h = hk * GROUP + g
    bucket = bucket_ref[...]

    def body(b, acc):
        return jnp.where(bucket == b, rb_ref[b, h], acc)

    bias = lax.fori_loop(0, N_BUCKETS, body, jnp.zeros((BLOCK, BLOCK), F32))
    bias2 = bias * LOG2E
    for variant in range(2):
        valid = mask_ref[variant] > 0.0
        plain_ref[variant] = jnp.where(valid, bias, NEG)
        exp2_ref[variant] = jnp.where(valid, bias2, NEG)


def _bias_table(rel_bias, jobs):
    c_in, c_out, c_shapes = _cast_specs(
        jobs, lambda g, hk: g * N_KV + hk, BIAS_STEPS)
    table_spec = pl.BlockSpec((2, BLOCK, BLOCK), lambda g, hk: (0, g, hk))
    table_shape = jax.ShapeDtypeStruct((2, GROUP * BLOCK, N_KV * BLOCK), F32)
    outs = pl.pallas_call(
        functools.partial(_bias_kernel,
                          cast_blocks=tuple(j.n_blocks for j in jobs)),
        grid=(GROUP, N_KV),
        in_specs=[
            pl.BlockSpec(memory_space=pltpu.SMEM),
            pl.BlockSpec((BLOCK, BLOCK), lambda g, hk: (0, 0)),
            pl.BlockSpec((2, BLOCK, BLOCK), lambda g, hk: (0, 0, 0)),
        ] + c_in,
        out_specs=[table_spec, table_spec] + c_out,
        out_shape=[table_shape, table_shape] + c_shapes,
        compiler_params=pltpu.CompilerParams(
            dimension_semantics=("arbitrary",) * 2),
        name="bias_table",
    )(rel_bias, jnp.asarray(_BUCKET_NP), jnp.asarray(_MASK_NP),
      *[j.src for j in jobs])
    return outs[0], outs[1], outs[2:]


def _ffn_rows(x, g_ref, wgu_ref, wd_ref, act_ref, gi):
    rows = x.shape[0]
    h = _rms(x, g_ref[gi:gi + 1, :]).astype(BF16)
    for c in range(D_FF // FFN_CHUNK):
        lo = c * FFN_CHUNK
        gate = _dot(h, wgu_ref[:, lo:lo + FFN_CHUNK])
        up = _dot(h, wgu_ref[:, D_FF + lo:D_FF + lo + FFN_CHUNK])
        act_ref[0:rows, lo:lo + FFN_CHUNK] = (
            gate * jax.nn.sigmoid(gate) * up).astype(BF16)
    y = _dot(act_ref[0:rows, :], wd_ref[...])
    return x + 0.5 * _rms(y, g_ref[gi + 1:gi + 2, :])


def _ffn_kernel(*refs, gi, prompt_steps, cast_perm):
    nj = len(cast_perm)
    xp_ref, xs_ref, g_ref, wgu_ref, wd_ref, perm_ref = refs[:6]
    cast_in = refs[6:6 + nj]
    op_ref, os_ref = refs[6 + nj:8 + nj]
    cast_out = refs[8 + nj:8 + 2 * nj]
    act_refs = refs[8 + 2 * nj:]
    i = pl.program_id(0)
    _run_casts(cast_in, cast_out, cast_perm, perm_ref, i)

    @pl.when(i < prompt_steps)
    def _():
        for t, act_ref in enumerate(act_refs):
            rs = slice(t * FFN_SUB_ROWS, (t + 1) * FFN_SUB_ROWS)
            op_ref[rs, :] = _ffn_rows(xp_ref[rs, :], g_ref, wgu_ref, wd_ref,
                                      act_ref, gi)

    @pl.when(i == prompt_steps)
    def _():
        os_ref[...] = _ffn_rows(xs_ref[...], g_ref, wgu_ref, wd_ref,
                                act_refs[0], gi)


def _ffn(xp, xs, norm_g, w_gu, w_down, layer, gi, jobs):
    m = xp.shape[0]
    n = xs.shape[0]
    tm = FFN_ROWS
    steps = m // tm
    row_block = lambda i: (jnp.minimum(i, steps - 1), 0)
    c_in, c_out, c_shapes = _cast_specs(jobs, lambda i: i, steps + 1)
    outs = pl.pallas_call(
        functools.partial(_ffn_kernel, gi=gi, prompt_steps=steps,
                          cast_perm=tuple(j.n_blocks if j.head_perm else 0
                                          for j in jobs)),
        grid=(steps + 1,),
        in_specs=[
            pl.BlockSpec((tm, D_MODEL), row_block),
            _resident((n, D_MODEL)),
            _layer_resident((N_NORMS, D_MODEL), layer),
            _resident((D_MODEL, 2 * D_FF)),
            _resident((D_FF, D_MODEL)),
            _resident((D_MODEL, D_MODEL)),
        ] + c_in,
        out_specs=[
            pl.BlockSpec((tm, D_MODEL), row_block),
            pl.BlockSpec((n, D_MODEL), lambda i: (0, 0)),
        ] + c_out,
        out_shape=[
            jax.ShapeDtypeStruct((m, D_MODEL), F32),
            jax.ShapeDtypeStruct((n, D_MODEL), F32),
        ] + c_shapes,
        scratch_shapes=[pltpu.VMEM((FFN_SUB_ROWS, D_FF), BF16)
                        for _ in range(FFN_ROWS // FFN_SUB_ROWS)],
        compiler_params=pltpu.CompilerParams(
            dimension_semantics=("arbitrary",),
            vmem_limit_bytes=VMEM_LIMIT_BYTES),
        name="ffn_half",
    )(xp, xs, norm_g, w_gu, w_down, jnp.asarray(_HEAD_PERM_NP, BF16),
      *[j.src for j in jobs])
    return outs[0], outs[1], outs[2:]


def _lane_segment_mask(rows):
    lane = lax.broadcasted_iota(jnp.int32, (rows, KV_W), 1)
    return [(lane >= hk * HEAD_DIM) & (lane < (hk + 1) * HEAD_DIM)
            for hk in range(N_KV)]


def _mixer_prompt_kernel(*refs, layer, rows, cast_blocks):
    nj = len(cast_blocks)
    (x_ref, g_ref, win_ref, wq_ref, cw_ref, wco_ref, wao_ref, wo_ref, bt_ref,
     sinks_ref) = refs[:10]
    cast_in = refs[10:10 + nj]
    y_ref, cst_ref, kout_ref, vout_ref = refs[10 + nj:14 + nj]
    cast_out = refs[14 + nj:14 + 2 * nj]
    q_s, kbuf, vbuf, upad, o_s, p_s = refs[14 + 2 * nj:]
    i = pl.program_id(1)
    nq = rows // BLOCK

    @pl.when(i == 0)
    def _():
        upad[0:8, :] = jnp.zeros((8, D_MODEL), F32)
        kbuf[:, 0:BLOCK, :] = jnp.zeros((N_KV, BLOCK, KV_W), BF16)
        vbuf[:, 0:BLOCK, :] = jnp.zeros((N_KV, BLOCK, KV_W), BF16)

    @pl.when(i > 0)
    def _():
        upad[0:8, :] = upad[rows:rows + 8, :]
        kbuf[:, 0:BLOCK, :] = kbuf[:, rows:rows + BLOCK, :]
        vbuf[:, 0:BLOCK, :] = vbuf[:, rows:rows + BLOCK, :]

    _run_casts(cast_in, cast_out)
    x = x_ref[...]
    h = _rms(x, g_ref[2:3, :]).astype(BF16)

    q_s[...] = (_dot(h, wq_ref[...])
                * (HEAD_DIM ** -0.5 * LOG2E)).astype(BF16)
    k = _dot(h, win_ref[:, C_K:C_V])
    v = _dot(h, win_ref[:, C_V:C_GA])
    k16 = k.astype(BF16)
    v16 = v.astype(BF16)
    seg = _lane_segment_mask(rows)
    zero_kv = jnp.zeros((rows, KV_W), BF16)
    for hk in range(N_KV):
        kbuf[hk, BLOCK:BLOCK + rows, :] = jnp.where(seg[hk], k16, zero_kv)
        vbuf[hk, BLOCK:BLOCK + rows, :] = jnp.where(seg[hk], v16, zero_kv)

    kout_ref[...] = k[rows - BLOCK:rows, :]
    vout_ref[...] = v[rows - BLOCK:rows, :]

    q_idx = lax.broadcasted_iota(jnp.int32, (BLOCK, BLOCK), 0)
    j_idx = lax.broadcasted_iota(jnp.int32, (BLOCK, BLOCK), 1)
    from_prev = j_idx > q_idx
    prev16 = jnp.where(from_prev, 1.0, 0.0).astype(BF16)
    own16 = jnp.where(from_prev, 0.0, 1.0).astype(BF16)
    first = (i == 0).astype(jnp.int32)
    seg_q = _lane_segment_mask(BLOCK)

    for j in range(nq):
        r0 = j * BLOCK
        variant = first if j == 0 else 0
        qj = q_s[r0:r0 + BLOCK, :]
        kb = jnp.concatenate(
            [kbuf[hk, r0:r0 + 2 * BLOCK, :] for hk in range(N_KV)], axis=0)
        vb = jnp.concatenate(
            [vbuf[hk, r0:r0 + 2 * BLOCK, :] for hk in range(N_KV)], axis=0)
        qs = jnp.concatenate(
            [qj[:, g * KV_W:(g + 1) * KV_W] for g in range(GROUP)], axis=0)
        s_all = _dot_t(qs, kb)
        inv = []
        for g in range(GROUP):
            inv_g = []
            for hk in range(N_KV):
                rs = slice(g * BLOCK, (g + 1) * BLOCK)
                c0 = hk * 2 * BLOCK
                s = jnp.where(from_prev, s_all[rs, c0:c0 + BLOCK],
                              s_all[rs, c0 + BLOCK:c0 + 2 * BLOCK])
                s = s + bt_ref[variant, rs, hk * BLOCK:(hk + 1) * BLOCK]
                sink = sinks_ref[layer, hk * GROUP + g] * LOG2E
                m = jnp.maximum(jnp.max(s, axis=-1, keepdims=True), sink)
                p = jnp.exp2(s - m)
                p16 = p.astype(BF16)
                p_s[j, rs, c0:c0 + BLOCK] = p16 * prev16
                p_s[j, rs, c0 + BLOCK:c0 + 2 * BLOCK] = p16 * own16
                den = jnp.sum(p, axis=-1, keepdims=True) + jnp.exp2(sink - m)
                inv_g.append(1.0 / den)
            inv.append(inv_g)
        o_all = _dot(p_s[j], vb)
        for g in range(GROUP):
            scale = jnp.where(
                seg_q[0], inv[g][0],
                jnp.where(seg_q[1], inv[g][1],
                          jnp.where(seg_q[2], inv[g][2], inv[g][3])))
            o_s[r0:r0 + BLOCK, g * KV_W:(g + 1) * KV_W] = (
                o_all[g * BLOCK:(g + 1) * BLOCK, :] * scale).astype(BF16)

    u = _dot(h, win_ref[:, C_CC:C_CX]) * _dot(h, win_ref[:, C_CX:C_Q])
    upad[8:rows + 8, :] = u

    cst_ref[...] = u[rows - 2:rows, :]
    yc =(cw_ref[0:1, :] * upad[6:rows + 6, :]
          + cw_ref[1:2, :] * upad[7:rows + 7, :]
          + cw_ref[2:3, :] * u)
    a_in = (_dot(h, win_ref[:, C_CB:C_CC]) * yc).astype(BF16)
    a_out = _dot(a_in, wco_ref[...])
    gated_a = jax.nn.sigmoid(_dot(h, win_ref[:, C_GA:C_GB])) * a_out

    att = _dot(o_s[...], wao_ref[...])
    gate_b = jax.nn.sigmoid(_dot(h, win_ref[:, C_GB:C_END]))
    merged = (gated_a + gate_b * att).astype(BF16)
    y_ref[...] = x + _rms(_dot(merged, wo_ref[...]), g_ref[3:4, :])


def _mixer_prompt(x, batch, norm_g, w_in, w_q, conv_w, w_co, w_ao, w_o, bt,
                  sinks, layer, jobs):
    m = x.shape[0]
    seq = m // batch
    rows = MIX_ROWS
    steps = seq // rows
    c_in, c_out, c_shapes = _cast_specs(jobs, lambda b, i: b * steps + i,
                                        batch * steps)
    kern = functools.partial(_mixer_prompt_kernel, layer=layer, rows=rows,
                             cast_blocks=tuple(j.n_blocks for j in jobs))
    outs = pl.pallas_call(
        kern,
        grid=(batch, steps),
        in_specs=[
            pl.BlockSpec((rows, D_MODEL), lambda b, i: (b * steps + i, 0)),
            _layer_resident((N_NORMS, D_MODEL), layer),
            _resident((D_MODEL, C_END)),
            _resident((D_MODEL, D_MODEL)),
            _layer_resident((3, D_MODEL), layer),
            _resident((D_MODEL, D_MODEL)),
            _resident((D_MODEL, D_MODEL)),
            _resident((D_MODEL, D_MODEL)),
            _resident((2, GROUP * BLOCK, N_KV * BLOCK)),
            pl.BlockSpec(memory_space=pltpu.SMEM),
        ] + c_in,
        out_specs=[
            pl.BlockSpec((rows, D_MODEL), lambda b, i: (b * steps + i, 0)),
            pl.BlockSpec((None, 2, D_MODEL), lambda b, i: (b, 0, 0)),
            pl.BlockSpec((None, BLOCK, KV_W), lambda b, i: (b, 0, 0)),
            pl.BlockSpec((None, BLOCK, KV_W), lambda b, i: (b, 0, 0)),
        ] + c_out,
        out_shape=[
            jax.ShapeDtypeStruct((m, D_MODEL), F32),
            jax.ShapeDtypeStruct((batch, 2, D_MODEL), F32),
            jax.ShapeDtypeStruct((batch, BLOCK, KV_W), F32),
            jax.ShapeDtypeStruct((batch, BLOCK, KV_W), F32),
        ] + c_shapes,
        scratch_shapes=[
            pltpu.VMEM((rows, D_MODEL), BF16),
            pltpu.VMEM((N_KV, rows + BLOCK, KV_W), BF16),
            pltpu.VMEM((N_KV, rows + BLOCK, KV_W), BF16),
            pltpu.VMEM((rows + 8, D_MODEL), F32),
            pltpu.VMEM((rows, D_MODEL), BF16),
            pltpu.VMEM((rows // BLOCK, GROUP * BLOCK, N_KV * 2 * BLOCK),
                       BF16),
        ],
        compiler_params=pltpu.CompilerParams(
            dimension_semantics=("arbitrary", "arbitrary"),
            vmem_limit_bytes=VMEM_LIMIT_BYTES,
        ),
        name="mixer_prompt",
    )(x, norm_g, w_in, w_q, conv_w, w_co, w_ao, w_o, bt, sinks,
      *[j.src for j in jobs])
    return outs[0], outs[1], outs[2], outs[3], outs[4:]


def _mixer_decode_kernel(*refs, n_alias):
    (x_ref, st_ref, kc_ref, vc_ref, g_ref, win_ref, wq_ref, cw_ref, wco_ref,
     wao_ref, wo_ref, bdec_ref, sink_ref) = refs[:13]
    (y_ref, nst_ref, kout_ref, vout_ref, h_s, q_s, knew_s, vnew_s, knewt_s,
     vnewt_s, ag_s, o_s) = refs[13 + n_alias:]
    i = pl.program_id(0)
    last = pl.num_programs(0) - 1

    @pl.when(i == 0)
    def _():
        x = x_ref[...]
        h = _rms(x, g_ref[2:3, :]).astype(BF16)
        h_s[...] = h
        u = _dot(h, win_ref[:, C_CC:C_CX]) * _dot(h, win_ref[:, C_CX:C_Q])
        st0 = st_ref[:, 0:D_MODEL]
        st1 = st_ref[:, D_MODEL:2 * D_MODEL]
        yc = cw_ref[0:1, :] * st0 + cw_ref[1:2, :] * st1 + cw_ref[2:3, :] * u
        nst_ref[:, 0:D_MODEL] = st1
        nst_ref[:, D_MODEL:2 * D_MODEL] = u
        a_in = (_dot(h, win_ref[:, C_CB:C_CC]) * yc).astype(BF16)
        a_out = _dot(a_in, wco_ref[...])
        ag_s[...] = jax.nn.sigmoid(_dot(h, win_ref[:, C_GA:C_GB])) * a_out
        q_s[...] = _dot(h, wq_ref[...]) * (HEAD_DIM ** -0.5)
        knew = _dot(h, win_ref[:, C_K:C_V])
        vnew = _dot(h, win_ref[:, C_V:C_GA])
        knew_s[...] = knew
        vnew_s[...] = vnew
        knewt_s[...] = knew.T
        vnewt_s[...] = vnew.T

    row = lax.broadcasted_iota(jnp.int32, (N_HEADS, KV_W), 0)
    lane = lax.broadcasted_iota(jnp.int32, (N_HEADS, KV_W), 1)
    row_g = row // N_KV
    seg = (lane // HEAD_DIM) == (row % N_KV)
    key0 = lax.broadcasted_iota(jnp.int32, (N_HEADS, WINDOW), 1) == 0
    newest = lax.broadcasted_iota(jnp.int32, (KV_W, WINDOW), 1) == WINDOW - 1
    bias = bdec_ref[...]
    sink = sink_ref[...]

    for b in range(DEC_SEQS):
        to_last = (WINDOW - 1) - (i * DEC_SEQS + b)
        kout_ref[b] = jnp.where(
            newest, pltpu.roll(knewt_s[...], to_last, axis=1),
            pltpu.roll(kc_ref[b], WINDOW - 1, axis=1))
        vout_ref[b] = jnp.where(
            newest, pltpu.roll(vnewt_s[...], to_last, axis=1),
            pltpu.roll(vc_ref[b], WINDOW - 1, axis=1))

    qms, k_rows, v_rows = [], [], []
    for b in range(DEC_SEQS):
        r = i * DEC_SEQS + b
        q_row = q_s[pl.ds(r, 1), :]
        pieces = [jnp.broadcast_to(q_row[:, g * KV_W:(g + 1) * KV_W],
                                   (N_HEADS, KV_W)) for g in range(GROUP)]
        q_rows = jnp.where(row_g == 0, pieces[0],
                           jnp.where(row_g == 1, pieces[1],
                                     jnp.where(row_g == 2, pieces[2],
                                               pieces[3])))
        qms.append(jnp.where(seg, q_rows, 0.0))
        k_rows.append(knew_s[pl.ds(r, 1), :])
        v_rows.append(vnew_s[pl.ds(r, 1), :])
    qm = jnp.stack(qms).astype(BF16)
    k_row = jnp.stack(k_rows).astype(BF16).astype(F32)
    v_row = jnp.stack(v_rows).astype(BF16).astype(F32)
    s_new = jnp.sum(qm.astype(F32) * k_row, axis=-1, keepdims=True)
    s_old = jnp.einsum('bqd,bdk->bqk', qm, kc_ref[...].astype(BF16),
                       preferred_element_type=F32)
    s = jnp.where(key0, s_new, s_old) + bias
    m = jnp.maximum(jnp.max(s, axis=-1, keepdims=True), sink)
    p = jnp.exp(s - m)
    den = jnp.sum(p, axis=-1, keepdims=True) + jnp.exp(sink - m)
    p16 = p.astype(BF16)
    p_new = p16[:, :, 0:1].astype(F32)
    p_old = jnp.where(key0, jnp.zeros_like(p16), p16)
    o = jnp.einsum('bqk,bdk->bqd', p_old, vc_ref[...].astype(BF16),
                   preferred_element_type=F32)
    om = jnp.where(seg, (o + p_new * v_row) / den, 0.0)
    for b in range(DEC_SEQS):
        r = i * DEC_SEQS + b
        for g in range(GROUP):
            o_s[pl.ds(r, 1), g * KV_W:(g + 1) * KV_W] = jnp.sum(
                om[b, g * N_KV:(g + 1) * N_KV, :], axis=0, keepdims=True)

    @pl.when(i == last)
    def _():
        att = _dot(o_s[...].astype(BF16), wao_ref[...])
        gate_b = jax.nn.sigmoid(_dot(h_s[...], win_ref[:, C_GB:C_END]))
        merged = (ag_s[...] + gate_b * att).astype(BF16)
        y_ref[...] = x_ref[...] + _rms(_dot(merged, wo_ref[...]),
                                       g_ref[3:4, :])


def _mixer_decode(x, st, kc, vc, norm_g, w_in, w_q, conv_w, w_co, w_ao, w_o,
                  bdec, sink_col, layer, prev_caches):
    n = x.shape[0]
    sb = DEC_SEQS
    cache_spec = pl.BlockSpec((None, sb, KV_W, WINDOW),
                              lambda i: (layer, i, 0, 0))
    n_alias = len(prev_caches)
    n_in = 13
    return pl.pallas_call(
        functools.partial(_mixer_decode_kernel, n_alias=n_alias),
        grid=(n // sb,),
        in_specs=[
            _resident((n, D_MODEL)),
            _resident((n, 2 * D_MODEL)),
            cache_spec,
            cache_spec,
            _layer_resident((N_NORMS, D_MODEL), layer),
            _resident((D_MODEL, C_END)),
            _resident((D_MODEL, D_MODEL)),
            _layer_resident((3, D_MODEL), layer),
            _resident((D_MODEL, D_MODEL)),
            _resident((D_MODEL, D_MODEL)),
            _resident((D_MODEL, D_MODEL)),
            _resident((N_HEADS, WINDOW)),
            _resident((N_HEADS, 1)),
        ] + [pl.BlockSpec(memory_space=pl.ANY)] * n_alias,
        out_specs=[
            pl.BlockSpec((n, D_MODEL), lambda i: (0, 0)),
            pl.BlockSpec((n, 2 * D_MODEL), lambda i: (0, 0)),
            cache_spec,
            cache_spec,
        ],
        out_shape=[
            jax.ShapeDtypeStruct((n, D_MODEL), F32),
            jax.ShapeDtypeStruct((n, 2 * D_MODEL), F32),
            jax.ShapeDtypeStruct(kc.shape, F32),
            jax.ShapeDtypeStruct(vc.shape, F32),
        ],
        input_output_aliases={n_in + a: 2 + a for a in range(n_alias)},
        scratch_shapes=[
            pltpu.VMEM((n, D_MODEL), BF16),
            pltpu.VMEM((n, D_MODEL), F32),
            pltpu.VMEM((n, KV_W), F32),
            pltpu.VMEM((n, KV_W), F32),
            pltpu.VMEM((KV_W, n), F32),
            pltpu.VMEM((KV_W, n), F32),
            pltpu.VMEM((n, D_MODEL), F32),
            pltpu.VMEM((n, D_MODEL), F32),
        ],
        compiler_params=pltpu.CompilerParams(
            dimension_semantics=("arbitrary",),
            vmem_limit_bytes=VMEM_LIMIT_BYTES),
        name="mixer_decode",
    )(x, st, kc, vc, norm_g, w_in, w_q, conv_w, w_co, w_ao, w_o, bdec,
      sink_col, *prev_caches)


def kernel(x_prompt, x_sample, state_conv, cache_k_win, cache_v_win, rel_bias,
           norm_g, w_ff1_gu, w_ff1_down, w_in, conv_w, sinks, w_conv_out,
           w_attn_out, w_out, w_ff2_gu, w_ff2_down):
    batch, seq, _ = x_prompt.shape
    n_dec = x_sample.shape[0]

    ffn_steps = (batch * seq) // FFN_ROWS + 1
    mix_steps = (batch * seq) // MIX_ROWS
    bt_plain, bt, ffn_w = _bias_table(
        rel_bias, _ffn_cast_jobs(w_ff1_gu, w_ff1_down, 0, BIAS_STEPS))
    bdec = bt_plain[0].reshape(GROUP, BLOCK, N_KV, BLOCK)[:, BLOCK - 1]
    bdec = bdec.reshape(N_HEADS, WINDOW)
    bdec = jnp.concatenate([bdec[:, WINDOW - 1:], bdec[:, :WINDOW - 1]], axis=1)

    def keys_minor(c):
        return jnp.transpose(c, (0, 1, 3, 4, 2)).reshape(
            DEPTH, n_dec, KV_W, WINDOW)

    def keys_major(c):
        return jnp.transpose(
            c.reshape(DEPTH, n_dec, N_KV, HEAD_DIM, WINDOW), (0, 1, 4, 2, 3))

    kct = keys_minor(cache_k_win)
    vct = keys_minor(cache_v_win)

    xp = x_prompt.reshape(batch * seq, D_MODEL)
    xs = x_sample.reshape(n_dec, D_MODEL)
    pc, pk, pv, sc = [], [], [], []
    new_caches = ()
    for l in range(DEPTH):
        sink_col = sinks[l].reshape(N_KV, GROUP).T.reshape(N_HEADS, 1)

        xp, xs, mix_w = _ffn(
            xp, xs, norm_g, ffn_w[0], ffn_w[1], l, 0,
            _mixer_cast_jobs(w_in, w_conv_out, w_attn_out, w_out, l,
                             ffn_steps))
        win, wco, wao, wo, wq = mix_w

        xp, c1, k1, v1, ffn_w = _mixer_prompt(
            xp, batch, norm_g, win, wq, conv_w, wco, wao, wo, bt, sinks, l,
            _ffn_cast_jobs(w_ff2_gu, w_ff2_down, l, mix_steps))
        xs, c2, k2, v2 = _mixer_decode(
            xs, state_conv[l].reshape(n_dec, 2 * D_MODEL), kct, vct,
            norm_g, win, wq, conv_w, wco, wao, wo, bdec, sink_col, l,
            new_caches)
        new_caches = (k2, v2)

        next_jobs = (_ffn_cast_jobs(w_ff1_gu, w_ff1_down, l + 1, ffn_steps)
                     if l + 1 < DEPTH else [])
        xp, xs, ffn_w = _ffn(xp, xs, norm_g, ffn_w[0], ffn_w[1], l, 4,
                             next_jobs)

        pc.append(c1)
        pk.append(k1.reshape(batch, WINDOW, N_KV, HEAD_DIM))
        pv.append(v1.reshape(batch, WINDOW, N_KV, HEAD_DIM))
        sc.append(c2.reshape(n_dec, 2, D_MODEL))

    return (xp.reshape(batch, seq, D_MODEL), xs.reshape(n_dec, 1, D_MODEL),
            jnp.stack(pc), jnp.stack(pk), jnp.stack(pv),
            jnp.stack(sc), keys_major(new_caches[0]),
            keys_major(new_caches[1]))
```

```python
import functools
from typing import Callable, NamedTuple

import numpy as np
import jax
import jax.numpy as jnp
from jax import lax
from jax.experimental import pallas as pl
from jax.experimental.pallas import tpu as pltpu

F32 = jnp.float32
BF16 = jnp.bfloat16

D_MODEL = 1024
D_FF = 2816
N_HEADS = 16
N_KV = 4
GROUP = N_HEADS // N_KV
HEAD_DIM = 64
KV_W = N_KV * HEAD_DIM
WINDOW = 128
BLOCK = 128
N_BUCKETS = 32
MAX_DISTANCE = 128
RMS_EPS = 1e-6
NEG = -1e30
LOG2E = 1.4426950408889634
DEPTH = 2
N_NORMS = 6

C_CB, C_CC, C_CX, C_Q, C_K, C_V, C_GA, C_GB, C_END = (
    0, 1024, 2048, 3072, 4096, 4352, 4608, 5632, 6656)

VMEM_LIMIT_BYTES = 56 * 1024 * 1024

FFN_ROWS = 1024
FFN_SUB_ROWS = 512
FFN_CHUNK = 256
MIX_ROWS = 512
DEC_SEQS = 16
BIAS_STEPS = N_HEADS


def _t5_bucket_np(rel):
    n = np.maximum(rel, 0)
    max_exact = N_BUCKETS // 2
    nf = np.maximum(n, 1).astype(np.float32)
    large = max_exact + (
        np.log(nf / max_exact) / np.log(MAX_DISTANCE / max_exact)
        * (N_BUCKETS - max_exact)).astype(np.int32)
    large = np.minimum(large, N_BUCKETS - 1)
    return np.where(n < max_exact, n, large).astype(np.int32)


def _band_tables():
    qi = np.arange(BLOCK)[:, None]
    kj = np.arange(BLOCK)[None, :]
    rel = np.where(kj > qi, qi + BLOCK - kj, qi - kj)
    bucket = _t5_bucket_np(rel)
    mask = np.stack([np.ones_like(rel, bool), kj <= qi]).astype(np.float32)
    return bucket, mask


_BUCKET_NP, _MASK_NP = _band_tables()


def _head_perm():
    old = np.arange(D_MODEL)
    hk, g, d = old // KV_W, (old // HEAD_DIM) % GROUP, old % HEAD_DIM
    perm = np.zeros((D_MODEL, D_MODEL), np.float32)
    perm[old, g * KV_W + hk * HEAD_DIM + d] = 1.0
    return perm


_HEAD_PERM_NP = _head_perm()


def _rms(x, g):
    r = lax.rsqrt(jnp.mean(x * x, axis=-1, keepdims=True) + RMS_EPS)
    return (x * r) * g


def _dot(a, b):
    return jnp.dot(a, b, preferred_element_type=F32)


def _dot_t(a, b):
    return lax.dot_general(a, b, (((1,), (1,)), ((), ())),
                           preferred_element_type=F32)


def _resident(shape):
    return pl.BlockSpec(shape, lambda *_: (0,) * len(shape),
                        pipeline_mode=pl.Buffered(1))


def _layer_resident(shape, layer):
    return pl.BlockSpec((None,) + tuple(shape),
                        lambda *_: (layer,) + (0,) * len(shape),
                        pipeline_mode=pl.Buffered(1))


class _CastJob(NamedTuple):
    src: jax.Array
    layer: int
    rows: int
    src_block: Callable = lambda d: d
    col_block: int | None = None
    head_perm: bool = False

    @property
    def n_blocks(self):
        return self.src.shape[1] // self.rows


def _cast_rows(total_rows, host_steps, align=16):
    rows = align
    while total_rows % rows or total_rows // rows > host_steps:
        rows += align
    return rows


def _cast_specs(jobs, step_of, host_steps):
    in_specs, out_specs, out_shapes = [], [], []
    for job in jobs:
        assert job.n_blocks <= host_steps, (job.n_blocks, host_steps)
        cols = job.src.shape[2] if job.col_block is None else D_MODEL
        col = job.col_block or 0

        def dst(*ids, job=job):
            return jnp.minimum(step_of(*ids), job.n_blocks - 1)

        in_specs.append(pl.BlockSpec(
            (None, job.rows, cols),
            lambda *ids, job=job, dst=dst, col=col: (
                job.layer, job.src_block(dst(*ids)), col)))
        out_specs.append(pl.BlockSpec(
            (job.rows, cols), lambda *ids, dst=dst: (dst(*ids), 0)))
        out_shapes.append(
            jax.ShapeDtypeStruct((job.src.shape[1], cols), BF16))
    return in_specs, out_specs, out_shapes


def _run_casts(in_refs, out_refs, perm_blocks=None):
    for n, (src, dst) in enumerate(zip(in_refs, out_refs)):
        if perm_blocks is None or not perm_blocks[n]:
            dst[...] = src[...].astype(BF16)


def _run_perm_casts(in_refs, out_refs, perm_blocks, perm_ref, step):
    for n, (src, dst) in enumerate(zip(in_refs, out_refs)):
        if perm_blocks[n]:
            @pl.when(step < perm_blocks[n])
            def _(src=src, dst=dst):
                dst[...] = _dot(src[...].astype(BF16),
                                perm_ref[...]).astype(BF16)


def _ffn_cast_jobs(w_gu, w_down, layer, host_steps):
    return [_CastJob(w_gu, layer, _cast_rows(D_MODEL, host_steps)),
            _CastJob(w_down, layer, _cast_rows(D_FF, host_steps))]


def _mixer_cast_jobs(w_in, w_co, w_ao, w_o, layer, host_steps):
    to_g_major = lambda d: (d % GROUP) * N_KV + d // GROUP
    rows = _cast_rows(D_MODEL, host_steps)
    assert N_HEADS <= host_steps
    return [_CastJob(w_in, layer, rows), _CastJob(w_co, layer, rows),
            _CastJob(w_ao, layer, HEAD_DIM, to_g_major),
            _CastJob(w_o, layer, rows),
            _CastJob(w_in, layer, 2 * BLOCK, col_block=C_Q // D_MODEL,
                     head_perm=True)]


def _bias_kernel(*refs, cast_blocks):
    nj = len(cast_blocks)
    rb_ref, bucket_ref, mask_ref = refs[:3]
    cast_in = refs[3:3 + nj]
    plain_ref, exp2_ref = refs[3 + nj:5 + nj]
    cast_out = refs[5 + nj:]
    g = pl.program_id(0)
    hk = pl.program_id(1)
    _run_casts(cast_in, cast_out)
    h = hk * GROUP + g
    bucket = bucket_ref[...]

    def body(b, acc):
        return jnp.where(bucket == b, rb_ref[b, h], acc)

    bias = lax.fori_loop(0, N_BUCKETS, body, jnp.zeros((BLOCK, BLOCK), F32))
    bias2 = bias * LOG2E
    for variant in range(2):
        valid = mask_ref[variant] > 0.0
        plain_ref[variant] = jnp.where(valid, bias, NEG)
        exp2_ref[variant] = jnp.where(valid, bias2, NEG)


def _bias_table(rel_bias, jobs):
    c_in, c_out, c_shapes = _cast_specs(
        jobs, lambda g, hk: g * N_KV + hk, BIAS_STEPS)
    table_spec = pl.BlockSpec((2, BLOCK, BLOCK), lambda g, hk: (0, g, hk))
    table_shape = jax.ShapeDtypeStruct((2, GROUP * BLOCK, N_KV * BLOCK), F32)
    outs = pl.pallas_call(
        functools.partial(_bias_kernel,
                          cast_blocks=tuple(j.n_blocks for j in jobs)),
        grid=(GROUP, N_KV),
        in_specs=[
            pl.BlockSpec(memory_space=pltpu.SMEM),
            pl.BlockSpec((BLOCK, BLOCK), lambda g, hk: (0, 0)),
            pl.BlockSpec((2, BLOCK, BLOCK), lambda g, hk: (0, 0, 0)),
        ] + c_in,
        out_specs=[table_spec, table_spec] + c_out,
        out_shape=[table_shape, table_shape] + c_shapes,
        compiler_params=pltpu.CompilerParams(
            dimension_semantics=("arbitrary",) * 2),
        name="bias_table",
    )(rel_bias, jnp.asarray(_BUCKET_NP), jnp.asarray(_MASK_NP),
      *[j.src for j in jobs])
    return outs[0], outs[1], outs[2:]


def _ffn_rows(x, g_ref, wgu_ref, wd_ref, act_ref, gi):
    rows = x.shape[0]
    h = _rms(x, g_ref[gi:gi + 1, :]).astype(BF16)
    for c in range(D_FF // FFN_CHUNK):
        lo = c * FFN_CHUNK
        gate = _dot(h, wgu_ref[:, lo:lo + FFN_CHUNK])
        up = _dot(h, wgu_ref[:, D_FF + lo:D_FF + lo + FFN_CHUNK])
        act_ref[0:rows, lo:lo + FFN_CHUNK] = (
            gate * jax.nn.sigmoid(gate) * up).astype(BF16)
    y = _dot(act_ref[0:rows, :], wd_ref[...])
    return x + 0.5 * _rms(y, g_ref[gi + 1:gi + 2, :])


def _ffn_kernel(*refs, gi, prompt_steps, cast_perm):
    nj = len(cast_perm)
    xp_ref, xs_ref, g_ref, wgu_ref, wd_ref, perm_ref = refs[:6]
    cast_in = refs[6:6 + nj]
    op_ref, os_ref = refs[6 + nj:8 + nj]
    cast_out = refs[8 + nj:8 + 2 * nj]
    act_refs = refs[8 + 2 * nj:]
    i = pl.program_id(0)
    _run_perm_casts(cast_in, cast_out, cast_perm, perm_ref, i)

    @pl.when(i < prompt_steps)
    def _():
        _run_casts(cast_in, cast_out, cast_perm)
        for t, act_ref in enumerate(act_refs):
            rs = slice(t * FFN_SUB_ROWS, (t + 1) * FFN_SUB_ROWS)
            op_ref[rs, :] = _ffn_rows(xp_ref[rs, :], g_ref, wgu_ref, wd_ref,
                                      act_ref, gi)

    @pl.when(i == prompt_steps)
    def _():
        os_ref[...] = _ffn_rows(xs_ref[...], g_ref, wgu_ref, wd_ref,
                                act_refs[0], gi)


def _ffn(xp, xs, norm_g, w_gu, w_down, layer, gi, jobs):
    m = xp.shape[0]
    n = xs.shape[0]
    tm = FFN_ROWS
    steps = m // tm
    row_block = lambda i: (jnp.minimum(i, steps - 1), 0)
    c_in, c_out, c_shapes = _cast_specs(jobs, lambda i: i, steps)
    outs = pl.pallas_call(
        functools.partial(_ffn_kernel, gi=gi, prompt_steps=steps,
                          cast_perm=tuple(j.n_blocks if j.head_perm else 0
                                          for j in jobs)),
        grid=(steps + 1,),
        in_specs=[
            pl.BlockSpec((tm, D_MODEL), row_block),
            _resident((n, D_MODEL)),
            _layer_resident((N_NORMS, D_MODEL), layer),
            _resident((D_MODEL, 2 * D_FF)),
            _resident((D_FF, D_MODEL)),
            _resident((D_MODEL, D_MODEL)),
        ] + c_in,
        out_specs=[
            pl.BlockSpec((tm, D_MODEL), row_block),
            pl.BlockSpec((n, D_MODEL), lambda i: (0, 0)),
        ] + c_out,
        out_shape=[
            jax.ShapeDtypeStruct((m, D_MODEL), F32),
            jax.ShapeDtypeStruct((n, D_MODEL), F32),
        ] + c_shapes,
        scratch_shapes=[pltpu.VMEM((FFN_SUB_ROWS, D_FF), BF16)
                        for _ in range(FFN_ROWS // FFN_SUB_ROWS)],
        compiler_params=pltpu.CompilerParams(
            dimension_semantics=("arbitrary",),
            vmem_limit_bytes=VMEM_LIMIT_BYTES),
        name="ffn_half",
    )(xp, xs, norm_g, w_gu, w_down, jnp.asarray(_HEAD_PERM_NP, BF16),
      *[j.src for j in jobs])
    return outs[0], outs[1], outs[2:]


def _lane_segment_mask(rows):
    lane = lax.broadcasted_iota(jnp.int32, (rows, KV_W), 1)
    return [(lane >= hk * HEAD_DIM) & (lane < (hk + 1) * HEAD_DIM)
            for hk in range(N_KV)]


def _mixer_prompt_kernel(*refs, layer, rows, cast_blocks):
    nj = len(cast_blocks)
    (x_ref, g_ref, win_ref, wq_ref, cw_ref, wco_ref, wao_ref, wo_ref, bt_ref,
     sinks_ref) = refs[:10]
    cast_in = refs[10:10 + nj]
    y_ref, cst_ref, kout_ref, vout_ref = refs[10 + nj:14 + nj]
    cast_out = refs[14 + nj:14 + 2 * nj]
    q_s, kbuf, vbuf, upad, o_s, p_s = refs[14 + 2 * nj:]
    i = pl.program_id(1)
    nq = rows // BLOCK

    @pl.when(i == 0)
    def _():
        upad[0:8, :] = jnp.zeros((8, D_MODEL), F32)
        kbuf[:, 0:BLOCK, :] = jnp.zeros((N_KV, BLOCK, KV_W), BF16)
        vbuf[:, 0:BLOCK, :] = jnp.zeros((N_KV, BLOCK, KV_W), BF16)

    @pl.when(i > 0)
    def _():
        upad[0:8, :] = upad[rows:rows + 8, :]
        kbuf[:, 0:BLOCK, :] = kbuf[:, rows:rows + BLOCK, :]
        vbuf[:, 0:BLOCK, :] = vbuf[:, rows:rows + BLOCK, :]

    _run_casts(cast_in, cast_out)
    x = x_ref[...]
    h = _rms(x, g_ref[2:3, :]).astype(BF16)

    q_s[...] = (_dot(h, wq_ref[...])
                * (HEAD_DIM ** -0.5 * LOG2E)).astype(BF16)
    k = _dot(h, win_ref[:, C_K:C_V])
    v = _dot(h, win_ref[:, C_V:C_GA])
    k16 = k.astype(BF16)
    v16 = v.astype(BF16)
    seg = _lane_segment_mask(rows)
    zero_kv = jnp.zeros((rows, KV_W), BF16)
    for hk in range(N_KV):
        kbuf[hk, BLOCK:BLOCK + rows, :] = jnp.where(seg[hk], k16, zero_kv)
        vbuf[hk, BLOCK:BLOCK + rows, :] = jnp.where(seg[hk], v16, zero_kv)

    kout_ref[...] = k[rows - BLOCK:rows, :]
    vout_ref[...] = v[rows - BLOCK:rows, :]

    q_idx = lax.broadcasted_iota(jnp.int32, (BLOCK, BLOCK), 0)
    j_idx = lax.broadcasted_iota(jnp.int32, (BLOCK, BLOCK), 1)
    from_prev = j_idx > q_idx
    prev16 = jnp.where(from_prev, 1.0, 0.0).astype(BF16)
    own16 = jnp.where(from_prev, 0.0, 1.0).astype(BF16)
    first = (i == 0).astype(jnp.int32)
    seg_q = _lane_segment_mask(BLOCK)

    for j in range(nq):
        r0 = j * BLOCK
        variant = first if j == 0 else 0
        qj = q_s[r0:r0 + BLOCK, :]
        kb = jnp.concatenate(
            [kbuf[hk, r0:r0 + 2 * BLOCK, :] for hk in range(N_KV)], axis=0)
        vb = jnp.concatenate(
            [vbuf[hk, r0:r0 + 2 * BLOCK, :] for hk in range(N_KV)], axis=0)
        qs = jnp.concatenate(
            [qj[:, g * KV_W:(g + 1) * KV_W] for g in range(GROUP)], axis=0)
        s_all = _dot_t(qs, kb)
        inv = []
        for g in range(GROUP):
            inv_g = []
            for hk in range(N_KV):
                rs = slice(g * BLOCK, (g + 1) * BLOCK)
                c0 = hk * 2 * BLOCK
                s = jnp.where(from_prev, s_all[rs, c0:c0 + BLOCK],
                              s_all[rs, c0 + BLOCK:c0 + 2 * BLOCK])
                s = s + bt_ref[variant, rs, hk * BLOCK:(hk + 1) * BLOCK]
                sink = sinks_ref[layer, hk * GROUP + g] * LOG2E
                m = jnp.maximum(jnp.max(s, axis=-1, keepdims=True), sink)
                p = jnp.exp2(s - m)
                p16 = p.astype(BF16)
                p_s[j, rs, c0:c0 + BLOCK] = p16 * prev16
                p_s[j, rs, c0 + BLOCK:c0 + 2 * BLOCK] = p16 * own16
                den = jnp.sum(p, axis=-1, keepdims=True) + jnp.exp2(sink - m)
                inv_g.append(1.0 / den)
            inv.append(inv_g)
        o_all = _dot(p_s[j], vb)
        for g in range(GROUP):
            scale = jnp.where(
                seg_q[0], inv[g][0],
                jnp.where(seg_q[1], inv[g][1],
                          jnp.where(seg_q[2], inv[g][2], inv[g][3])))
            o_s[r0:r0 + BLOCK, g * KV_W:(g + 1) * KV_W] = (
                o_all[g * BLOCK:(g + 1) * BLOCK, :] * scale).astype(BF16)

    u = _dot(h, win_ref[:, C_CC:C_CX]) * _dot(h, win_ref[:, C_CX:C_Q])
    upad[8:rows + 8, :] = u

    cst_ref[...] = u[rows - 2:rows, :]
    yc =(cw_ref[0:1, :] * upad[6:rows + 6, :]
          + cw_ref[1:2, :] * upad[7:rows + 7, :]
          + cw_ref[2:3, :] * u)
    a_in = (_dot(h, win_ref[:, C_CB:C_CC]) * yc).astype(BF16)
    a_out = _dot(a_in, wco_ref[...])
    gated_a = jax.nn.sigmoid(_dot(h, win_ref[:, C_GA:C_GB])) * a_out

    att = _dot(o_s[...], wao_ref[...])
    gate_b = jax.nn.sigmoid(_dot(h, win_ref[:, C_GB:C_END]))
    merged = (gated_a + gate_b * att).astype(BF16)
    y_ref[...] = x + _rms(_dot(merged, wo_ref[...]), g_ref[3:4, :])


def _mixer_prompt(x, batch, norm_g, w_in, w_q, conv_w, w_co, w_ao, w_o, bt,
                  sinks, layer, jobs):
    m = x.shape[0]
    seq = m // batch
    rows = MIX_ROWS
    steps = seq // rows
    c_in, c_out, c_shapes = _cast_specs(jobs, lambda b, i: b * steps + i,
                                        batch * steps)
    kern = functools.partial(_mixer_prompt_kernel, layer=layer, rows=rows,
                             cast_blocks=tuple(j.n_blocks for j in jobs))
    outs = pl.pallas_call(
        kern,
        grid=(batch, steps),
        in_specs=[
            pl.BlockSpec((rows, D_MODEL), lambda b, i: (b * steps + i, 0)),
            _layer_resident((N_NORMS, D_MODEL), layer),
            _resident((D_MODEL, C_END)),
            _resident((D_MODEL, D_MODEL)),
            _layer_resident((3, D_MODEL), layer),
            _resident((D_MODEL, D_MODEL)),
            _resident((D_MODEL, D_MODEL)),
            _resident((D_MODEL, D_MODEL)),
            _resident((2, GROUP * BLOCK, N_KV * BLOCK)),
            pl.BlockSpec(memory_space=pltpu.SMEM),
        ] + c_in,
        out_specs=[
            pl.BlockSpec((rows, D_MODEL), lambda b, i: (b * steps + i, 0)),
            pl.BlockSpec((None, 2, D_MODEL), lambda b, i: (b, 0, 0)),
            pl.BlockSpec((None, BLOCK, KV_W), lambda b, i: (b, 0, 0)),
            pl.BlockSpec((None, BLOCK, KV_W), lambda b, i: (b, 0, 0)),
        ] + c_out,
        out_shape=[
            jax.ShapeDtypeStruct((m, D_MODEL), F32),
            jax.ShapeDtypeStruct((batch, 2, D_MODEL), F32),
            jax.ShapeDtypeStruct((batch, BLOCK, KV_W), F32),
            jax.ShapeDtypeStruct((batch, BLOCK, KV_W), F32),
        ] + c_shapes,
        scratch_shapes=[
            pltpu.VMEM((rows, D_MODEL), BF16),
            pltpu.VMEM((N_KV, rows + BLOCK, KV_W), BF16),
            pltpu.VMEM((N_KV, rows + BLOCK, KV_W), BF16),
            pltpu.VMEM((rows + 8, D_MODEL), F32),
            pltpu.VMEM((rows, D_MODEL), BF16),
            pltpu.VMEM((rows // BLOCK, GROUP * BLOCK, N_KV * 2 * BLOCK),
                       BF16),
        ],
        compiler_params=pltpu.CompilerParams(
            dimension_semantics=("arbitrary", "arbitrary"),
            vmem_limit_bytes=VMEM_LIMIT_BYTES,
        ),
        name="mixer_prompt",
    )(x, norm_g, w_in, w_q, conv_w, w_co, w_ao, w_o, bt, sinks,
      *[j.src for j in jobs])
    return outs[0], outs[1], outs[2], outs[3], outs[4:]


def _mixer_decode_kernel(*refs, n_alias):
    (x_ref, st_ref, kc_ref, vc_ref, g_ref, win_ref, wq_ref, cw_ref, wco_ref,
     wao_ref, wo_ref, bdec_ref, sink_ref) = refs[:13]
    (y_ref, nst_ref, kout_ref, vout_ref, h_s, q_s, knew_s, vnew_s, knewt_s,
     vnewt_s, ag_s, o_s) = refs[13 + n_alias:]
    i = pl.program_id(0)
    last = pl.num_programs(0) - 1

    @pl.when(i == 0)
    def _():
        x = x_ref[...]
        h = _rms(x, g_ref[2:3, :]).astype(BF16)
        h_s[...] = h
        u = _dot(h, win_ref[:, C_CC:C_CX]) * _dot(h, win_ref[:, C_CX:C_Q])
        st0 = st_ref[:, 0:D_MODEL]
        st1 = st_ref[:, D_MODEL:2 * D_MODEL]
        yc = cw_ref[0:1, :] * st0 + cw_ref[1:2, :] * st1 + cw_ref[2:3, :] * u
        nst_ref[:, 0:D_MODEL] = st1
        nst_ref[:, D_MODEL:2 * D_MODEL] = u
        a_in = (_dot(h, win_ref[:, C_CB:C_CC]) * yc).astype(BF16)
        a_out = _dot(a_in, wco_ref[...])
        ag_s[...] = jax.nn.sigmoid(_dot(h, win_ref[:, C_GA:C_GB])) * a_out
        q_s[...] = _dot(h, wq_ref[...]) * (HEAD_DIM ** -0.5)
        knew = _dot(h, win_ref[:, C_K:C_V])
        vnew = _dot(h, win_ref[:, C_V:C_GA])
        knew_s[...] = knew
        vnew_s[...] = vnew
        knewt_s[...] = knew.T
        vnewt_s[...] = vnew.T

    row = lax.broadcasted_iota(jnp.int32, (N_HEADS, KV_W), 0)
    lane = lax.broadcasted_iota(jnp.int32, (N_HEADS, KV_W), 1)
    row_g = row // N_KV
    seg = (lane // HEAD_DIM) == (row % N_KV)
    key0 = lax.broadcasted_iota(jnp.int32, (N_HEADS, WINDOW), 1) == 0
    newest = lax.broadcasted_iota(jnp.int32, (KV_W, WINDOW), 1) == WINDOW - 1
    bias = bdec_ref[...]
    sink = sink_ref[...]

    for b in range(DEC_SEQS):
        to_last = (WINDOW - 1) - (i * DEC_SEQS + b)
        kout_ref[b] = jnp.where(
            newest, pltpu.roll(knewt_s[...], to_last, axis=1),
            pltpu.roll(kc_ref[b], WINDOW - 1, axis=1))
        vout_ref[b] = jnp.where(
            newest, pltpu.roll(vnewt_s[...], to_last, axis=1),
            pltpu.roll(vc_ref[b], WINDOW - 1, axis=1))

    qms, k_rows, v_rows = [], [], []
    for b in range(DEC_SEQS):
        r = i * DEC_SEQS + b
        q_row = q_s[pl.ds(r, 1), :]
        pieces = [jnp.broadcast_to(q_row[:, g * KV_W:(g + 1) * KV_W],
                                   (N_HEADS, KV_W)) for g in range(GROUP)]
        q_rows = jnp.where(row_g == 0, pieces[0],
                           jnp.where(row_g == 1, pieces[1],
                                     jnp.where(row_g == 2, pieces[2],
                                               pieces[3])))
        qms.append(jnp.where(seg, q_rows, 0.0))
        k_rows.append(knew_s[pl.ds(r, 1), :])
        v_rows.append(vnew_s[pl.ds(r, 1), :])
    qm = jnp.stack(qms).astype(BF16)
    k_row = jnp.stack(k_rows).astype(BF16).astype(F32)
    v_row = jnp.stack(v_rows).astype(BF16).astype(F32)
    s_new = jnp.sum(qm.astype(F32) * k_row, axis=-1, keepdims=True)
    s_old = jnp.einsum('bqd,bdk->bqk', qm, kc_ref[...].astype(BF16),
                       preferred_element_type=F32)
    s = jnp.where(key0, s_new, s_old) + bias
    m = jnp.maximum(jnp.max(s, axis=-1, keepdims=True), sink)
    p = jnp.exp(s - m)
    den = jnp.sum(p, axis=-1, keepdims=True) + jnp.exp(sink - m)
    p16 = p.astype(BF16)
    p_new = p16[:, :, 0:1].astype(F32)
    p_old = jnp.where(key0, jnp.zeros_like(p16), p16)
    o = jnp.einsum('bqk,bdk->bqd', p_old, vc_ref[...].astype(BF16),
                   preferred_element_type=F32)
    om = jnp.where(seg, (o + p_new * v_row) / den, 0.0)
    for b in range(DEC_SEQS):
        r = i * DEC_SEQS + b
        for g in range(GROUP):
            o_s[pl.ds(r, 1), g * KV_W:(g + 1) * KV_W] = jnp.sum(
                om[b, g * N_KV:(g + 1) * N_KV, :], axis=0, keepdims=True)

    @pl.when(i == last)
    def _():
        att = _dot(o_s[...].astype(BF16), wao_ref[...])
        gate_b = jax.nn.sigmoid(_dot(h_s[...], win_ref[:, C_GB:C_END]))
        merged = (ag_s[...] + gate_b * att).astype(BF16)
        y_ref[...] = x_ref[...] + _rms(_dot(merged, wo_ref[...]),
                                       g_ref[3:4, :])


def _mixer_decode(x, st, kc, vc, norm_g, w_in, w_q, conv_w, w_co, w_ao, w_o,
                  bdec, sink_col, layer, prev_caches):
    n = x.shape[0]
    sb = DEC_SEQS
    cache_spec = pl.BlockSpec((None, sb, KV_W, WINDOW),
                              lambda i: (layer, i, 0, 0))
    n_alias = len(prev_caches)
    n_in = 13
    return pl.pallas_call(
        functools.partial(_mixer_decode_kernel, n_alias=n_alias),
        grid=(n // sb,),
        in_specs=[
            _resident((n, D_MODEL)),
            _resident((n, 2 * D_MODEL)),
            cache_spec,
            cache_spec,
            _layer_resident((N_NORMS, D_MODEL), layer),
            _resident((D_MODEL, C_END)),
            _resident((D_MODEL, D_MODEL)),
            _layer_resident((3, D_MODEL), layer),
            _resident((D_MODEL, D_MODEL)),
            _resident((D_MODEL, D_MODEL)),
            _resident((D_MODEL, D_MODEL)),
            _resident((N_HEADS, WINDOW)),
            _resident((N_HEADS, 1)),
        ] + [pl.BlockSpec(memory_space=pl.ANY)] * n_alias,
        out_specs=[
            pl.BlockSpec((n, D_MODEL), lambda i: (0, 0)),
            pl.BlockSpec((n, 2 * D_MODEL), lambda i: (0, 0)),
            cache_spec,
            cache_spec,
        ],
        out_shape=[
            jax.ShapeDtypeStruct((n, D_MODEL), F32),
            jax.ShapeDtypeStruct((n, 2 * D_MODEL), F32),
            jax.ShapeDtypeStruct(kc.shape, F32),
            jax.ShapeDtypeStruct(vc.shape, F32),
        ],
        input_output_aliases={n_in + a: 2 + a for a in range(n_alias)},
        scratch_shapes=[
            pltpu.VMEM((n, D_MODEL), BF16),
            pltpu.VMEM((n, D_MODEL), F32),
            pltpu.VMEM((n, KV_W), F32),
            pltpu.VMEM((n, KV_W), F32),
            pltpu.VMEM((KV_W, n), F32),
            pltpu.VMEM((KV_W, n), F32),
            pltpu.VMEM((n, D_MODEL), F32),
            pltpu.VMEM((n, D_MODEL), F32),
        ],
        compiler_params=pltpu.CompilerParams(
            dimension_semantics=("arbitrary",),
            vmem_limit_bytes=VMEM_LIMIT_BYTES),
        name="mixer_decode",
    )(x, st, kc, vc, norm_g, w_in, w_q, conv_w, w_co, w_ao, w_o, bdec,
      sink_col, *prev_caches)


def kernel(x_prompt, x_sample, state_conv, cache_k_win, cache_v_win, rel_bias,
           norm_g, w_ff1_gu, w_ff1_down, w_in, conv_w, sinks, w_conv_out,
           w_attn_out, w_out, w_ff2_gu, w_ff2_down):
    batch, seq, _ = x_prompt.shape
    n_dec = x_sample.shape[0]

    ffn_steps = (batch * seq) // FFN_ROWS
    mix_steps = (batch * seq) // MIX_ROWS
    bt_plain, bt, ffn_w = _bias_table(
        rel_bias, _ffn_cast_jobs(w_ff1_gu, w_ff1_down, 0, BIAS_STEPS))
    bdec = bt_plain[0].reshape(GROUP, BLOCK, N_KV, BLOCK)[:, BLOCK - 1]
    bdec = bdec.reshape(N_HEADS, WINDOW)
    bdec = jnp.concatenate([bdec[:, WINDOW - 1:], bdec[:, :WINDOW - 1]], axis=1)

    def keys_minor(c):
        return jnp.transpose(c, (0, 1, 3, 4, 2)).reshape(
            DEPTH, n_dec, KV_W, WINDOW)

    def keys_major(c):
        return jnp.transpose(
            c.reshape(DEPTH, n_dec, N_KV, HEAD_DIM, WINDOW), (0, 1, 4, 2, 3))

    kct = keys_minor(cache_k_win)
    vct = keys_minor(cache_v_win)

    xp = x_prompt.reshape(batch * seq, D_MODEL)
    xs = x_sample.reshape(n_dec, D_MODEL)
    pc, pk, pv, sc = [], [], [], []
    new_caches = ()
    for l in range(DEPTH):
        sink_col = sinks[l].reshape(N_KV, GROUP).T.reshape(N_HEADS, 1)

        xp, xs, mix_w = _ffn(
            xp, xs, norm_g, ffn_w[0], ffn_w[1], l, 0,
            _mixer_cast_jobs(w_in, w_conv_out, w_attn_out, w_out, l,
                             ffn_steps))
        win, wco, wao, wo, wq = mix_w

        xp, c1, k1, v1, ffn_w = _mixer_prompt(
            xp, batch, norm_g, win, wq, conv_w, wco, wao, wo, bt, sinks, l,
            _ffn_cast_jobs(w_ff2_gu, w_ff2_down, l, mix_steps))
        xs, c2, k2, v2 = _mixer_decode(
            xs, state_conv[l].reshape(n_dec, 2 * D_MODEL), kct, vct,
            norm_g, win, wq, conv_w, wco, wao, wo, bdec, sink_col, l,
            new_caches)
        new_caches = (k2, v2)

        next_jobs = (_ffn_cast_jobs(w_ff1_gu, w_ff1_down, l + 1, ffn_steps)
                     if l + 1 < DEPTH else [])
        xp, xs, ffn_w = _ffn(xp, xs, norm_g, ffn_w[0], ffn_w[1], l, 4,
                             next_jobs)

        pc.append(c1)
        pk.append(k1.reshape(batch, WINDOW, N_KV, HEAD_DIM))
        pv.append(v1.reshape(batch, WINDOW, N_KV, HEAD_DIM))
        sc.append(c2.reshape(n_dec, 2, D_MODEL))

    return (xp.reshape(batch, seq, D_MODEL), xs.reshape(n_dec, 1, D_MODEL),
            jnp.stack(pc), jnp.stack(pk), jnp.stack(pv),
            jnp.stack(sc), keys_major(new_caches[0]),
            keys_major(new_caches[1]))
```

```python
import functools
from typing import Callable, NamedTuple

import numpy as np
import jax
import jax.numpy as jnp
from jax import lax
from jax.experimental import pallas as pl
from jax.experimental.pallas import tpu as pltpu

F32 = jnp.float32
BF16 = jnp.bfloat16

D_MODEL = 1024
D_FF = 2816
N_HEADS = 16
N_KV = 4
GROUP = N_HEADS // N_KV
HEAD_DIM = 64
KV_W = N_KV * HEAD_DIM
WINDOW = 128
BLOCK = 128
N_BUCKETS = 32
MAX_DISTANCE = 128
RMS_EPS = 1e-6
NEG = -1e30
LOG2E = 1.4426950408889634
DEPTH = 2
N_NORMS = 6

C_CB, C_CC, C_CX, C_Q, C_K, C_V, C_GA, C_GB, C_END = (
    0, 1024, 2048, 3072, 4096, 4352, 4608, 5632, 6656)

VMEM_LIMIT_BYTES = 56 * 1024 * 1024

FFN_ROWS = 1024
FFN_SUB_ROWS = 512
FFN_CHUNK = 256
MIX_ROWS = 512
DEC_SEQS = 16
BIAS_STEPS = N_HEADS


def _t5_bucket_np(rel):
    n = np.maximum(rel, 0)
    max_exact = N_BUCKETS // 2
    nf = np.maximum(n, 1).astype(np.float32)
    large = max_exact + (
        np.log(nf / max_exact) / np.log(MAX_DISTANCE / max_exact)
        * (N_BUCKETS - max_exact)).astype(np.int32)
    large = np.minimum(large, N_BUCKETS - 1)
    return np.where(n < max_exact, n, large).astype(np.int32)


def _band_tables():
    qi = np.arange(BLOCK)[:, None]
    kj = np.arange(BLOCK)[None, :]
    rel = np.where(kj > qi, qi + BLOCK - kj, qi - kj)
    bucket = _t5_bucket_np(rel)
    mask = np.stack([np.ones_like(rel, bool), kj <= qi]).astype(np.float32)
    return bucket, mask


_BUCKET_NP, _MASK_NP = _band_tables()


def _head_perm():
    old = np.arange(D_MODEL)
    hk, g, d = old // KV_W, (old // HEAD_DIM) % GROUP, old % HEAD_DIM
    perm = np.zeros((D_MODEL, D_MODEL), np.float32)
    perm[old, g * KV_W + hk * HEAD_DIM + d] = 1.0
    return perm


_HEAD_PERM_NP = _head_perm()


def _rms(x, g):
    r = lax.rsqrt(jnp.mean(x * x, axis=-1, keepdims=True) + RMS_EPS)
    return (x * r) * g


def _dot(a, b):
    return jnp.dot(a, b, preferred_element_type=F32)


def _dot_t(a, b):
    return lax.dot_general(a, b, (((1,), (1,)), ((), ())),
                           preferred_element_type=F32)


def _resident(shape):
    return pl.BlockSpec(shape, lambda *_: (0,) * len(shape),
                        pipeline_mode=pl.Buffered(1))


def _layer_resident(shape, layer):
    return pl.BlockSpec((None,) + tuple(shape),
                        lambda *_: (layer,) + (0,) * len(shape),
                        pipeline_mode=pl.Buffered(1))


class _CastJob(NamedTuple):
    src: jax.Array
    layer: int
    rows: int
    src_block: Callable = lambda d: d
    col_block: int | None = None
    head_perm: bool = False

    @property
    def n_blocks(self):
        return self.src.shape[1] // self.rows


def _cast_rows(total_rows, host_steps, align=16):
    rows = align
    while total_rows % rows or total_rows // rows > host_steps:
        rows += align
    return rows


def _cast_specs(jobs, step_of, host_steps):
    in_specs, out_specs, out_shapes = [], [], []
    for job in jobs:
        assert job.n_blocks <= host_steps, (job.n_blocks, host_steps)
        cols = job.src.shape[2] if job.col_block is None else D_MODEL
        col = job.col_block or 0

        def dst(*ids, job=job):
            return jnp.minimum(step_of(*ids), job.n_blocks - 1)

        in_specs.append(pl.BlockSpec(
            (None, job.rows, cols),
            lambda *ids, job=job, dst=dst, col=col: (
                job.layer, job.src_block(dst(*ids)), col)))
        out_specs.append(pl.BlockSpec(
            (job.rows, cols), lambda *ids, dst=dst: (dst(*ids), 0)))
        out_shapes.append(
            jax.ShapeDtypeStruct((job.src.shape[1], cols), BF16))
    return in_specs, out_specs, out_shapes


def _run_casts(in_refs, out_refs, perm_blocks=None):
    for n, (src, dst) in enumerate(zip(in_refs, out_refs)):
        if perm_blocks is None or not perm_blocks[n]:
            dst[...] = src[...].astype(BF16)


def _run_perm_casts(in_refs, out_refs, perm_blocks, perm_ref, step):
    for n, (src, dst) in enumerate(zip(in_refs, out_refs)):
        if perm_blocks[n]:
            @pl.when(step < perm_blocks[n])
            def _(src=src, dst=dst):
                dst[...] = _dot(src[...].astype(BF16),
                                perm_ref[...]).astype(BF16)


def _ffn_cast_jobs(w_gu, w_down, layer, host_steps):
    return [_CastJob(w_gu, layer, _cast_rows(D_MODEL, host_steps)),
            _CastJob(w_down, layer, _cast_rows(D_FF, host_steps))]


def _mixer_cast_jobs(w_in, w_co, w_ao, w_o, layer, host_steps):
    to_g_major = lambda d: (d % GROUP) * N_KV + d // GROUP
    rows = _cast_rows(D_MODEL, host_steps)
    assert N_HEADS <= host_steps
    return [_CastJob(w_in, layer, rows), _CastJob(w_co, layer, rows),
            _CastJob(w_ao, layer, HEAD_DIM, to_g_major),
            _CastJob(w_o, layer, rows),
            _CastJob(w_in, layer, 2 * BLOCK, col_block=C_Q // D_MODEL,
                     head_perm=True)]


def _bias_kernel(*refs, cast_blocks):
    nj = len(cast_blocks)
    rb_ref, bucket_ref, mask_ref = refs[:3]
    cast_in = refs[3:3 + nj]
    plain_ref, exp2_ref = refs[3 + nj:5 + nj]
    cast_out = refs[5 + nj:]
    g = pl.program_id(0)
    hk = pl.program_id(1)
    _run_casts(cast_in, cast_out)
    h = hk * GROUP + g
    bucket = bucket_ref[...]

    def body(b, acc):
        return jnp.where(bucket == b, rb_ref[b, h], acc)

    bias = lax.fori_loop(0, N_BUCKETS, body, jnp.zeros((BLOCK, BLOCK), F32))
    bias2 = bias * LOG2E
    for variant in range(2):
        valid = mask_ref[variant] > 0.0
        plain_ref[variant] = jnp.where(valid, bias, NEG)
        exp2_ref[variant] = jnp.where(valid, bias2, NEG)


def _bias_table(rel_bias, jobs):
    c_in, c_out, c_shapes = _cast_specs(
        jobs, lambda g, hk: g * N_KV + hk, BIAS_STEPS)
    table_spec = pl.BlockSpec((2, BLOCK, BLOCK), lambda g, hk: (0, g, hk))
    table_shape = jax.ShapeDtypeStruct((2, GROUP * BLOCK, N_KV * BLOCK), F32)
    outs = pl.pallas_call(
        functools.partial(_bias_kernel,
                          cast_blocks=tuple(j.n_blocks for j in jobs)),
        grid=(GROUP, N_KV),
        in_specs=[
            pl.BlockSpec(memory_space=pltpu.SMEM),
            pl.BlockSpec((BLOCK, BLOCK), lambda g, hk: (0, 0)),
            pl.BlockSpec((2, BLOCK, BLOCK), lambda g, hk: (0, 0, 0)),
        ] + c_in,
        out_specs=[table_spec, table_spec] + c_out,
        out_shape=[table_shape, table_shape] + c_shapes,
        compiler_params=pltpu.CompilerParams(
            dimension_semantics=("arbitrary",) * 2),
        name="bias_table",
    )(rel_bias, jnp.asarray(_BUCKET_NP), jnp.asarray(_MASK_NP),
      *[j.src for j in jobs])
    return outs[0], outs[1], outs[2:]


def _ffn_rows(x, g_ref, wgu_ref, wd_ref, act_ref, gi):
    rows = x.shape[0]
    h = _rms(x, g_ref[gi:gi + 1, :]).astype(BF16)
    for c in range(D_FF // FFN_CHUNK):
        lo = c * FFN_CHUNK
        gate = _dot(h, wgu_ref[:, lo:lo + FFN_CHUNK])
        up = _dot(h, wgu_ref[:, D_FF + lo:D_FF + lo + FFN_CHUNK])
        act_ref[0:rows, lo:lo + FFN_CHUNK] = (
            gate * jax.nn.sigmoid(gate) * up).astype(BF16)
    y = _dot(act_ref[0:rows, :], wd_ref[...])
    return x + 0.5 * _rms(y, g_ref[gi + 1:gi + 2, :])


def _ffn_kernel(*refs, gi, prompt_steps, cast_perm):
    nj = len(cast_perm)
    n_perm = 1 if any(cast_perm) else 0
    xp_ref, xs_ref, g_ref, wgu_ref, wd_ref = refs[:5]
    n_in = 5 + n_perm
    cast_in = refs[n_in:n_in + nj]
    op_ref, os_ref = refs[n_in + nj:n_in + nj + 2]
    cast_out = refs[n_in + nj + 2:n_in + 2 * nj + 2]
    act_refs = refs[n_in + 2 * nj + 2:]
    i = pl.program_id(0)
    if n_perm:
        _run_perm_casts(cast_in, cast_out, cast_perm, refs[5], i)

    @pl.when(i < prompt_steps)
    def _():
        _run_casts(cast_in, cast_out, cast_perm)
        for t, act_ref in enumerate(act_refs):
            rs = slice(t * FFN_SUB_ROWS, (t + 1) * FFN_SUB_ROWS)
            op_ref[rs, :] = _ffn_rows(xp_ref[rs, :], g_ref, wgu_ref, wd_ref,
                                      act_ref, gi)

    @pl.when(i == prompt_steps)
    def _():
        os_ref[...] = _ffn_rows(xs_ref[...], g_ref, wgu_ref, wd_ref,
                                act_refs[0], gi)


def _ffn(xp, xs, norm_g, w_gu, w_down, layer, gi, jobs):
    m = xp.shape[0]
    n = xs.shape[0]
    tm = FFN_ROWS
    steps = m // tm
    row_block = lambda i: (jnp.minimum(i, steps - 1), 0)
    c_in, c_out, c_shapes = _cast_specs(jobs, lambda i: i, steps)
    cast_perm = tuple(j.n_blocks if j.head_perm else 0 for j in jobs)
    perm_specs, perm_args = [], []
    if any(cast_perm):
        perm_specs = [_resident((D_MODEL, D_MODEL))]
        perm_args = [jnp.asarray(_HEAD_PERM_NP, BF16)]
    outs = pl.pallas_call(
        functools.partial(_ffn_kernel, gi=gi, prompt_steps=steps,
                          cast_perm=cast_perm),
        grid=(steps + 1,),
        in_specs=[
            pl.BlockSpec((tm, D_MODEL), row_block),
            _resident((n, D_MODEL)),
            _layer_resident((N_NORMS, D_MODEL), layer),
            _resident((D_MODEL, 2 * D_FF)),
            _resident((D_FF, D_MODEL)),
        ] + perm_specs + c_in,
        out_specs=[
            pl.BlockSpec((tm, D_MODEL), row_block),
            pl.BlockSpec((n, D_MODEL), lambda i: (0, 0)),
        ] + c_out,
        out_shape=[
            jax.ShapeDtypeStruct((m, D_MODEL), F32),
            jax.ShapeDtypeStruct((n, D_MODEL), F32),
        ] + c_shapes,
        scratch_shapes=[pltpu.VMEM((FFN_SUB_ROWS, D_FF), BF16)
                        for _ in range(FFN_ROWS // FFN_SUB_ROWS)],
        compiler_params=pltpu.CompilerParams(
            dimension_semantics=("arbitrary",),
            vmem_limit_bytes=VMEM_LIMIT_BYTES),
        name="ffn_half",
    )(xp, xs, norm_g, w_gu, w_down, *perm_args, *[j.src for j in jobs])
    return outs[0], outs[1], outs[2:]


def _lane_segment_mask(rows):
    lane = lax.broadcasted_iota(jnp.int32, (rows, KV_W), 1)
    return [(lane >= hk * HEAD_DIM) & (lane < (hk + 1) * HEAD_DIM)
            for hk in range(N_KV)]


def _mixer_prompt_kernel(*refs, layer, rows, cast_blocks):
    nj = len(cast_blocks)
    (x_ref, g_ref, win_ref, wq_ref, cw_ref, wco_ref, wao_ref, wo_ref, bt_ref,
     sinks_ref) = refs[:10]
    cast_in = refs[10:10 + nj]
    y_ref, cst_ref, kout_ref, vout_ref = refs[10 + nj:14 + nj]
    cast_out = refs[14 + nj:14 + 2 * nj]
    q_s, kbuf, vbuf, upad, o_s, p_s = refs[14 + 2 * nj:]
    i = pl.program_id(1)
    nq = rows // BLOCK

    @pl.when(i == 0)
    def _():
        upad[0:8, :] = jnp.zeros((8, D_MODEL), F32)
        kbuf[:, 0:BLOCK, :] = jnp.zeros((N_KV, BLOCK, KV_W), BF16)
        vbuf[:, 0:BLOCK, :] = jnp.zeros((N_KV, BLOCK, KV_W), BF16)

    @pl.when(i > 0)
    def _():
        upad[0:8, :] = upad[rows:rows + 8, :]
        kbuf[:, 0:BLOCK, :] = kbuf[:, rows:rows + BLOCK, :]
        vbuf[:, 0:BLOCK, :] = vbuf[:, rows:rows + BLOCK, :]

    _run_casts(cast_in, cast_out)
    x = x_ref[...]
    h = _rms(x, g_ref[2:3, :]).astype(BF16)

    q_s[...] = (_dot(h, wq_ref[...])
                * (HEAD_DIM ** -0.5 * LOG2E)).astype(BF16)
    k = _dot(h, win_ref[:, C_K:C_V])
    v = _dot(h, win_ref[:, C_V:C_GA])
    k16 = k.astype(BF16)
    v16 = v.astype(BF16)
    seg = _lane_segment_mask(rows)
    zero_kv = jnp.zeros((rows, KV_W), BF16)
    for hk in range(N_KV):
        kbuf[hk, BLOCK:BLOCK + rows, :] = jnp.where(seg[hk], k16, zero_kv)
        vbuf[hk, BLOCK:BLOCK + rows, :] = jnp.where(seg[hk], v16, zero_kv)

    kout_ref[...] = k[rows - BLOCK:rows, :]
    vout_ref[...] = v[rows - BLOCK:rows, :]

    q_idx = lax.broadcasted_iota(jnp.int32, (BLOCK, BLOCK), 0)
    j_idx = lax.broadcasted_iota(jnp.int32, (BLOCK, BLOCK), 1)
    from_prev = j_idx > q_idx
    prev16 = jnp.where(from_prev, 1.0, 0.0).astype(BF16)
    own16 = jnp.where(from_prev, 0.0, 1.0).astype(BF16)
    first = (i == 0).astype(jnp.int32)
    seg_q = _lane_segment_mask(BLOCK)

    for j in range(nq):
        r0 = j * BLOCK
        variant = first if j == 0 else 0
        qj = q_s[r0:r0 + BLOCK, :]
        kb = jnp.concatenate(
            [kbuf[hk, r0:r0 + 2 * BLOCK, :] for hk in range(N_KV)], axis=0)
        vb = jnp.concatenate(
            [vbuf[hk, r0:r0 + 2 * BLOCK, :] for hk in range(N_KV)], axis=0)
        qs = jnp.concatenate(
            [qj[:, g * KV_W:(g + 1) * KV_W] for g in range(GROUP)], axis=0)
        s_all = _dot_t(qs, kb)
        inv = []
        for g in range(GROUP):
            inv_g = []
            for hk in range(N_KV):
                rs = slice(g * BLOCK, (g + 1) * BLOCK)
                c0 = hk * 2 * BLOCK
                s = jnp.where(from_prev, s_all[rs, c0:c0 + BLOCK],
                              s_all[rs, c0 + BLOCK:c0 + 2 * BLOCK])
                s = s + bt_ref[variant, rs, hk * BLOCK:(hk + 1) * BLOCK]
                sink = sinks_ref[layer, hk * GROUP + g] * LOG2E
                m = jnp.maximum(jnp.max(s, axis=-1, keepdims=True), sink)
                p = jnp.exp2(s - m)
                p16 = p.astype(BF16)
                p_s[j, rs, c0:c0 + BLOCK] = p16 * prev16
                p_s[j, rs, c0 + BLOCK:c0 + 2 * BLOCK] = p16 * own16
                den = jnp.sum(p, axis=-1, keepdims=True) + jnp.exp2(sink - m)
                inv_g.append(1.0 / den)
            inv.append(inv_g)
        o_all = _dot(p_s[j], vb)
        for g in range(GROUP):
            scale = jnp.where(
                seg_q[0], inv[g][0],
                jnp.where(seg_q[1], inv[g][1],
                          jnp.where(seg_q[2], inv[g][2], inv[g][3])))
            o_s[r0:r0 + BLOCK, g * KV_W:(g + 1) * KV_W] = (
                o_all[g * BLOCK:(g + 1) * BLOCK, :] * scale).astype(BF16)

    u = _dot(h, win_ref[:, C_CC:C_CX]) * _dot(h, win_ref[:, C_CX:C_Q])
    upad[8:rows + 8, :] = u
    cst_ref[...] = u[rows - 2:rows, :]
    yc = (cw_ref[0:1, :] * upad[6:rows + 6, :]
          + cw_ref[1:2, :] * upad[7:rows + 7, :]
          + cw_ref[2:3, :] * u)
    a_in = (_dot(h, win_ref[:, C_CB:C_CC]) * yc).astype(BF16)
    a_out = _dot(a_in, wco_ref[...])
    gated_a = jax.nn.sigmoid(_dot(h, win_ref[:, C_GA:C_GB])) * a_out

    att = _dot(o_s[...], wao_ref[...])
    gate_b = jax.nn.sigmoid(_dot(h, win_ref[:, C_GB:C_END]))
    merged = (gated_a + gate_b * att).astype(BF16)
    y_ref[...] = x + _rms(_dot(merged, wo_ref[...]), g_ref[3:4, :])


def _mixer_prompt(x, batch, norm_g, w_in, w_q, conv_w, w_co, w_ao, w_o, bt,
                  sinks, layer, jobs):
    m = x.shape[0]
    seq = m // batch
    rows = MIX_ROWS
    steps = seq // rows
    c_in, c_out, c_shapes = _cast_specs(jobs, lambda b, i: b * steps + i,
                                        batch * steps)
    kern = functools.partial(_mixer_prompt_kernel, layer=layer, rows=rows,
                             cast_blocks=tuple(j.n_blocks for j in jobs))
    outs = pl.pallas_call(
        kern,
        grid=(batch, steps),
        in_specs=[
            pl.BlockSpec((rows, D_MODEL), lambda b, i: (b * steps + i, 0)),
            _layer_resident((N_NORMS, D_MODEL), layer),
            _resident((D_MODEL, C_END)),
            _resident((D_MODEL, D_MODEL)),
            _layer_resident((3, D_MODEL), layer),
            _resident((D_MODEL, D_MODEL)),
            _resident((D_MODEL, D_MODEL)),
            _resident((D_MODEL, D_MODEL)),
            _resident((2, GROUP * BLOCK, N_KV * BLOCK)),
            pl.BlockSpec(memory_space=pltpu.SMEM),
        ] + c_in,
        out_specs=[
            pl.BlockSpec((rows, D_MODEL), lambda b, i: (b * steps + i, 0)),
            pl.BlockSpec((None, 2, D_MODEL), lambda b, i: (b, 0, 0)),
            pl.BlockSpec((None, BLOCK, KV_W), lambda b, i: (b, 0, 0)),
            pl.BlockSpec((None, BLOCK, KV_W), lambda b, i: (b, 0, 0)),
        ] + c_out,
        out_shape=[
            jax.ShapeDtypeStruct((m, D_MODEL), F32),
            jax.ShapeDtypeStruct((batch, 2, D_MODEL), F32),
            jax.ShapeDtypeStruct((batch, BLOCK, KV_W), F32),
            jax.ShapeDtypeStruct((batch, BLOCK, KV_W), F32),
        ] + c_shapes,
        scratch_shapes=[
            pltpu.VMEM((rows, D_MODEL), BF16),
            pltpu.VMEM((N_KV, rows + BLOCK, KV_W), BF16),
            pltpu.VMEM((N_KV, rows + BLOCK, KV_W), BF16),
            pltpu.VMEM((rows + 8, D_MODEL), F32),
            pltpu.VMEM((rows, D_MODEL), BF16),
            pltpu.VMEM((rows // BLOCK, GROUP * BLOCK, N_KV * 2 * BLOCK),
                       BF16),
        ],
        compiler_params=pltpu.CompilerParams(
            dimension_semantics=("arbitrary", "arbitrary"),
            vmem_limit_bytes=VMEM_LIMIT_BYTES,
        ),
        name="mixer_prompt",
    )(x, norm_g, w_in, w_q, conv_w, w_co, w_ao, w_o, bt, sinks,
      *[j.src for j in jobs])
    return outs[0], outs[1], outs[2], outs[3], outs[4:]


def _mixer_decode_kernel(*refs, n_alias):
    (x_ref, st_ref, kc_ref, vc_ref, g_ref, win_ref, wq_ref, cw_ref, wco_ref,
     wao_ref, wo_ref, bdec_ref, sink_ref) = refs[:13]
    (y_ref, nst_ref, kout_ref, vout_ref, h_s, q_s, knew_s, vnew_s, knewt_s,
     vnewt_s, ag_s, o_s) = refs[13 + n_alias:]
    i = pl.program_id(0)
    last = pl.num_programs(0) - 1

    @pl.when(i == 0)
    def _():
        x = x_ref[...]
        h = _rms(x, g_ref[2:3, :]).astype(BF16)
        h_s[...] = h
        u = _dot(h, win_ref[:, C_CC:C_CX]) * _dot(h, win_ref[:, C_CX:C_Q])
        st0 = st_ref[:, 0:D_MODEL]
        st1 = st_ref[:, D_MODEL:2 * D_MODEL]
        yc = cw_ref[0:1, :] * st0 + cw_ref[1:2, :] * st1 + cw_ref[2:3, :] * u
        nst_ref[:, 0:D_MODEL] = st1
        nst_ref[:, D_MODEL:2 * D_MODEL] = u
        a_in = (_dot(h, win_ref[:, C_CB:C_CC]) * yc).astype(BF16)
        a_out = _dot(a_in, wco_ref[...])
        ag_s[...] = jax.nn.sigmoid(_dot(h, win_ref[:, C_GA:C_GB])) * a_out
        q_s[...] = _dot(h, wq_ref[...]) * (HEAD_DIM ** -0.5)
        knew = _dot(h, win_ref[:, C_K:C_V])
        vnew = _dot(h, win_ref[:, C_V:C_GA])
        knew_s[...] = knew
        vnew_s[...] = vnew
        knewt_s[...] = knew.T
        vnewt_s[...] = vnew.T

    row = lax.broadcasted_iota(jnp.int32, (N_HEADS, KV_W), 0)
    lane = lax.broadcasted_iota(jnp.int32, (N_HEADS, KV_W), 1)
    row_g = row // N_KV
    seg = (lane // HEAD_DIM) == (row % N_KV)
    key0 = lax.broadcasted_iota(jnp.int32, (N_HEADS, WINDOW), 1) == 0
    newest = lax.broadcasted_iota(jnp.int32, (KV_W, WINDOW), 1) == WINDOW - 1
    bias = bdec_ref[...]
    sink = sink_ref[...]

    for b in range(DEC_SEQS):
        to_last = (WINDOW - 1) - (i * DEC_SEQS + b)
        kout_ref[b] = jnp.where(
            newest, pltpu.roll(knewt_s[...], to_last, axis=1),
            pltpu.roll(kc_ref[b], WINDOW - 1, axis=1))
        vout_ref[b] = jnp.where(
            newest, pltpu.roll(vnewt_s[...], to_last, axis=1),
            pltpu.roll(vc_ref[b], WINDOW - 1, axis=1))

    qms, k_rows, v_rows = [], [], []
    for b in range(DEC_SEQS):
        r = i * DEC_SEQS + b
        q_row = q_s[pl.ds(r, 1), :]
        pieces = [jnp.broadcast_to(q_row[:, g * KV_W:(g + 1) * KV_W],
                                   (N_HEADS, KV_W)) for g in range(GROUP)]
        q_rows = jnp.where(row_g == 0, pieces[0],
                           jnp.where(row_g == 1, pieces[1],
                                     jnp.where(row_g == 2, pieces[2],
                                               pieces[3])))
        qms.append(jnp.where(seg, q_rows, 0.0))
        k_rows.append(knew_s[pl.ds(r, 1), :])
        v_rows.append(vnew_s[pl.ds(r, 1), :])
    qm = jnp.stack(qms).astype(BF16)
    k_row = jnp.stack(k_rows).astype(BF16).astype(F32)
    v_row = jnp.stack(v_rows).astype(BF16).astype(F32)
    s_new = jnp.sum(qm.astype(F32) * k_row, axis=-1, keepdims=True)
    s_old = jnp.einsum('bqd,bdk->bqk', qm, kc_ref[...].astype(BF16),
                       preferred_element_type=F32)
    s = jnp.where(key0, s_new, s_old) + bias
    m = jnp.maximum(jnp.max(s, axis=-1, keepdims=True), sink)
    p = jnp.exp(s - m)
    den = jnp.sum(p, axis=-1, keepdims=True) + jnp.exp(sink - m)
    p16 = p.astype(BF16)
    p_new = p16[:, :, 0:1].astype(F32)
    p_old = jnp.where(key0, jnp.zeros_like(p16), p16)
    o = jnp.einsum('bqk,bdk->bqd', p_old, vc_ref[...].astype(BF16),
                   preferred_element_type=F32)
    om = jnp.where(seg, (o + p_new * v_row) / den, 0.0)
    for b in range(DEC_SEQS):
        r = i * DEC_SEQS + b
        for g in range(GROUP):
            o_s[pl.ds(r, 1), g * KV_W:(g + 1) * KV_W] = jnp.sum(
                om[b, g * N_KV:(g + 1) * N_KV, :], axis=0, keepdims=True)

    @pl.when(i == last)
    def _():
        att = _dot(o_s[...].astype(BF16), wao_ref[...])
        gate_b = jax.nn.sigmoid(_dot(h_s[...], win_ref[:, C_GB:C_END]))
        merged = (ag_s[...] + gate_b * att).astype(BF16)
        y_ref[...] = x_ref[...] + _rms(_dot(merged, wo_ref[...]),
                                       g_ref[3:4, :])


def _mixer_decode(x, st, kc, vc, norm_g, w_in, w_q, conv_w, w_co, w_ao, w_o,
                  bdec, sink_col, layer, prev_caches):
    n = x.shape[0]
    sb = DEC_SEQS
    cache_spec = pl.BlockSpec((None, sb, KV_W, WINDOW),
                              lambda i: (layer, i, 0, 0))
    n_alias = len(prev_caches)
    n_in = 13
    return pl.pallas_call(
        functools.partial(_mixer_decode_kernel, n_alias=n_alias),
        grid=(n // sb,),
        in_specs=[
            _resident((n, D_MODEL)),
            _resident((n, 2 * D_MODEL)),
            cache_spec,
            cache_spec,
            _layer_resident((N_NORMS, D_MODEL), layer),
            _resident((D_MODEL, C_END)),
            _resident((D_MODEL, D_MODEL)),
            _layer_resident((3, D_MODEL), layer),
            _resident((D_MODEL, D_MODEL)),
            _resident((D_MODEL, D_MODEL)),
            _resident((D_MODEL, D_MODEL)),
            _resident((N_HEADS, WINDOW)),
            _resident((N_HEADS, 1)),
        ] + [pl.BlockSpec(memory_space=pl.ANY)] * n_alias,
        out_specs=[
            pl.BlockSpec((n, D_MODEL), lambda i: (0, 0)),
            pl.BlockSpec((n, 2 * D_MODEL), lambda i: (0, 0)),
            cache_spec,
            cache_spec,
        ],
        out_shape=[
            jax.ShapeDtypeStruct((n, D_MODEL), F32),
            jax.ShapeDtypeStruct((n, 2 * D_MODEL), F32),
            jax.ShapeDtypeStruct(kc.shape, F32),
            jax.ShapeDtypeStruct(vc.shape, F32),
        ],
        input_output_aliases={n_in + a: 2 + a for a in range(n_alias)},
        scratch_shapes=[
            pltpu.VMEM((n, D_MODEL), BF16),
            pltpu.VMEM((n, D_MODEL), F32),
            pltpu.VMEM((n, KV_W), F32),
            pltpu.VMEM((n, KV_W), F32),
            pltpu.VMEM((KV_W, n), F32),
            pltpu.VMEM((KV_W, n), F32),
            pltpu.VMEM((n, D_MODEL), F32),
            pltpu.VMEM((n, D_MODEL), F32),
        ],
        compiler_params=pltpu.CompilerParams(
            dimension_semantics=("arbitrary",),
            vmem_limit_bytes=VMEM_LIMIT_BYTES),
        name="mixer_decode",
    )(x, st, kc, vc, norm_g, w_in, w_q, conv_w, w_co, w_ao, w_o, bdec,
      sink_col, *prev_caches)


def kernel(x_prompt, x_sample, state_conv, cache_k_win, cache_v_win, rel_bias,
           norm_g, w_ff1_gu, w_ff1_down, w_in, conv_w, sinks, w_conv_out,
           w_attn_out, w_out, w_ff2_gu, w_ff2_down):
    batch, seq, _ = x_prompt.shape
    n_dec = x_sample.shape[0]

    ffn_steps = (batch * seq) // FFN_ROWS
    mix_steps = (batch * seq) // MIX_ROWS
    bt_plain, bt, ffn_w = _bias_table(
        rel_bias, _ffn_cast_jobs(w_ff1_gu, w_ff1_down, 0, BIAS_STEPS))
    bdec = bt_plain[0].reshape(GROUP, BLOCK, N_KV, BLOCK)[:, BLOCK - 1]
    bdec = bdec.reshape(N_HEADS, WINDOW)
    bdec = jnp.concatenate([bdec[:, WINDOW - 1:], bdec[:, :WINDOW - 1]], axis=1)

    def keys_minor(c):
        return jnp.transpose(c, (0, 1, 3, 4, 2)).reshape(
            DEPTH, n_dec, KV_W, WINDOW)

    def keys_major(c):
        return jnp.transpose(
            c.reshape(DEPTH, n_dec, N_KV, HEAD_DIM, WINDOW), (0, 1, 4, 2, 3))

    kct = keys_minor(cache_k_win)
    vct = keys_minor(cache_v_win)

    xp = x_prompt.reshape(batch * seq, D_MODEL)
    xs = x_sample.reshape(n_dec, D_MODEL)
    pc, pk, pv, sc = [], [], [], []
    new_caches = ()
    for l in range(DEPTH):
        sink_col = sinks[l].reshape(N_KV, GROUP).T.reshape(N_HEADS, 1)

        xp, xs, mix_w = _ffn(
            xp, xs, norm_g, ffn_w[0], ffn_w[1], l, 0,
            _mixer_cast_jobs(w_in, w_conv_out, w_attn_out, w_out, l,
                             ffn_steps))
        win, wco, wao, wo, wq = mix_w

        xp, c1, k1, v1, ffn_w = _mixer_prompt(
            xp, batch, norm_g, win, wq, conv_w, wco, wao, wo, bt, sinks, l,
            _ffn_cast_jobs(w_ff2_gu, w_ff2_down, l, mix_steps))
        xs, c2, k2, v2 = _mixer_decode(
            xs, state_conv[l].reshape(n_dec, 2 * D_MODEL), kct, vct,
            norm_g, win, wq, conv_w, wco, wao, wo, bdec, sink_col, l,
            new_caches)
        new_caches = (k2, v2)

        next_jobs = (_ffn_cast_jobs(w_ff1_gu, w_ff1_down, l + 1, ffn_steps)
                     if l + 1 < DEPTH else [])
        xp, xs, ffn_w = _ffn(xp, xs, norm_g, ffn_w[0], ffn_w[1], l, 4,
                             next_jobs)

        pc.append(c1)
        pk.append(k1.reshape(batch, WINDOW, N_KV, HEAD_DIM))
        pv.append(v1.reshape(batch, WINDOW, N_KV, HEAD_DIM))
        sc.append(c2.reshape(n_dec, 2, D_MODEL))

    return (xp.reshape(batch, seq, D_MODEL), xs.reshape(n_dec, 1, D_MODEL),
            jnp.stack(pc), jnp.stack(pk), jnp.stack(pv),
            jnp.stack(sc), keys_major(new_caches[0]),
            keys_major(new_caches[1]))
```
